```python
import math
import functools
import jax
import jax.numpy as jnp
from jax import lax
import numpy as np

D_MODEL = 2048
BATCH = 4
SEQ = 4096
DEPTH = 2

GRID_W = 64
CTX_LEN = 256
MIX_HALF = D_MODEL // 2
D_FF = ((8 * D_MODEL // 3 + 127) // 128) * 128
FFN_RES = 0.5
N_MOD = 9
CHUNK = 64
EPS = 1e-6

GLA_HEADS = 4
GLA_DV = MIX_HALF // GLA_HEADS
GLA_DK = GLA_DV // 2
GLA_QK = GLA_HEADS * GLA_DK
GLA_RANK = 16
GLA_GATE_NORM = 16.0

MLSTM_HEADS = 4
MLSTM_DH = MIX_HALF // MLSTM_HEADS
MLSTM_CONV = 3

S5_CHANNELS = MIX_HALF
S5_GROUP = 16
S5_GROUPS = S5_CHANNELS // S5_GROUP
S5_STATE = 64

RET_HEADS = 4
RET_DH = MIX_HALF // RET_HEADS

N_EVEN = (DEPTH + 1) // 2
N_ODD = DEPTH // 2

EV_SIZES = (GLA_QK, GLA_QK, MIX_HALF, MIX_HALF, 2 * GLA_RANK, MIX_HALF, MIX_HALF, MIX_HALF, MIX_HALF, 4 * MLSTM_HEADS)
EV_COLS = sum(EV_SIZES)
OD_SIZES = (S5_CHANNELS, MIX_HALF, MIX_HALF, MIX_HALF, MIX_HALF)
OD_COLS = sum(OD_SIZES)

kernel_name = "hybrid_gla_mlstm_s5_retention_dit"


def rms_norm(x, g):
    xf = x.astype(jnp.float32)
    y = xf * lax.rsqrt(jnp.mean(xf * xf, axis=-1, keepdims=True) + EPS)
    return (y * g.astype(jnp.float32)).astype(x.dtype)


def head_norm(o, g, center):
    b, h, n, d = o.shape
    o = o.transpose(0, 2, 1, 3)
    if center:
        o = o - jnp.mean(o, axis=-1, keepdims=True)
    o = o * lax.rsqrt(jnp.mean(o * o, axis=-1, keepdims=True) + EPS)
    return o.reshape(b, n, h * d) * g


def sublayer_in(h, g, shift, scale):
    return rms_norm(h, g) * (1 + scale) + shift


def sublayer_out(y, g, gate, weight):
    return weight * gate * rms_norm(y, g)


def swiglu(h, w_gate, w_up, w_down):
    return (jax.nn.silu(h @ w_gate) * (h @ w_up)) @ w_down


def split_cols(p, sizes):
    out, start = [], 0
    for s in sizes:
        out.append(p[..., start:start + s])
        start += s
    return out


def split_heads(t, heads):
    b, n, _ = t.shape
    return t.reshape(b, n, heads, -1).transpose(0, 2, 1, 3)


def to_col_major(t, rows):
    b, rest = t.shape[0], t.shape[2:]
    return t.reshape(b, rows, GRID_W, *rest).swapaxes(1, 2).reshape(b, rows * GRID_W, *rest)


def to_row_major(t, rows):
    b, rest = t.shape[0], t.shape[2:]
    return t.reshape(b, GRID_W, rows, *rest).swapaxes(1, 2).reshape(b, rows * GRID_W, *rest)


def depthwise_conv(u, w, bias):
    out = lax.conv_general_dilated(u, w[:, None, :].astype(u.dtype), window_strides=(1,), padding='SAME',
                                   dimension_numbers=('NWC', 'WIO', 'NWC'), feature_group_count=u.shape[-1])
    return out + bias


def to_chunks(t):
    b, h, n = t.shape[:3]
    return jnp.moveaxis(t.reshape(b, h, n // CHUNK, CHUNK, *t.shape[3:]), 2, 0)


def from_chunks(t):
    t = jnp.moveaxis(t, 0, 2)
    return t.reshape(t.shape[0], t.shape[1], t.shape[2] * t.shape[3], *t.shape[4:])


def gla_scan(q, k, v, log_a, s0):
    mask = jnp.tril(jnp.ones((CHUNK, CHUNK), dtype=bool))
    mid = CHUNK // 2

    def step(s, inp):
        qc, kc, vc, ac = inp
        b = jnp.cumsum(ac, axis=2)
        b_mid = b[:, :, mid:mid + 1]
        b_end = b[:, :, -1:]
        scores = jnp.einsum('bhtd,bhsd->bhts', qc * jnp.exp(b - b_mid), kc * jnp.exp(b_mid - b))
        scores = jnp.where(mask, scores, 0.0)
        o = (jnp.einsum('bhts,bhsv->bhtv', scores, vc)
             + jnp.einsum('bhtd,bhdv->bhtv', qc * jnp.exp(b), s))
        s_new = (jnp.exp(b_end[:, :, 0])[..., None] * s
                 + jnp.einsum('bhsd,bhsv->bhdv', kc * jnp.exp(b_end - b), vc))
        return s_new, o

    s_fin, o = lax.scan(step, s0, (to_chunks(q), to_chunks(k), to_chunks(v), to_chunks(log_a)))
    return from_chunks(o), s_fin


def mlstm_scan(q, k, v, ig, lf, state):
    mask = jnp.tril(jnp.ones((CHUNK, CHUNK), dtype=bool))

    def step(carry, inp):
        c_st, n_st, m_st = carry
        qc, kc, vc, ic, fc = inp
        b = jnp.cumsum(fc, axis=-1)
        dmat = jnp.where(mask, b[..., :, None] - b[..., None, :] + ic[..., None, :], -jnp.inf)
        inter = b + m_st[..., None]
        m_t = jnp.maximum(inter, jnp.max(dmat, axis=-1))
        scores = jnp.einsum('bhtd,bhsd->bhts', qc, kc) * jnp.exp(dmat - m_t[..., None])
        w_inter = jnp.exp(inter - m_t)
        num = (jnp.einsum('bhts,bhsv->bhtv', scores, vc)
               + w_inter[..., None] * jnp.einsum('bhtd,bhdv->bhtv', qc, c_st))
        den = jnp.sum(scores, axis=-1) + w_inter * jnp.einsum('bhtd,bhd->bht', qc, n_st)
        h = num / jnp.maximum(jnp.abs(den), jnp.exp(-m_t))[..., None]
        b_end = b[..., -1]
        d_end = b_end[..., None] - b + ic
        m_new = jnp.maximum(b_end + m_st, jnp.max(d_end, axis=-1))
        wk = jnp.exp(d_end - m_new[..., None])
        decay = jnp.exp(b_end + m_st - m_new)
        c_new = decay[..., None, None] * c_st + jnp.einsum('bhsd,bhsv->bhdv', kc * wk[..., None], vc)
        n_new = decay[..., None] * n_st + jnp.einsum('bhsd,bhs->bhd', kc, wk)
        return (c_new, n_new, m_new), h

    fin, h = lax.scan(step, state, (to_chunks(q), to_chunks(k), to_chunks(v), to_chunks(ig), to_chunks(lf)))
    return from_chunks(h), fin


def complex_affine_combine(e1, e2):
    a1r, a1i, b1r, b1i = e1
    a2r, a2i, b2r, b2i = e2
    return (a2r * a1r - a2i * a1i, a2r * a1i + a2i * a1r,
            a2r * b1r - a2i * b1i + b2r, a2r * b1i + a2i * b1r + b2i)


def s5_direction(lam_re, lam_im, log_step, b_re, b_im, c_re, c_im, u, state):
    lam_re = lam_re.astype(jnp.float32)
    lam_im = lam_im.astype(jnp.float32)
    step = jnp.exp(log_step.astype(jnp.float32))[:, None]
    mag = jnp.exp(lam_re * step)
    ab_re, ab_im = mag * jnp.cos(lam_im * step), mag * jnp.sin(lam_im * step)
    den = lam_re * lam_re + lam_im * lam_im
    coef_re = ((ab_re - 1.0) * lam_re + ab_im * lam_im) / den
    coef_im = (ab_im * lam_re - (ab_re - 1.0) * lam_im) / den
    bu_re = jnp.einsum('bngp,gsp->bngs', u, b_re)
    bu_im = jnp.einsum('bngp,gsp->bngs', u, b_im)
    x_re = coef_re * bu_re - coef_im * bu_im
    x_im = coef_re * bu_im + coef_im * bu_re
    a_re = jnp.broadcast_to(ab_re, x_re.shape)
    a_im = jnp.broadcast_to(ab_im, x_re.shape)
    acc_re, acc_im, hs_re, hs_im = lax.associative_scan(complex_affine_combine, (a_re, a_im, x_re, x_im), axis=1)
    h0_re, h0_im = state[0][:, None], state[1][:, None]
    h_re = hs_re + acc_re * h0_re - acc_im * h0_im
    h_im = hs_im + acc_re * h0_im + acc_im * h0_re
    y = jnp.einsum('bngs,gps->bngp', h_re, c_re) - jnp.einsum('bngs,gps->bngp', h_im, c_im)
    return y, (h_re[:, -1], h_im[:, -1])


def bidirectional(scan_f, scan_b, ctx_f, lat_f, ctx_b, lat_b, state0, axis):
    flip = lambda t: jnp.flip(t, axis=axis)
    oc_f, sc_f = scan_f(*ctx_f, state0)
    ol_f, _ = scan_f(*lat_f, sc_f)
    oc_b, sc_b = scan_b(*[flip(t) for t in ctx_b], state0)
    ol_b, _ = scan_b(*[flip(t) for t in lat_b], sc_b)
    return oc_f + flip(oc_b), ol_f + flip(ol_b)


def even_mixer(hx, hc, rows, need_ctx, w_in, w_out, gla_w_gate, gla_b_gate, gla_norm,
               ml_conv_w, ml_conv_b, ml_b_gates, ml_norm):
    bsz = hx.shape[0]
    sx = split_cols((hx @ w_in).astype(jnp.float32), EV_SIZES)
    sc = split_cols((hc @ w_in).astype(jnp.float32), EV_SIZES)

    def gla_prep(s):
        q, k, v, _, low_rank = s[:5]
        base = (split_heads(q, GLA_HEADS) * GLA_DK ** -0.5, split_heads(k, GLA_HEADS), split_heads(v, GLA_HEADS))
        dirs = []
        for d in range(2):
            z = low_rank[..., d * GLA_RANK:(d + 1) * GLA_RANK] @ gla_w_gate[d] + gla_b_gate[d]
            dirs.append(base + (split_heads(jax.nn.log_sigmoid(z) / GLA_GATE_NORM, GLA_HEADS),))
        return dirs

    gc, gx = gla_prep(sc), gla_prep(sx)
    s0 = jnp.zeros((bsz, GLA_HEADS, GLA_DK, GLA_DV), jnp.float32)
    oc_a, ox_a = bidirectional(gla_scan, gla_scan, gc[0], gx[0], gc[1], gx[1], s0, axis=2)

    def ml_prep(s, reorder):
        q, k, v, _, g = s[5:]
        if reorder:
            q, k, v, g = (to_col_major(t, rows) for t in (q, k, v, g))
        qk = jax.nn.silu(depthwise_conv(jnp.concatenate([q, k], axis=-1), ml_conv_w, ml_conv_b))
        base = (split_heads(qk[..., :MIX_HALF], MLSTM_HEADS),
                split_heads(qk[..., MIX_HALF:], MLSTM_HEADS) * MLSTM_DH ** -0.5,
                split_heads(v, MLSTM_HEADS))
        gates = (g.reshape(g.shape[0], g.shape[1], 2, 2, MLSTM_HEADS) + ml_b_gates).transpose(2, 3, 0, 4, 1)
        return [base + (gates[d, 0], jax.nn.log_sigmoid(gates[d, 1])) for d in range(2)]

    mc, mx = ml_prep(sc, False), ml_prep(sx, True)
    m0 = (jnp.zeros((bsz, MLSTM_HEADS, MLSTM_DH, MLSTM_DH), jnp.float32),
          jnp.zeros((bsz, MLSTM_HEADS, MLSTM_DH), jnp.float32),
          jnp.zeros((bsz, MLSTM_HEADS), jnp.float32))
    oc_b, ox_b = bidirectional(mlstm_scan, mlstm_scan, mc[0], mx[0], mc[1], mx[1], m0, axis=2)

    def merge(o_a, o_b, s, reorder):
        a = jax.nn.silu(s[3]) * head_norm(o_a, gla_norm, False)
        b = head_norm(o_b, ml_norm, True)
        if reorder:
            b = to_row_major(b, rows)
        b = jax.nn.sigmoid(s[8]) * b
        return jnp.concatenate([a, b], axis=-1).astype(hx.dtype) @ w_out

    yx = merge(ox_a, ox_b, sx, True)
    yc = merge(oc_a, oc_b, sc, False) if need_ctx else None
    return yx, yc


def odd_mixer(hx, hc, rows, need_ctx, w_in, w_out, lam_re, lam_im, log_step, b_re, b_im, c_re, c_im,
              s5_d, w_glu, b_glu, ret_log_decay, ret_norm):
    bsz = hx.shape[0]
    sx = split_cols((hx @ w_in).astype(jnp.float32), OD_SIZES)
    sc = split_cols((hc @ w_in).astype(jnp.float32), OD_SIZES)

    scans = [functools.partial(s5_direction, lam_re[d], lam_im[d], log_step[d], b_re[d], b_im[d], c_re[d], c_im[d])
             for d in range(2)]
    grp = lambda u: u.reshape(u.shape[0], u.shape[1], S5_GROUPS, S5_GROUP)
    uc, ux = grp(sc[0]), grp(sx[0])
    h0 = (jnp.zeros((bsz, S5_GROUPS, S5_STATE), jnp.float32), jnp.zeros((bsz, S5_GROUPS, S5_STATE), jnp.float32))
    oc_c, ox_c = bidirectional(scans[0], scans[1], (uc,), (ux,), (uc,), (ux,), h0, axis=1)

    log_gamma = -jnp.exp(ret_log_decay.astype(jnp.float32))

    def ret_prep(s, reorder):
        q, k, v = s[1:4]
        if reorder:
            q, k, v = (to_col_major(t, rows) for t in (q, k, v))
        base = (split_heads(q, RET_HEADS), split_heads(k, RET_HEADS) * RET_DH ** -0.5, split_heads(v, RET_HEADS))
        shape = base[0].shape
        return [base + (jnp.broadcast_to(log_gamma[d][None, :, None, None], shape),) for d in range(2)]

    rc, rx = ret_prep(sc, False), ret_prep(sx, True)
    r0 = jnp.zeros((bsz, RET_HEADS, RET_DH, RET_DH), jnp.float32)
    oc_d, ox_d = bidirectional(gla_scan, gla_scan, rc[0], rx[0], rc[1], rx[1], r0, axis=2)

    def merge(o_c, o_d, s, reorder):
        y = o_c.reshape(s[0].shape) + s5_d * s[0]
        g = jax.nn.gelu(y)
        a = g * jax.nn.sigmoid(g @ w_glu + b_glu)
        b = head_norm(o_d, ret_norm, True)
        if reorder:
            b = to_row_major(b, rows)
        b = jax.nn.silu(s[4]) * b
        return jnp.concatenate([a, b], axis=-1).astype(hx.dtype) @ w_out

    yx = merge(ox_c, ox_d, sx, True)
    yc = merge(oc_c, oc_d, sc, False) if need_ctx else None
    return yx, yc


def setup_inputs(seed: int = 0) -> dict:
    key = jax.random.key(seed)
    keys = iter(jax.random.split(key, 48))

    def nrm(shape, scale):
        return jax.random.normal(next(keys), shape, jnp.float32) * scale

    D = D_MODEL
    gate_bias = jnp.stack([jnp.zeros((MLSTM_HEADS,), jnp.float32),
                           jnp.linspace(3.0, 6.0, MLSTM_HEADS, dtype=jnp.float32)])
    ret_base = jnp.log(-jnp.log(1.0 - 2.0 ** (-5.0 - jnp.arange(RET_HEADS, dtype=jnp.float32))))
    return {
        'x': nrm((BATCH, SEQ, D), 1.0),
        'c': nrm((BATCH, D), 1.0),
        'ctx': nrm((BATCH, CTX_LEN, D), 1.0),
        'c_ctx': nrm((D,), 1.0),
        'w_mod': nrm((DEPTH, D, N_MOD * D), 0.5 * D ** -0.5),
        'b_mod': nrm((DEPTH, N_MOD * D), 0.02),
        'norm_pre': 1.0 + nrm((DEPTH, 3, D), 0.02),
        'norm_post': 1.0 + nrm((DEPTH, 3, D), 0.02),
        'ffn_w_gate': nrm((DEPTH, 2, D, D_FF), D ** -0.5),
        'ffn_w_up': nrm((DEPTH, 2, D, D_FF), D ** -0.5),
        'ffn_w_down': nrm((DEPTH, 2, D_FF, D), D_FF ** -0.5),
        'ev_w_in': nrm((N_EVEN, D, EV_COLS), D ** -0.5),
        'ev_w_out': nrm((N_EVEN, 2 * MIX_HALF, D), (2 * MIX_HALF) ** -0.5),
        'gla_w_gate': nrm((N_EVEN, 2, GLA_RANK, GLA_QK), GLA_RANK ** -0.5),
        'gla_b_gate': 2.0 + nrm((N_EVEN, 2, GLA_QK), 0.1),
        'gla_norm': 1.0 + nrm((N_EVEN, MIX_HALF), 0.02),
        'ml_conv_w': nrm((N_EVEN, MLSTM_CONV, 2 * MIX_HALF), MLSTM_CONV ** -0.5),
        'ml_conv_b': nrm((N_EVEN, 2 * MIX_HALF), 0.02),
        'ml_b_gates': gate_bias + nrm((N_EVEN, 2, 2, MLSTM_HEADS), 0.1),
        'ml_norm': 1.0 + nrm((N_EVEN, MIX_HALF), 0.02),
        'od_w_in': nrm((N_ODD, D, OD_COLS), D ** -0.5),
        'od_w_out': nrm((N_ODD, 2 * MIX_HALF, D), (2 * MIX_HALF) ** -0.5),
        's5_lam_re': -0.5 + nrm((N_ODD, 2, S5_GROUPS, S5_STATE), 0.01),
        's5_lam_im': jnp.pi * jnp.arange(S5_STATE, dtype=jnp.float32) + nrm((N_ODD, 2, S5_GROUPS, S5_STATE), 0.01),
        's5_log_step': jax.random.uniform(next(keys), (N_ODD, 2, S5_GROUPS), jnp.float32,
                                          math.log(1e-3), math.log(1e-1)),
        's5_b_re': nrm((N_ODD, 2, S5_GROUPS, S5_STATE, S5_GROUP), (2 * S5_GROUP) ** -0.5),
        's5_b_im': nrm((N_ODD, 2, S5_GROUPS, S5_STATE, S5_GROUP), (2 * S5_GROUP) ** -0.5),
        's5_c_re': nrm((N_ODD, 2, S5_GROUPS, S5_GROUP, S5_STATE), 0.5),
        's5_c_im': nrm((N_ODD, 2, S5_GROUPS, S5_GROUP, S5_STATE), 0.5),
        's5_d': nrm((N_ODD, S5_CHANNELS), 0.5),
        's5_w_glu': nrm((N_ODD, S5_CHANNELS, S5_CHANNELS), S5_CHANNELS ** -0.5),
        's5_b_glu': nrm((N_ODD, S5_CHANNELS), 0.02),
        'ret_log_decay': ret_base + nrm((N_ODD, 2, RET_HEADS), 0.05),
        'ret_norm': 1.0 + nrm((N_ODD, MIX_HALF), 0.02),
    }


def reference(x, c, ctx, c_ctx, w_mod, b_mod, norm_pre, norm_post, ffn_w_gate, ffn_w_up, ffn_w_down,
              ev_w_in, ev_w_out, gla_w_gate, gla_b_gate, gla_norm, ml_conv_w, ml_conv_b, ml_b_gates, ml_norm,
              od_w_in, od_w_out, s5_lam_re, s5_lam_im, s5_log_step, s5_b_re, s5_b_im, s5_c_re, s5_c_im,
              s5_d, s5_w_glu, s5_b_glu, ret_log_decay, ret_norm):
    rows = x.shape[1] // GRID_W
    for layer in range(DEPTH):
        need_ctx = layer < DEPTH - 1
        mod_x = (jax.nn.silu(c) @ w_mod[layer] + b_mod[layer]).reshape(c.shape[0], N_MOD, 1, D_MODEL)
        mod_c = (jax.nn.silu(c_ctx) @ w_mod[layer] + b_mod[layer]).reshape(N_MOD, D_MODEL)
        mx = [mod_x[:, j] for j in range(N_MOD)]
        mc = [mod_c[j] for j in range(N_MOD)]
        g_pre, g_post = norm_pre[layer], norm_post[layer]
        wg, wu, wd = ffn_w_gate[layer], ffn_w_up[layer], ffn_w_down[layer]

        x = x + sublayer_out(swiglu(sublayer_in(x, g_pre[0], mx[0], mx[1]), wg[0], wu[0], wd[0]), g_post[0], mx[2], FFN_RES)
        ctx = ctx + sublayer_out(swiglu(sublayer_in(ctx, g_pre[0], mc[0], mc[1]), wg[0], wu[0], wd[0]), g_post[0], mc[2], FFN_RES)

        hx = sublayer_in(x, g_pre[1], mx[3], mx[4])
        hc = sublayer_in(ctx, g_pre[1], mc[3], mc[4])
        if layer % 2 == 0:
            e = layer // 2
            yx, yc = even_mixer(hx, hc, rows, need_ctx, ev_w_in[e], ev_w_out[e], gla_w_gate[e], gla_b_gate[e],
                                gla_norm[e], ml_conv_w[e], ml_conv_b[e], ml_b_gates[e], ml_norm[e])
        else:
            o = layer // 2
            yx, yc = odd_mixer(hx, hc, rows, need_ctx, od_w_in[o], od_w_out[o], s5_lam_re[o], s5_lam_im[o],
                               s5_log_step[o], s5_b_re[o], s5_b_im[o], s5_c_re[o], s5_c_im[o], s5_d[o],
                               s5_w_glu[o], s5_b_glu[o], ret_log_decay[o], ret_norm[o])
        x = x + sublayer_out(yx, g_post[1], mx[5], 1.0)

        x = x + sublayer_out(swiglu(sublayer_in(x, g_pre[2], mx[6], mx[7]), wg[1], wu[1], wd[1]), g_post[2], mx[8], FFN_RES)
        if need_ctx:
            ctx = ctx + sublayer_out(yc, g_post[1], mc[5], 1.0)
            ctx = ctx + sublayer_out(swiglu(sublayer_in(ctx, g_pre[2], mc[6], mc[7]), wg[1], wu[1], wd[1]), g_post[2], mc[8], FFN_RES)
    return x
```

```python
import functools

import jax
import jax.numpy as jnp
from jax import lax
from jax.experimental import pallas as pl
from jax.experimental.pallas import tpu as pltpu

F32 = jnp.float32
BF16 = jnp.bfloat16

EPS = 1e-6
GRID_W = 64
CHUNK = 64
FFN_RES = 0.5
N_MOD = 9
HEADS = 4
GLA_RANK = 16
GLA_GATE_NORM = 16.0
S5_GROUP = 16
S5_STATE = 64
S5_CHUNK = 16

TOKEN_TILE = 512
MERGE_TILE = 256
FF_TILE = 512
PROJ_TILE = 1024
MOD_TILE = 1024
CONV_TILE = 256
S5_LANES = 512
VMEM_LIMIT = 56 * 1024 * 1024


def _cparams(sem):
    return pltpu.CompilerParams(dimension_semantics=sem, vmem_limit_bytes=VMEM_LIMIT)


def _rms(x):
    return x * lax.rsqrt(jnp.mean(x * x, axis=-1, keepdims=True) + EPS)


def _sigmoid(x):
    return 1.0 / (1.0 + jnp.exp(-x))


def _silu(x):
    return x * _sigmoid(x)


def _log_sigmoid(x):
    return jnp.minimum(x, 0.0) - jnp.log(1.0 + jnp.exp(-jnp.abs(x)))


def _dot(a, b):
    return jnp.dot(a, b, preferred_element_type=F32)


def _dot_nt(a, b):
    return lax.dot_general(a, b, (((1,), (1,)), ((), ())), preferred_element_type=F32)


def _dot_tn(a, b):
    return lax.dot_general(a, b, (((0,), (0,)), ((), ())), preferred_element_type=F32)


def _dot01(t01, x):
    x0 = x.astype(BF16)
    r1 = x - x0.astype(F32)
    x1 = r1.astype(BF16)
    x2 = (r1 - x1.astype(F32)).astype(BF16)
    return _dot(t01, x0) + _dot(t01, x1) + _dot(t01, x2)


def _mod_kernel(c_ref, w_ref, b_ref, o_ref):
    a = _silu(c_ref[...]).astype(BF16)
    o_ref[...] = _dot(a, w_ref[...].astype(BF16)) + b_ref[...]


def _modulation(cc, w_mod, b_mod):
    depth, d, nm = w_mod.shape
    return pl.pallas_call(
        _mod_kernel,
        out_shape=jax.ShapeDtypeStruct((depth, 8, nm), F32),
        grid=(depth, nm // MOD_TILE),
        in_specs=[pl.BlockSpec((8, d), lambda l, j: (0, 0)),
                  pl.BlockSpec((None, d, MOD_TILE), lambda l, j: (l, 0, j)),
                  pl.BlockSpec((None, 1, MOD_TILE), lambda l, j: (l, 0, j))],
        out_specs=pl.BlockSpec((None, 8, MOD_TILE), lambda l, j: (l, 0, j)),
        compiler_params=_cparams(("parallel", "parallel")),
        name="modulation",
    )(cc, w_mod, b_mod.reshape(depth, 1, nm))


def _ffn_kernel(x_ref, mod_ref, gpre_ref, gpost_ref, wg_ref, wu_ref, wd_ref, o_ref, h_ref, acc_ref, *, sub, nff):
    j = pl.program_id(1)

    @pl.when(j == 0)
    def _():
        shift = mod_ref[3 * sub:3 * sub + 1, :]
        scale = mod_ref[3 * sub + 1:3 * sub + 2, :]
        h = _rms(x_ref[...]) * gpre_ref[sub:sub + 1, :] * (1.0 + scale) + shift
        h_ref[...] = h.astype(BF16)
        acc_ref[...] = jnp.zeros_like(acc_ref)

    h = h_ref[...]
    a = _dot(h, wg_ref[...])
    u = _dot(h, wu_ref[...])
    t = (_silu(a) * u).astype(BF16)
    acc_ref[...] += _dot(t, wd_ref[...])

    @pl.when(j == nff - 1)
    def _():
        gate = mod_ref[3 * sub + 2:3 * sub + 3, :]
        o_ref[...] = x_ref[...] + FFN_RES * gate * _rms(acc_ref[...]) * gpost_ref[sub:sub + 1, :]


def _ffn(s, mod, gpre, gpost, wg, wu, wd, *, layer, which, sub, n_tiles, mod_of_tile):
    d = s.shape[1]
    ffp = wg.shape[-1]
    nff = ffp // FF_TILE
    tm = TOKEN_TILE
    return pl.pallas_call(
        functools.partial(_ffn_kernel, sub=sub, nff=nff),
        out_shape=jax.ShapeDtypeStruct((n_tiles * tm, d), F32),
        grid=(n_tiles, nff),
        in_specs=[pl.BlockSpec((tm, d), lambda i, j: (i, 0)),
                  pl.BlockSpec((None, N_MOD, d), lambda i, j: (mod_of_tile(i), 0, 0)),
                  pl.BlockSpec((3, d), lambda i, j: (0, 0)),
                  pl.BlockSpec((3, d), lambda i, j: (0, 0)),
                  pl.BlockSpec((None, None, d, FF_TILE), lambda i, j: (layer, which, 0, j)),
                  pl.BlockSpec((None, None, d, FF_TILE), lambda i, j: (layer, which, 0, j)),
                  pl.BlockSpec((None, None, FF_TILE, d), lambda i, j: (layer, which, j, 0))],
        out_specs=pl.BlockSpec((tm, d), lambda i, j: (i, 0)),
        scratch_shapes=[pltpu.VMEM((tm, d), BF16), pltpu.VMEM((tm, d), F32)],
        compiler_params=_cparams(("parallel", "arbitrary")),
        name="ffn",
    )(s, mod, gpre, gpost, wg, wu, wd)


def _proj_kernel(x_ref, mod_ref, gpre_ref, w_ref, o_ref, h_ref):
    @pl.when(pl.program_id(1) == 0)
    def _():
        h = _rms(x_ref[...]) * gpre_ref[1:2, :] * (1.0 + mod_ref[4:5, :]) + mod_ref[3:4, :]
        h_ref[...] = h.astype(BF16)

    o_ref[...] = _dot(h_ref[...], w_ref[...])


def _proj(s, mod, gpre, w, *, n_tiles, mod_of_tile):
    d = s.shape[1]
    npad = w.shape[1]
    tm = TOKEN_TILE
    return pl.pallas_call(
        _proj_kernel,
        out_shape=jax.ShapeDtypeStruct((n_tiles * tm, npad), F32),
        grid=(n_tiles, npad // PROJ_TILE),
        in_specs=[pl.BlockSpec((tm, d), lambda i, j: (i, 0)),
                  pl.BlockSpec((None, N_MOD, d), lambda i, j: (mod_of_tile(i), 0, 0)),
                  pl.BlockSpec((3, d), lambda i, j: (0, 0)),
                  pl.BlockSpec((d, PROJ_TILE), lambda i, j: (0, j))],
        out_specs=pl.BlockSpec((tm, PROJ_TILE), lambda i, j: (i, j)),
        scratch_shapes=[pltpu.VMEM((tm, d), BF16)],
        compiler_params=_cparams(("parallel", "arbitrary")),
        name="in_proj",
    )(s, mod, gpre, w)


def _conv_kernel(x_ref, w_ref, b_ref, sc_ref, o_ref, *, n, stride):
    w0, w1, w2 = w_ref[0:1, :], w_ref[1:2, :], w_ref[2:3, :]
    bias, scale = b_ref[...], sc_ref[...]

    def emit(lo, hi, prev, nxt):
        y = w1 * x_ref[lo:hi, :] + w0 * prev + w2 * nxt + bias
        o_ref[lo:hi, :] = _silu(y) * scale

    if stride == 1:
        x = x_ref[...]
        r = lax.broadcasted_iota(jnp.int32, x.shape, 0)
        prev = jnp.where(r == 0, 0.0, pltpu.roll(x, 1, 0))
        nxt = jnp.where(r == n - 1, 0.0, pltpu.roll(x, n - 1, 0))
        emit(0, n, prev, nxt)
    else:
        w = stride
        r = lax.broadcasted_iota(jnp.int32, (w, x_ref.shape[1]), 0)
        top_prev = jnp.where(r == 0, 0.0, pltpu.roll(x_ref[n - w:n, :], 1, 0))
        bot_next = jnp.where(r == w - 1, 0.0, pltpu.roll(x_ref[0:w, :], w - 1, 0))
        emit(0, w, top_prev, x_ref[w:2 * w, :])
        emit(w, n - w, x_ref[0:n - 2 * w, :], x_ref[2 * w:n, :])
        emit(n - w, n, x_ref[n - 2 * w:n - w, :], bot_next)


def _conv(p, conv_w, conv_b, scale, *, bsz, seq, ctx_len, coloff):
    nt = p.shape[0]
    c = conv_w.shape[1]
    ct = CONV_TILE
    cb0 = coloff // ct
    consts = (conv_w, conv_b.reshape(1, c), scale.reshape(1, c))
    const_specs = [pl.BlockSpec((3, ct), lambda b, j: (0, j)),
                   pl.BlockSpec((1, ct), lambda b, j: (0, j)),
                   pl.BlockSpec((1, ct), lambda b, j: (0, j))]
    lat = pl.pallas_call(
        functools.partial(_conv_kernel, n=seq, stride=GRID_W),
        out_shape=jax.ShapeDtypeStruct((nt, c), F32),
        grid=(bsz, c // ct),
        in_specs=[pl.BlockSpec((seq, ct), lambda b, j: (b, cb0 + j))] + const_specs,
        out_specs=pl.BlockSpec((seq, ct), lambda b, j: (b, j)),
        compiler_params=_cparams(("parallel", "parallel")),
        name="conv_lat",
    )(p, *consts)
    ctx0 = bsz * seq // ctx_len
    return pl.pallas_call(
        lambda x_ref, w_ref, b_ref, sc_ref, prev_ref, o_ref: _conv_kernel(
            x_ref, w_ref, b_ref, sc_ref, o_ref, n=ctx_len, stride=1),
        out_shape=jax.ShapeDtypeStruct((nt, c), F32),
        grid=(bsz, c // ct),
        in_specs=[pl.BlockSpec((ctx_len, ct), lambda b, j: (ctx0 + b, cb0 + j))] + const_specs
        + [pl.BlockSpec(memory_space=pl.ANY)],
        out_specs=pl.BlockSpec((ctx_len, ct), lambda b, j: (ctx0 + b, j)),
        input_output_aliases={4: 0},
        compiler_params=_cparams(("parallel", "parallel")),
        name="conv_ctx",
    )(p, *consts, lat)


def _gla_kernel(*refs, L, dk, dv, gated, q_scale, k_scale):
    if gated:
        qf, kf, vf, df, qb, kb, vb, db, wgate_ref, bgate_ref, st_in, of, ob, st_out = refs
    else:
        qf, kf, vf, qb, kb, vb, lg_ref, st_in, of, ob, st_out = refs
        df = db = None

    @pl.when(pl.program_id(1) == 0)
    def _():
        st_out[...] = st_in[...]

    row = lax.broadcasted_iota(jnp.int32, (L, L), 0)
    col = lax.broadcasted_iota(jnp.int32, (L, L), 1)
    trow = lax.broadcasted_iota(jnp.int32, (L, 1), 0)
    for d, (q_ref, k_ref, v_ref, d_ref, o_ref) in enumerate(((qf, kf, vf, df, of), (qb, kb, vb, db, ob))):
        mask = (row >= col) if d == 0 else (row <= col)
        mid = L // 2 if d == 0 else L - 1 - L // 2
        end = L - 1 if d == 0 else 0
        if gated:
            low = d_ref[:, d * GLA_RANK:(d + 1) * GLA_RANK].astype(BF16)
            z = _dot(low, wgate_ref[d]) + bgate_ref[d:d + 1, :]
            la = _log_sigmoid(z) * (1.0 / GLA_GATE_NORM)
            bcum = _dot01(mask.astype(BF16), la)
        else:
            cnt = (trow + 1) if d == 0 else (L - trow)
            bcum = cnt.astype(F32) * lg_ref[d:d + 1, :]
        for h in range(HEADS):
            b = bcum[:, h * dk:(h + 1) * dk]
            b_mid = b[mid:mid + 1, :]
            b_end = b[end:end + 1, :]
            q = q_ref[:, h * dk:(h + 1) * dk] * q_scale
            k = k_ref[:, h * dk:(h + 1) * dk] * k_scale
            v = v_ref[:, h * dv:(h + 1) * dv].astype(BF16)
            qs = (q * jnp.exp(b - b_mid)).astype(BF16)
            ks = (k * jnp.exp(b_mid - b)).astype(BF16)
            scores = jnp.where(mask, _dot_nt(qs, ks), 0.0)
            st = st_out[d, h]
            o = _dot(scores.astype(BF16), v) + _dot_nt((q * jnp.exp(b)).astype(BF16), st.astype(BF16))
            o_ref[:, h * dv:(h + 1) * dv] = o
            kd = (k * jnp.exp(b_end - b)).astype(BF16)
            st_out[d, h] = jnp.exp(b_end) * st + _dot_tn(v, kd)


def _mlstm_kernel(qf, kf, vf, gf, qb, kb, vb, gb, bias_ref, c_in, nm_in, of, ob, c_out, nm_out, *, L, dh, gcol):
    @pl.when(pl.program_id(1) == 0)
    def _():
        c_out[...] = c_in[...]
        nm_out[...] = nm_in[...]

    row = lax.broadcasted_iota(jnp.int32, (L, L), 0)
    col = lax.broadcasted_iota(jnp.int32, (L, L), 1)
    eye = row == col
    ones = jnp.ones((L, L), BF16)
    for d, (q_ref, k_ref, v_ref, g_ref, o_ref) in enumerate(((qf, kf, vf, gf, of), (qb, kb, vb, gb, ob))):
        mask = (row >= col) if d == 0 else (row <= col)
        cum_us = (row <= col) if d == 0 else (row >= col)
        end = L - 1 if d == 0 else 0
        g = g_ref[...] + bias_ref[...]
        fl = _log_sigmoid(g)
        bc_all = _dot01(mask.astype(BF16), fl)
        for h in range(HEADS):
            ci = gcol + d * 2 * HEADS + h
            cf = ci + HEADS
            i_col = g[:, ci:ci + 1]
            f_col = fl[:, cf:cf + 1]
            b_col = bc_all[:, cf:cf + 1]
            x = jnp.where(cum_us, f_col, 0.0) - jnp.where(eye, i_col, 0.0)
            drow = _dot01(ones, x)
            m_st = nm_out[d, h, 1:2, 0:1]
            dmat = jnp.where(mask, b_col - drow, -jnp.inf)
            inter = b_col + m_st
            m_t = jnp.maximum(inter, jnp.max(dmat, axis=-1, keepdims=True))
            q = q_ref[:, h * dh:(h + 1) * dh]
            k = k_ref[:, h * dh:(h + 1) * dh]
            v = v_ref[:, h * dh:(h + 1) * dh].astype(BF16)
            qh = q.astype(BF16)
            scores = _dot_nt(qh, k.astype(BF16)) * jnp.exp(dmat - m_t)
            w_inter = jnp.exp(inter - m_t)
            c_st = c_out[d, h]
            n_st = nm_out[d, h, 0:1, :]
            num = _dot(scores.astype(BF16), v) + w_inter * _dot(qh, c_st.astype(BF16))
            den = (jnp.sum(scores, axis=-1, keepdims=True)
                   + w_inter * jnp.sum(q * n_st, axis=-1, keepdims=True))
            o_ref[:, h * dh:(h + 1) * dh] = num / jnp.maximum(jnp.abs(den), jnp.exp(-m_t))
            b_end = b_col[end:end + 1, :]
            d_end = b_end - b_col + i_col
            m_new = jnp.maximum(b_end + m_st, jnp.max(d_end, axis=0, keepdims=True))
            kw = k * jnp.exp(d_end - m_new)
            decay = jnp.exp(b_end + m_st - m_new)
            c_out[d, h] = decay * c_st + _dot_tn(kw.astype(BF16), v)
            nm_out[d, h, 0:1, :] = decay * n_st + jnp.sum(kw, axis=0, keepdims=True)
            nm_out[d, h, 1:2, :] = jnp.broadcast_to(m_new, (1, dh))


class _Seg:
    def __init__(self, *, bsz, seq, ctx_len, kind):
        self.kind = kind
        rows = seq // GRID_W
        if kind == "ctx":
            self.L, self.n = CHUNK, ctx_len // CHUNK
            self.base = lambda b: bsz * seq // CHUNK + b * (ctx_len // CHUNK)
        elif kind == "lat_rm":
            self.L, self.n = CHUNK, seq // CHUNK
            self.base = lambda b: b * (seq // CHUNK)
        else:
            self.L, self.n = rows, GRID_W

    def view(self, a):
        if self.kind == "lat_cm":
            return a.reshape(a.shape[0] // GRID_W, GRID_W * a.shape[1])
        return a

    def spec(self, width, ncols, coloff, d):
        n = self.n
        chunk = (lambda s: s) if d == 0 else (lambda s: n - 1 - s)
        if self.kind == "lat_cm":
            return pl.BlockSpec((self.L, width), lambda b, s: (b, (chunk(s) * ncols + coloff) // width))
        base = self.base
        return pl.BlockSpec((self.L, width), lambda b, s: (base(b) + chunk(s), coloff // width))


def _full_spec(a):
    nd = a.ndim
    return pl.BlockSpec(a.shape, lambda b, s: (0,) * nd)


def _state_spec(a):
    nd = a.ndim
    return pl.BlockSpec((None,) + a.shape[1:], lambda b, s: (b,) + (0,) * (nd - 1))


def _scan_call(kernel, seg, bsz, chunk_ins, consts, states, out_prev, out_width, name):
    nt = chunk_ins[0][0].shape[0]
    ins, in_specs = [], []
    for d in (0, 1):
        for a, width, coloff in chunk_ins:
            ins.append(seg.view(a))
            in_specs.append(seg.spec(width, a.shape[1], coloff, d))
    for a in consts:
        ins.append(a)
        in_specs.append(_full_spec(a))
    for a in states:
        ins.append(a)
        in_specs.append(_state_spec(a))
    out_struct = jax.ShapeDtypeStruct(_view_shape(seg, nt, out_width), F32)
    out_shape = [out_struct, out_struct] + [jax.ShapeDtypeStruct(a.shape, F32) for a in states]
    out_specs = [seg.spec(out_width, out_width, 0, 0), seg.spec(out_width, out_width, 0, 1)] \
        + [_state_spec(a) for a in states]
    aliases = {}
    if out_prev is not None:
        for i, a in enumerate(out_prev):
            aliases[len(ins)] = i
            ins.append(seg.view(a))
            in_specs.append(pl.BlockSpec(memory_space=pl.ANY))
        inner = kernel
        n_extra = len(out_prev)
        n_in = len(ins) - n_extra

        def kernel(*refs):
            inner(*refs[:n_in], *refs[n_in + n_extra:])

    res = pl.pallas_call(
        kernel,
        out_shape=out_shape,
        grid=(bsz, seg.n),
        in_specs=in_specs,
        out_specs=out_specs,
        input_output_aliases=aliases,
        compiler_params=_cparams(("parallel", "arbitrary")),
        name=name,
    )(*ins)
    o_f = res[0].reshape(nt, out_width)
    o_b = res[1].reshape(nt, out_width)
    return (o_f, o_b), list(res[2:])


def _view_shape(seg, nt, width):
    if seg.kind == "lat_cm":
        return (nt // GRID_W, GRID_W * width)
    return (nt, width)


def _bidir_scan(make_kernel, lat_kind, chunk_ins, consts, states0, *, bsz, seq, ctx_len, out_width, name):
    seg_c = _Seg(bsz=bsz, seq=seq, ctx_len=ctx_len, kind="ctx")
    seg_l = _Seg(bsz=bsz, seq=seq, ctx_len=ctx_len, kind=lat_kind)
    outs, st = _scan_call(make_kernel(seg_c.L), seg_c, bsz, chunk_ins, consts, states0, None, out_width,
                          name + "_ctx")
    outs, _ = _scan_call(make_kernel(seg_l.L), seg_l, bsz, chunk_ins, consts, st, outs, out_width,
                         name + "_lat")
    return outs


def _s5_state_kernel(x_ref, q_ref, zre_ref, zim_ref, *, bsz):
    x = x_ref[...]
    z0 = _dot(x, q_ref[0])
    z1 = _dot(x, q_ref[1])
    r = lax.broadcasted_iota(jnp.int32, z0.shape, 0)
    z = jnp.where((r // bsz) % 2 == 0, z0, z1)
    half = z.shape[1] // 2
    zre_ref[...] = z[:, :half]
    zim_ref[...] = z[:, half:]


def _s5_rec_kernel(zre_ref, zim_ref, are_ref, aim_ref, hre_ref, him_ref, *, n_chunks, rs):
    ar, ai = are_ref[...], aim_ref[...]

    def body(k, carry):
        hr, hi = carry
        r0 = pl.multiple_of(k * rs, rs)
        hre_ref[pl.ds(r0, rs), :] = hr
        him_ref[pl.ds(r0, rs), :] = hi
        zr = zre_ref[pl.ds(r0, rs), :]
        zi = zim_ref[pl.ds(r0, rs), :]
        return ar * hr - ai * hi + zr, ar * hi + ai * hr + zi

    zero = jnp.zeros(ar.shape, F32)
    lax.fori_loop(0, n_chunks, body, (zero, zero))


def _s5_out_kernel(x_ref, hre_ref, him_ref, m_ref, rre_ref, rim_ref, y_ref, *, bsz):
    x = x_ref[...]
    hre = hre_ref[...].astype(BF16)
    him = him_ref[...].astype(BF16)
    y0 = _dot(x, m_ref[0]) + _dot(hre, rre_ref[0]) + _dot(him, rim_ref[0])
    y1 = _dot(x, m_ref[1]) + _dot(hre, rre_ref[1]) + _dot(him, rim_ref[1])
    r = lax.broadcasted_iota(jnp.int32, y0.shape, 0)
    y_ref[...] = jnp.where((r // bsz) % 2 == 0, y0, y1)


def _pair_blockdiag(a):
    two, g, k, n = a.shape
    a = a.reshape(two, g // 2, 2, k, n)
    z = jnp.zeros((two, g // 2, k, n), a.dtype)
    top = jnp.concatenate([a[:, :, 0], z], axis=-1)
    bot = jnp.concatenate([z, a[:, :, 1]], axis=-1)
    return jnp.concatenate([top, bot], axis=-2)


def _s5_matrices(lam_re, lam_im, log_step, b_re, b_im, c_re, c_im):
    hi = lax.Precision.HIGHEST
    L = S5_CHUNK
    step = jnp.exp(log_step)[..., None]
    n = jnp.arange(L + 1, dtype=F32)[:, None, None, None]
    mag = jnp.exp(n * (lam_re * step)[None])
    ang = n * (lam_im * step)[None]
    p_re, p_im = mag * jnp.cos(ang), mag * jnp.sin(ang)
    ab_re, ab_im = p_re[1], p_im[1]
    den = lam_re * lam_re + lam_im * lam_im
    coef_re = ((ab_re - 1.0) * lam_re + ab_im * lam_im) / den
    coef_im = (ab_im * lam_re - (ab_re - 1.0) * lam_im) / den
    e_re = p_re[:L] * coef_re - p_im[:L] * coef_im
    e_im = p_re[:L] * coef_im + p_im[:L] * coef_re
    ce_re = c_re[None] * e_re[..., None, :] - c_im[None] * e_im[..., None, :]
    ce_im = c_re[None] * e_im[..., None, :] + c_im[None] * e_re[..., None, :]
    kern = (jnp.einsum('ndgps,dgsq->ndgpq', ce_re, b_re, precision=hi)
            - jnp.einsum('ndgps,dgsq->ndgpq', ce_im, b_im, precision=hi))
    lag = jnp.arange(L)[None, :] - jnp.arange(L)[:, None]
    toe = (lag[None] == jnp.arange(L)[:, None, None]).astype(F32)
    m = jnp.einsum('nst,ndgpq->dgsqtp', toe, kern, precision=hi)
    g = lam_re.shape[1]
    pp = S5_GROUP
    m = m.reshape(2, g, L * pp, L * pp)
    w_re, w_im = e_re[::-1], e_im[::-1]
    q_re = w_re[..., None] * b_re[None] - w_im[..., None] * b_im[None]
    q_im = w_re[..., None] * b_im[None] + w_im[..., None] * b_re[None]
    q_re = q_re.transpose(1, 2, 0, 4, 3).reshape(2, g, L * pp, S5_STATE)
    q_im = q_im.transpose(1, 2, 0, 4, 3).reshape(2, g, L * pp, S5_STATE)
    a_re, a_im = p_re[1:], p_im[1:]
    ca_re = c_re[None] * a_re[..., None, :] - c_im[None] * a_im[..., None, :]
    ca_im = c_re[None] * a_im[..., None, :] + c_im[None] * a_re[..., None, :]
    r_re = ca_re.transpose(1, 2, 4, 0, 3).reshape(2, g, S5_STATE, L * pp)
    r_im = (-ca_im).transpose(1, 2, 4, 0, 3).reshape(2, g, S5_STATE, L * pp)
    return m, q_re, q_im, r_re, r_im, p_re[L], p_im[L]


def _s5_layer(u, params, *, bsz, seq, ctx_len):
    nt, c = u.shape
    L, pp = S5_CHUNK, S5_GROUP
    g = c // pp
    m, q_re, q_im, r_re, r_im, at_re, at_im = _s5_matrices(*params)
    m2 = _pair_blockdiag(m).astype(BF16)
    q2 = jnp.concatenate([_pair_blockdiag(q_re), _pair_blockdiag(q_im)], axis=-1).astype(BF16)
    rre2 = _pair_blockdiag(r_re).astype(BF16)
    rim2 = _pair_blockdiag(r_im).astype(BF16)
    lat = u[:bsz * seq].reshape(bsz, seq, c)
    ctx = u[bsz * seq:].reshape(bsz, ctx_len, c)
    fwd = jnp.concatenate([ctx, lat], axis=1)
    bwd = jnp.concatenate([ctx[:, ::-1], lat[:, ::-1]], axis=1)
    n_tok = seq + ctx_len
    nck = n_tok // L
    xs = jnp.stack([fwd, bwd], axis=0).astype(BF16).reshape(2, bsz, nck, L, g, pp)
    xs = xs.transpose(2, 0, 1, 4, 3, 5).reshape(nck * 2 * bsz, g * L * pp)
    rs = 2 * bsz
    rows = nck * rs
    gp = g // 2
    kw = 2 * L * pp
    sw = 2 * S5_STATE
    zre, zim = pl.pallas_call(
        functools.partial(_s5_state_kernel, bsz=bsz),
        out_shape=[jax.ShapeDtypeStruct((rows, g * S5_STATE), F32)] * 2,
        grid=(gp,),
        in_specs=[pl.BlockSpec((rows, kw), lambda j: (0, j)),
                  pl.BlockSpec((2, None, kw, 2 * sw), lambda j: (0, j, 0, 0))],
        out_specs=[pl.BlockSpec((rows, sw), lambda j: (0, j))] * 2,
        compiler_params=_cparams(("parallel",)),
        name="s5_state",
    )(xs, q2)
    a_re = jnp.repeat(at_re.reshape(2, 1, g * S5_STATE), bsz, axis=1).reshape(rs, g * S5_STATE)
    a_im = jnp.repeat(at_im.reshape(2, 1, g * S5_STATE), bsz, axis=1).reshape(rs, g * S5_STATE)
    lanes = g * S5_STATE
    hre, him = pl.pallas_call(
        functools.partial(_s5_rec_kernel, n_chunks=nck, rs=rs),
        out_shape=[jax.ShapeDtypeStruct((rows, lanes), F32)] * 2,
        grid=(lanes // S5_LANES,),
        in_specs=[pl.BlockSpec((rows, S5_LANES), lambda j: (0, j))] * 2
        + [pl.BlockSpec((rs, S5_LANES), lambda j: (0, j))] * 2,
        out_specs=[pl.BlockSpec((rows, S5_LANES), lambda j: (0, j))] * 2,
        compiler_params=_cparams(("parallel",)),
        name="s5_recurrence",
    )(zre, zim, a_re, a_im)
    y = pl.pallas_call(
        functools.partial(_s5_out_kernel, bsz=bsz),
        out_shape=jax.ShapeDtypeStruct((rows, g * L * pp), F32),
        grid=(gp,),
        in_specs=[pl.BlockSpec((rows, kw), lambda j: (0, j)),
                  pl.BlockSpec((rows, sw), lambda j: (0, j)),
                  pl.BlockSpec((rows, sw), lambda j: (0, j)),
                  pl.BlockSpec((2, None, kw, kw), lambda j: (0, j, 0, 0)),
                  pl.BlockSpec((2, None, sw, kw), lambda j: (0, j, 0, 0)),
                  pl.BlockSpec((2, None, sw, kw), lambda j: (0, j, 0, 0))],
        out_specs=pl.BlockSpec((rows, kw), lambda j: (0, j)),
        compiler_params=_cparams(("parallel",)),
        name="s5_out",
    )(xs, hre, him, m2, rre2, rim2)
    y = y.reshape(nck, 2, bsz, g, L, pp).transpose(1, 2, 0, 4, 3, 5).reshape(2, bsz, n_tok, c)
    y_ctx = y[0, :, :ctx_len] + y[1, :, :ctx_len][:, ::-1]
    y_lat = y[0, :, ctx_len:] + y[1, :, ctx_len:][:, ::-1]
    return jnp.concatenate([y_lat.reshape(bsz * seq, c), y_ctx.reshape(bsz * ctx_len, c)], axis=0)


def _head_norm(o, g_ref, center, dh):
    outs = []
    for h in range(HEADS):
        oh = o[:, h * dh:(h + 1) * dh]
        if center:
            oh = oh - jnp.mean(oh, axis=-1, keepdims=True)
        outs.append(oh * lax.rsqrt(jnp.mean(oh * oh, axis=-1, keepdims=True) + EPS))
    return jnp.concatenate(outs, axis=-1) * g_ref[...]


def _finish(x_ref, a, b, wout_ref, mod_ref, gpost_ref, o_ref):
    half = a.shape[1]
    y = _dot(a.astype(BF16), wout_ref[0:half, :]) + _dot(b.astype(BF16), wout_ref[half:2 * half, :])
    o_ref[...] = x_ref[...] + mod_ref[5:6, :] * _rms(y) * gpost_ref[1:2, :]


def _even_merge_kernel(x_ref, ga_ref, gb_ref, oaf_ref, oab_ref, obf_ref, obb_ref, na_ref, nb_ref,
                       wout_ref, mod_ref, gpost_ref, o_ref, *, dh):
    a = _silu(ga_ref[...]) * _head_norm(oaf_ref[...] + oab_ref[...], na_ref, False, dh)
    b = _sigmoid(gb_ref[...]) * _head_norm(obf_ref[...] + obb_ref[...], nb_ref, True, dh)
    _finish(x_ref, a, b, wout_ref, mod_ref, gpost_ref, o_ref)


def _odd_merge_kernel(x_ref, u_ref, gb_ref, ys_ref, obf_ref, obb_ref, sd_ref, wglu_ref, bglu_ref, nb_ref,
                      wout_ref, mod_ref, gpost_ref, o_ref, *, dh):
    y = ys_ref[...] + sd_ref[...] * u_ref[...]
    g = jax.nn.gelu(y)
    a = g * _sigmoid(_dot(g.astype(BF16), wglu_ref[...]) + bglu_ref[...])
    b = _silu(gb_ref[...]) * _head_norm(obf_ref[...] + obb_ref[...], nb_ref, True, dh)
    _finish(x_ref, a, b, wout_ref, mod_ref, gpost_ref, o_ref)


def _merge_call(kernel, s, tiles, consts, w_out, mod, gpost, *, n_rows, mod_of_row, name):
    d = s.shape[1]
    half = d // 2
    tm = MERGE_TILE
    n_tiles = n_rows // tm
    mod_of_tile = lambda i: mod_of_row(i * tm)
    ins = [s] + [a for a, _ in tiles] + list(consts) + [w_out, mod, gpost]
    in_specs = [pl.BlockSpec((tm, d), lambda i: (i, 0))]
    for _, cb in tiles:
        in_specs.append(pl.BlockSpec((tm, half), lambda i, cb=cb: (i, cb)))
    for a in consts:
        in_specs.append(pl.BlockSpec(a.shape, lambda i, nd=a.ndim: (0,) * nd))
    in_specs += [pl.BlockSpec((d, d), lambda i: (0, 0)),
                 pl.BlockSpec((None, N_MOD, d), lambda i: (mod_of_tile(i), 0, 0)),
                 pl.BlockSpec((3, d), lambda i: (0, 0))]
    return pl.pallas_call(
        kernel,
        out_shape=jax.ShapeDtypeStruct((n_tiles * tm, d), F32),
        grid=(n_tiles,),
        in_specs=in_specs,
        out_specs=pl.BlockSpec((tm, d), lambda i: (i, 0)),
        compiler_params=_cparams(("parallel",)),
        name=name,
    )(*ins)


def kernel(x, c, ctx, c_ctx, w_mod, b_mod, norm_pre, norm_post, ffn_w_gate, ffn_w_up, ffn_w_down, ev_w_in, ev_w_out, gla_w_gate, gla_b_gate, gla_norm, ml_conv_w, ml_conv_b, ml_b_gates, ml_norm, od_w_in, od_w_out, s5_lam_re, s5_lam_im, s5_log_step, s5_b_re, s5_b_im, s5_c_re, s5_c_im, s5_d, s5_w_glu, s5_b_glu, ret_log_decay, ret_norm):
    bsz, seq, d = x.shape
    ctx_len = ctx.shape[1]
    depth = w_mod.shape[0]
    half = d // 2
    dh = half // HEADS
    gla_dk = dh // 2
    d_ff = ffn_w_gate.shape[-1]
    tm = TOKEN_TILE
    assert seq % tm == 0 and (bsz * ctx_len) % tm == 0 and seq % GRID_W == 0 and ctx_len % CHUNK == 0
    assert 2 * bsz == 8 and (seq // GRID_W) % 8 == 0 and half == PROJ_TILE

    tiles_x = bsz * seq // tm
    tiles_all = tiles_x + bsz * ctx_len // tm
    tiles_per_batch = seq // tm
    mod_of_tile = lambda i: jnp.minimum(i // tiles_per_batch, bsz)
    mod_of_row = lambda r: jnp.minimum(r // seq, bsz)

    ffp = -(-d_ff // FF_TILE) * FF_TILE
    wg = jnp.pad(ffn_w_gate.astype(BF16), ((0, 0), (0, 0), (0, 0), (0, ffp - d_ff)))
    wu = jnp.pad(ffn_w_up.astype(BF16), ((0, 0), (0, 0), (0, 0), (0, ffp - d_ff)))
    wd = jnp.pad(ffn_w_down.astype(BF16), ((0, 0), (0, 0), (0, ffp - d_ff), (0, 0)))

    cc = jnp.concatenate([c, c_ctx[None], jnp.zeros((8 - bsz - 1, d), F32)], axis=0)
    mod_all = _modulation(cc, w_mod, b_mod).reshape(depth, 8, N_MOD, d)

    s = jnp.concatenate([x.reshape(bsz * seq, d), ctx.reshape(bsz * ctx_len, d)], axis=0)
    dims = dict(bsz=bsz, seq=seq, ctx_len=ctx_len)

    for layer in range(depth):
        need_ctx = layer < depth - 1
        mod = mod_all[layer]
        gpre, gpost = norm_pre[layer], norm_post[layer]
        ffn = functools.partial(_ffn, mod=mod, gpre=gpre, gpost=gpost, wg=wg, wu=wu, wd=wd, layer=layer,
                                mod_of_tile=mod_of_tile)
        s = ffn(s, which=0, sub=0, n_tiles=tiles_all)
        n_out = tiles_all if need_ctx else tiles_x

        if layer % 2 == 0:
            e = layer // 2
            w = ev_w_in[e]
            qk_w = 2 * HEADS * gla_dk
            o_q, o_k, o_v, o_ga = 0, qk_w // 2, qk_w, qk_w + half
            o_mq, o_mk, o_mv, o_gb, o_small = (o_ga + half, o_ga + 2 * half, o_ga + 3 * half, o_ga + 4 * half,
                                               o_ga + 5 * half)
            src = [0, qk_w // 2, qk_w, qk_w + half, qk_w + 2 * half]
            src_ml = qk_w + 2 * half + 2 * GLA_RANK
            n_gate = 4 * HEADS
            w_in = jnp.concatenate([w[:, :src[4]], w[:, src_ml:src_ml + 4 * half], w[:, src[4]:src_ml],
                                    w[:, src_ml + 4 * half:],
                                    jnp.zeros((d, PROJ_TILE - 2 * GLA_RANK - n_gate), F32)], axis=1).astype(BF16)
            p = _proj(s, mod, gpre, w_in, n_tiles=tiles_all, mod_of_tile=mod_of_tile)

            st0 = [jnp.zeros((bsz, 2, HEADS, dh, gla_dk), F32)]
            gla_ins = [(p, qk_w // 2, o_q), (p, qk_w // 2, o_k), (p, half, o_v), (p, 128, o_small)]
            gla_consts = [gla_w_gate[e].astype(BF16), gla_b_gate[e]]
            oa = _bidir_scan(
                lambda L: functools.partial(_gla_kernel, L=L, dk=gla_dk, dv=dh, gated=True,
                                            q_scale=float(gla_dk) ** -0.5, k_scale=1.0),
                "lat_rm", gla_ins, gla_consts, st0, out_width=half, name="gla", **dims)

            scale = jnp.concatenate([jnp.ones((half,), F32), jnp.full((half,), float(dh) ** -0.5, F32)])
            qk = _conv(p, ml_conv_w[e], ml_conv_b[e], scale, coloff=o_mq, **dims)
            bias = jnp.zeros((1, 128), F32).at[0, 2 * GLA_RANK:2 * GLA_RANK + n_gate].set(ml_b_gates[e].reshape(-1))
            ml_ins = [(qk, half, 0), (qk, half, half), (p, half, o_mv), (p, 128, o_small)]
            st0 = [jnp.zeros((bsz, 2, HEADS, dh, dh), F32), jnp.zeros((bsz, 2, HEADS, 8, dh), F32)]
            ob = _bidir_scan(
                lambda L: functools.partial(_mlstm_kernel, L=L, dh=dh, gcol=2 * GLA_RANK),
                "lat_cm", ml_ins, [bias], st0, out_width=half, name="mlstm", **dims)

            s = _merge_call(
                functools.partial(_even_merge_kernel, dh=dh), s,
                [(p, o_ga // half), (p, o_gb // half), (oa[0], 0), (oa[1], 0), (ob[0], 0), (ob[1], 0)],
                [gla_norm[e].reshape(1, half), ml_norm[e].reshape(1, half)],
                ev_w_out[e].astype(BF16), mod, gpost, n_rows=n_out * tm, mod_of_row=mod_of_row, name="even_merge")
        else:
            o = layer // 2
            p = _proj(s, mod, gpre, od_w_in[o].astype(BF16), n_tiles=tiles_all, mod_of_tile=mod_of_tile)

            ys = _s5_layer(p[:, :half], (s5_lam_re[o], s5_lam_im[o], s5_log_step[o], s5_b_re[o], s5_b_im[o],
                                         s5_c_re[o], s5_c_im[o]), **dims)

            lg = jnp.repeat(-jnp.exp(ret_log_decay[o]), dh, axis=1)
            ret_ins = [(p, half, half), (p, half, 2 * half), (p, half, 3 * half)]
            st0 = [jnp.zeros((bsz, 2, HEADS, dh, dh), F32)]
            od = _bidir_scan(
                lambda L: functools.partial(_gla_kernel, L=L, dk=dh, dv=dh, gated=False,
                                            q_scale=1.0, k_scale=float(dh) ** -0.5),
                "lat_cm", ret_ins, [lg], st0, out_width=half, name="retention", **dims)

            s = _merge_call(
                functools.partial(_odd_merge_kernel, dh=dh), s,
                [(p, 0), (p, 4), (ys, 0), (od[0], 0), (od[1], 0)],
                [s5_d[o].reshape(1, half), s5_w_glu[o].astype(BF16), s5_b_glu[o].reshape(1, half),
                 ret_norm[o].reshape(1, half)],
                od_w_out[o].astype(BF16), mod, gpost, n_rows=n_out * tm, mod_of_row=mod_of_row, name="odd_merge")

        s = ffn(s, which=1, sub=2, n_tiles=n_out)

    return s[:bsz * seq].reshape(bsz, seq, d)
```

```python
import functools

import jax
import jax.numpy as jnp
from jax import lax
from jax.experimental import pallas as pl
from jax.experimental.pallas import tpu as pltpu

F32 = jnp.float32
BF16 = jnp.bfloat16

EPS = 1e-6
GRID_W = 64
COL_TILE = 8
CHUNK = 64
FFN_RES = 0.5
N_MOD = 9
HEADS = 4
GLA_RANK = 16
GLA_GATE_NORM = 16.0
S5_GROUP = 16
S5_STATE = 64
S5_CHUNK = 16
SMALL_W = 128

TOKEN_TILE = 512
MERGE_ROWS = 32
FF_TILE = 512
PROJ_TILE = 1024
MOD_TILE = 1024
CONV_TILE = 256
CONV_ROWS = 256
S5_LANES = 512
VMEM_LIMIT = 56 * 1024 * 1024


def _cparams(sem):
    return pltpu.CompilerParams(dimension_semantics=sem, vmem_limit_bytes=VMEM_LIMIT)


def _rms(x):
    return x * lax.rsqrt(jnp.mean(x * x, axis=-1, keepdims=True) + EPS)


def _sigmoid(x):
    return 1.0 / (1.0 + jnp.exp(-x))


def _silu(x):
    return x * _sigmoid(x)


def _log_sigmoid(x):
    return jnp.minimum(x, 0.0) - jnp.log(1.0 + jnp.exp(-jnp.abs(x)))


def _dot(a, b):
    return jnp.dot(a, b, preferred_element_type=F32)


def _dot_nt(a, b):
    return lax.dot_general(a, b, (((1,), (1,)), ((), ())), preferred_element_type=F32)


def _dot_tn(a, b):
    return lax.dot_general(a, b, (((0,), (0,)), ((), ())), preferred_element_type=F32)


def _dot01(t01, x):
    x0 = x.astype(BF16)
    r1 = x - x0.astype(F32)
    x1 = r1.astype(BF16)
    x2 = (r1 - x1.astype(F32)).astype(BF16)
    return _dot(t01, x0) + _dot(t01, x1) + _dot(t01, x2)


def _perm_matrix(n, n_r, to_col_major):
    i = lax.broadcasted_iota(jnp.int32, (n, n), 0)
    c = lax.broadcasted_iota(jnp.int32, (n, n), 1)
    if to_col_major:
        hit = (i // n_r == c % COL_TILE) & (i % n_r == c // COL_TILE)
    else:
        hit = (i // COL_TILE == c % n_r) & (i % COL_TILE == c // n_r)
    return jnp.where(hit, 1.0, 0.0).astype(BF16)


def _tile2d(ref):
    v = ref[...]
    return v.reshape(-1, v.shape[-1])


def _mod_kernel(c_ref, w_ref, b_ref, o_ref):
    a = _silu(c_ref[...]).astype(BF16)
    o_ref[...] = _dot(a, w_ref[...].astype(BF16)) + b_ref[...]


def _modulation(cc, w_mod, b_mod):
    depth, d, nm = w_mod.shape
    return pl.pallas_call(
        _mod_kernel,
        out_shape=jax.ShapeDtypeStruct((depth, 8, nm), F32),
        grid=(depth, nm // MOD_TILE),
        in_specs=[pl.BlockSpec((8, d), lambda l, j: (0, 0)),
                  pl.BlockSpec((None, d, MOD_TILE), lambda l, j: (l, 0, j)),
                  pl.BlockSpec((None, 1, MOD_TILE), lambda l, j: (l, 0, j))],
        out_specs=pl.BlockSpec((None, 8, MOD_TILE), lambda l, j: (l, 0, j)),
        compiler_params=_cparams(("parallel", "parallel")),
        name="modulation",
    )(cc, w_mod, b_mod.reshape(depth, 1, nm))


def _ffn_kernel(x_ref, mod_ref, gpre_ref, gpost_ref, wg_ref, wu_ref, wd_ref, o_ref, h_ref, acc_ref, *, sub, nff):
    j = pl.program_id(1)

    @pl.when(j == 0)
    def _():
        shift = mod_ref[3 * sub:3 * sub + 1, :]
        scale = mod_ref[3 * sub + 1:3 * sub + 2, :]
        h = _rms(x_ref[...]) * gpre_ref[sub:sub + 1, :] * (1.0 + scale) + shift
        h_ref[...] = h.astype(BF16)
        acc_ref[...] = jnp.zeros_like(acc_ref)

    h = h_ref[...]
    a = _dot(h, wg_ref[...])
    u = _dot(h, wu_ref[...])
    t = (_silu(a) * u).astype(BF16)
    acc_ref[...] += _dot(t, wd_ref[...])

    @pl.when(j == nff - 1)
    def _():
        gate = mod_ref[3 * sub + 2:3 * sub + 3, :]
        o_ref[...] = x_ref[...] + FFN_RES * gate * _rms(acc_ref[...]) * gpost_ref[sub:sub + 1, :]


def _ffn(s, mod, gpre, gpost, wg, wu, wd, *, layer, which, sub, n_tiles, mod_of_tile):
    d = s.shape[1]
    ffp = wg.shape[-1]
    nff = ffp // FF_TILE
    tm = TOKEN_TILE
    return pl.pallas_call(
        functools.partial(_ffn_kernel, sub=sub, nff=nff),
        out_shape=jax.ShapeDtypeStruct((n_tiles * tm, d), F32),
        grid=(n_tiles, nff),
        in_specs=[pl.BlockSpec((tm, d), lambda i, j: (i, 0)),
                  pl.BlockSpec((None, N_MOD, d), lambda i, j: (mod_of_tile(i), 0, 0)),
                  pl.BlockSpec((3, d), lambda i, j: (0, 0)),
                  pl.BlockSpec((3, d), lambda i, j: (0, 0)),
                  pl.BlockSpec((None, None, d, FF_TILE), lambda i, j: (layer, which, 0, j)),
                  pl.BlockSpec((None, None, d, FF_TILE), lambda i, j: (layer, which, 0, j)),
                  pl.BlockSpec((None, None, FF_TILE, d), lambda i, j: (layer, which, j, 0))],
        out_specs=pl.BlockSpec((tm, d), lambda i, j: (i, 0)),
        scratch_shapes=[pltpu.VMEM((tm, d), BF16), pltpu.VMEM((tm, d), F32)],
        compiler_params=_cparams(("parallel", "arbitrary")),
        name="ffn",
    )(s, mod, gpre, gpost, wg, wu, wd)


def _proj_kernel(*refs, n_a, n_r, has_small):
    if has_small:
        x_ref, mod_ref, gpre_ref, w_ref, ws_ref, pa_ref, pb_ref, psa_ref, psb_ref, h_ref, hp_ref = refs
    else:
        x_ref, mod_ref, gpre_ref, w_ref, pa_ref, pb_ref, h_ref, hp_ref = refs
    j = pl.program_id(1)
    tm = h_ref.shape[0]

    @pl.when(j == 0)
    def _():
        x = _tile2d(x_ref)
        h = (_rms(x) * gpre_ref[1:2, :] * (1.0 + mod_ref[4:5, :]) + mod_ref[3:4, :]).astype(BF16)
        hp = _dot(_perm_matrix(tm, n_r, True), h).astype(BF16) if n_r else h
        h_ref[...] = h
        hp_ref[...] = hp
        if has_small:
            psa_ref[...] = _dot(h, ws_ref[:, :SMALL_W]).reshape(psa_ref.shape)
            psb_ref[...] = _dot(hp, ws_ref[:, SMALL_W:])

    @pl.when(j < n_a)
    def _():
        pa_ref[...] = _dot(h_ref[...], w_ref[...]).reshape(pa_ref.shape)

    @pl.when(j >= n_a)
    def _():
        pb_ref[...] = _dot(hp_ref[...], w_ref[...])


def _proj(s, mod, gpre, w, w_small, *, n_a, bsz, seq, ctx_len):
    d = s.shape[1]
    rows = seq // GRID_W
    tn = PROJ_TILE
    n_tiles_n = w.shape[1] // tn
    n_b = n_tiles_n - n_a
    has_small = w_small is not None
    w_ins = [w] + ([w_small] if has_small else [])
    w_specs = [pl.BlockSpec((d, tn), lambda i, j: (0, j))]
    if has_small:
        w_specs.append(pl.BlockSpec((d, 2 * SMALL_W), lambda i, j: (0, 0)))
    ja = lambda j: jnp.minimum(j, n_a - 1)
    jb = lambda j: jnp.maximum(j - n_a, 0)
    cp = _cparams(("parallel", "arbitrary"))

    tm = rows * COL_TILE
    ncb = GRID_W // COL_TILE
    s3 = s.reshape(s.shape[0] // GRID_W, GRID_W, d)
    out_shape = [jax.ShapeDtypeStruct((bsz * rows, GRID_W, n_a * tn), F32),
                 jax.ShapeDtypeStruct((bsz * seq, n_b * tn), F32)]
    out_specs = [pl.BlockSpec((rows, COL_TILE, tn), lambda i, j: (i // ncb, i % ncb, ja(j))),
                 pl.BlockSpec((tm, tn), lambda i, j: (i, jb(j)))]
    if has_small:
        out_shape += [jax.ShapeDtypeStruct((bsz * rows, GRID_W, SMALL_W), F32),
                      jax.ShapeDtypeStruct((bsz * seq, SMALL_W), F32)]
        out_specs += [pl.BlockSpec((rows, COL_TILE, SMALL_W), lambda i, j: (i // ncb, i % ncb, 0)),
                      pl.BlockSpec((tm, SMALL_W), lambda i, j: (i, 0))]
    lat = pl.pallas_call(
        functools.partial(_proj_kernel, n_a=n_a, n_r=rows, has_small=has_small),
        out_shape=out_shape,
        grid=(bsz * ncb, n_tiles_n),
        in_specs=[pl.BlockSpec((rows, COL_TILE, d), lambda i, j: (i // ncb, i % ncb, 0)),
                  pl.BlockSpec((None, N_MOD, d), lambda i, j: (i // ncb, 0, 0)),
                  pl.BlockSpec((3, d), lambda i, j: (0, 0))] + w_specs,
        out_specs=out_specs,
        scratch_shapes=[pltpu.VMEM((tm, d), BF16), pltpu.VMEM((tm, d), BF16)],
        compiler_params=cp,
        name="in_proj_lat",
    )(s3, mod, gpre, *w_ins)
    lat = [lat[0].reshape(bsz * seq, n_a * tn), lat[1]] \
        + ([lat[2].reshape(bsz * seq, SMALL_W), lat[3]] if has_small else [])

    tc = TOKEN_TILE
    n_c = bsz * ctx_len
    t0 = bsz * seq // tc
    out_shape = [jax.ShapeDtypeStruct((n_c, n_a * tn), F32), jax.ShapeDtypeStruct((n_c, n_b * tn), F32)]
    out_specs = [pl.BlockSpec((tc, tn), lambda i, j: (i, ja(j))), pl.BlockSpec((tc, tn), lambda i, j: (i, jb(j)))]
    if has_small:
        out_shape += [jax.ShapeDtypeStruct((n_c, SMALL_W), F32)] * 2
        out_specs += [pl.BlockSpec((tc, SMALL_W), lambda i, j: (i, 0))] * 2
    ctx = pl.pallas_call(
        functools.partial(_proj_kernel, n_a=n_a, n_r=0, has_small=has_small),
        out_shape=out_shape,
        grid=(n_c // tc, n_tiles_n),
        in_specs=[pl.BlockSpec((tc, d), lambda i, j: (t0 + i, 0)),
                  pl.BlockSpec((None, N_MOD, d), lambda i, j: (bsz, 0, 0)),
                  pl.BlockSpec((3, d), lambda i, j: (0, 0))] + w_specs,
        out_specs=out_specs,
        scratch_shapes=[pltpu.VMEM((tc, d), BF16), pltpu.VMEM((tc, d), BF16)],
        compiler_params=cp,
        name="in_proj_ctx",
    )(s, mod, gpre, *w_ins)
    return lat, list(ctx)


def _conv_kernel(x_ref, w_ref, b_ref, sc_ref, o_ref):
    n, ct = x_ref.shape
    ch = CONV_ROWS
    n_ch = n // ch
    r = lax.broadcasted_iota(jnp.int32, (ch, ct), 0)

    def body(c, carry):
        r0 = pl.multiple_of(c * ch, ch)
        x = x_ref[pl.ds(r0, ch), :]
        before = x_ref[pl.ds(pl.multiple_of(jnp.maximum(r0 - 8, 0), 8), 8), :][7:8, :]
        after = x_ref[pl.ds(pl.multiple_of(jnp.minimum(r0 + ch, n - 8), 8), 8), :][0:1, :]
        before = jnp.where(c == 0, 0.0, before)
        after = jnp.where(c == n_ch - 1, 0.0, after)
        prev = jnp.where(r == 0, before, pltpu.roll(x, 1, 0))
        nxt = jnp.where(r == ch - 1, after, pltpu.roll(x, ch - 1, 0))
        y = w_ref[1:2, :] * x + w_ref[0:1, :] * prev + w_ref[2:3, :] * nxt + b_ref[...]
        o_ref[pl.ds(r0, ch), :] = _silu(y) * sc_ref[...]
        return carry

    lax.fori_loop(0, n_ch, body, 0)


def _conv(p, conv_w, conv_b, scale, *, bsz, n):
    c = conv_w.shape[1]
    ct = CONV_TILE
    return pl.pallas_call(
        _conv_kernel,
        out_shape=jax.ShapeDtypeStruct((bsz * n, c), F32),
        grid=(bsz, c // ct),
        in_specs=[pl.BlockSpec((n, ct), lambda b, j: (b, j)),
                  pl.BlockSpec((3, ct), lambda b, j: (0, j)),
                  pl.BlockSpec((1, ct), lambda b, j: (0, j)),
                  pl.BlockSpec((1, ct), lambda b, j: (0, j))],
        out_specs=pl.BlockSpec((n, ct), lambda b, j: (b, j)),
        compiler_params=_cparams(("parallel", "parallel")),
        name="conv",
    )(p, conv_w, conv_b.reshape(1, c), scale.reshape(1, c))


def _gla_kernel(*refs, L, dk, dv, gated, q_scale, k_scale):
    if gated:
        qf, kf, vf, df, qb, kb, vb, db, wgate_ref, bgate_ref, st_in, of, ob, st_out = refs
    else:
        qf, kf, vf, qb, kb, vb, lg_ref, st_in, of, ob, st_out = refs
        df = db = None

    @pl.when(pl.program_id(1) == 0)
    def _():
        st_out[...] = st_in[...]

    row = lax.broadcasted_iota(jnp.int32, (L, L), 0)
    col = lax.broadcasted_iota(jnp.int32, (L, L), 1)
    trow = lax.broadcasted_iota(jnp.int32, (L, 1), 0)
    for d, (q_ref, k_ref, v_ref, d_ref, o_ref) in enumerate(((qf, kf, vf, df, of), (qb, kb, vb, db, ob))):
        mask = (row >= col) if d == 0 else (row <= col)
        mid = L // 2 if d == 0 else L - 1 - L // 2
        end = L - 1 if d == 0 else 0
        if gated:
            low = d_ref[:, d * GLA_RANK:(d + 1) * GLA_RANK].astype(BF16)
            z = _dot(low, wgate_ref[d]) + bgate_ref[d:d + 1, :]
            la = _log_sigmoid(z) * (1.0 / GLA_GATE_NORM)
            bcum = _dot01(mask.astype(BF16), la)
        else:
            cnt = (trow + 1) if d == 0 else (L - trow)
            bcum = cnt.astype(F32) * lg_ref[d:d + 1, :]
        for h in range(HEADS):
            b = bcum[:, h * dk:(h + 1) * dk]
            b_mid = b[mid:mid + 1, :]
            b_end = b[end:end + 1, :]
            q = q_ref[:, h * dk:(h + 1) * dk] * q_scale
            k = k_ref[:, h * dk:(h + 1) * dk] * k_scale
            v = v_ref[:, h * dv:(h + 1) * dv].astype(BF16)
            qs = (q * jnp.exp(b - b_mid)).astype(BF16)
            ks = (k * jnp.exp(b_mid - b)).astype(BF16)
            scores = jnp.where(mask, _dot_nt(qs, ks), 0.0)
            st = st_out[d, h]
            o = _dot(scores.astype(BF16), v) + _dot_nt((q * jnp.exp(b)).astype(BF16), st.astype(BF16))
            o_ref[:, h * dv:(h + 1) * dv] = o
            kd = (k * jnp.exp(b_end - b)).astype(BF16)
            st_out[d, h] = jnp.exp(b_end) * st + _dot_tn(v, kd)


def _mlstm_kernel(qf, kf, vf, gf, qb, kb, vb, gb, bias_ref, c_in, nm_in, of, ob, c_out, nm_out, *, L, dh):
    @pl.when(pl.program_id(1) == 0)
    def _():
        c_out[...] = c_in[...]
        nm_out[...] = nm_in[...]

    row = lax.broadcasted_iota(jnp.int32, (L, L), 0)
    col = lax.broadcasted_iota(jnp.int32, (L, L), 1)
    eye = row == col
    ones = jnp.ones((L, L), BF16)
    for d, (q_ref, k_ref, v_ref, g_ref, o_ref) in enumerate(((qf, kf, vf, gf, of), (qb, kb, vb, gb, ob))):
        mask = (row >= col) if d == 0 else (row <= col)
        cum_us = (row <= col) if d == 0 else (row >= col)
        end = L - 1 if d == 0 else 0
        g = g_ref[...] + bias_ref[...]
        fl = _log_sigmoid(g)
        bc_all = _dot01(mask.astype(BF16), fl)
        for h in range(HEADS):
            ci = d * 2 * HEADS + h
            cf = ci + HEADS
            i_col = g[:, ci:ci + 1]
            f_col = fl[:, cf:cf + 1]
            b_col = bc_all[:, cf:cf + 1]
            x = jnp.where(cum_us, f_col, 0.0) - jnp.where(eye, i_col, 0.0)
            drow = _dot01(ones, x)
            m_st = nm_out[d, h, 1:2, 0:1]
            dmat = jnp.where(mask, b_col - drow, -jnp.inf)
            inter = b_col + m_st
            m_t = jnp.maximum(inter, jnp.max(dmat, axis=-1, keepdims=True))
            q = q_ref[:, h * dh:(h + 1) * dh]
            k = k_ref[:, h * dh:(h + 1) * dh]
            v = v_ref[:, h * dh:(h + 1) * dh].astype(BF16)
            qh = q.astype(BF16)
            scores = _dot_nt(qh, k.astype(BF16)) * jnp.exp(dmat - m_t)
            w_inter = jnp.exp(inter - m_t)
            c_st = c_out[d, h]
            n_st = nm_out[d, h, 0:1, :]
            num = _dot(scores.astype(BF16), v) + w_inter * _dot(qh, c_st.astype(BF16))
            den = (jnp.sum(scores, axis=-1, keepdims=True)
                   + w_inter * jnp.sum(q * n_st, axis=-1, keepdims=True))
            o_ref[:, h * dh:(h + 1) * dh] = num / jnp.maximum(jnp.abs(den), jnp.exp(-m_t))
            b_end = b_col[end:end + 1, :]
            d_end = b_end - b_col + i_col
            m_new = jnp.maximum(b_end + m_st, jnp.max(d_end, axis=0, keepdims=True))
            kw = k * jnp.exp(d_end - m_new)
            decay = jnp.exp(b_end + m_st - m_new)
            c_out[d, h] = decay * c_st + _dot_tn(kw.astype(BF16), v)
            nm_out[d, h, 0:1, :] = decay * n_st + jnp.sum(kw, axis=0, keepdims=True)
            nm_out[d, h, 1:2, :] = jnp.broadcast_to(m_new, (1, dh))


def _scan_call(kernel, bsz, n, chunk_ins, consts, states, out_width, name):
    L = CHUNK
    chunk = (lambda s: s, lambda s: n - 1 - s)
    ins, in_specs = [], []
    for d in (0, 1):
        for a, width, coloff in chunk_ins:
            ins.append(a)
            in_specs.append(pl.BlockSpec((L, width), lambda b, s, d=d, cb=coloff // width: (b * n + chunk[d](s), cb)))
    for a in consts:
        ins.append(a)
        in_specs.append(pl.BlockSpec(a.shape, lambda b, s, nd=a.ndim: (0,) * nd))
    state_specs = [pl.BlockSpec((None,) + a.shape[1:], lambda b, s, nd=a.ndim: (b,) + (0,) * (nd - 1))
                   for a in states]
    ins += list(states)
    in_specs += state_specs
    out_struct = jax.ShapeDtypeStruct((bsz * n * L, out_width), F32)
    res = pl.pallas_call(
        kernel,
        out_shape=[out_struct, out_struct] + [jax.ShapeDtypeStruct(a.shape, F32) for a in states],
        grid=(bsz, n),
        in_specs=in_specs,
        out_specs=[pl.BlockSpec((L, out_width), lambda b, s, d=d: (b * n + chunk[d](s), 0)) for d in (0, 1)]
        + state_specs,
        compiler_params=_cparams(("parallel", "arbitrary")),
        name=name,
    )(*ins)
    return (res[0], res[1]), list(res[2:])


def _bidir_scan(kernel, ins_lat, ins_ctx, consts, states0, *, bsz, seq, ctx_len, out_width, name):
    o_ctx, st = _scan_call(kernel, bsz, ctx_len // CHUNK, ins_ctx, consts, states0, out_width, name + "_ctx")
    o_lat, _ = _scan_call(kernel, bsz, seq // CHUNK, ins_lat, consts, st, out_width, name + "_lat")
    return o_lat, o_ctx


def _s5_state_kernel(x_ref, q_ref, zfr_ref, zfi_ref, zbr_ref, zbi_ref):
    z = _dot(x_ref[...], q_ref[...])
    w = z.shape[1] // 4
    for i, ref in enumerate((zfr_ref, zfi_ref, zbr_ref, zbi_ref)):
        ref[...] = z[:, i * w:(i + 1) * w]


def _s5_rec_kernel(zfr, zfi, zbr, zbi, afr, afi, abr, abi, hfr, hfi, hbr, hbi, *, n_ctx_tiles, n_tiles, bsz):
    shape = afr.shape
    lo = lax.broadcasted_iota(jnp.int32, shape, 0) < bsz
    a_f = (afr[...], afi[...])
    a_b = (abr[...], abi[...])

    def step(a, h, z):
        return a[0] * h[0] - a[1] * h[1] + z[0], a[0] * h[1] + a[1] * h[0] + z[1]

    def swap(h, keep_lo):
        keep = lo if keep_lo else jnp.logical_not(lo)
        return tuple(jnp.where(keep, pltpu.roll(v, bsz, 0), 0.0) for v in h)

    def tile(tf, tb, carry):
        hf, hb = carry
        rf = pl.multiple_of(tf * 8, 8)
        rb = pl.multiple_of(tb * 8, 8)
        zf = (zfr[pl.ds(rf, 8), :], zfi[pl.ds(rf, 8), :])
        zb = (zbr[pl.ds(rb, 8), :], zbi[pl.ds(rb, 8), :])
        hf1 = swap(step(a_f, hf, zf), False)
        hb1 = swap(step(a_b, hb, zb), True)
        hfr[pl.ds(rf, 8), :] = hf[0] + hf1[0]
        hfi[pl.ds(rf, 8), :] = hf[1] + hf1[1]
        hbr[pl.ds(rb, 8), :] = hb[0] + hb1[0]
        hbi[pl.ds(rb, 8), :] = hb[1] + hb1[1]
        return swap(step(a_f, hf1, zf), True), swap(step(a_b, hb1, zb), False)

    zero = (jnp.zeros(shape, F32), jnp.zeros(shape, F32))
    carry = lax.fori_loop(0, n_ctx_tiles, lambda i, c: tile(i, n_ctx_tiles - 1 - i, c), (zero, zero))
    lax.fori_loop(0, n_tiles - n_ctx_tiles, lambda i, c: tile(n_ctx_tiles + i, n_tiles - 1 - i, c), carry)


def _s5_out_kernel(x_ref, hfr, hfi, hbr, hbi, m_ref, r_ref, y_ref):
    y = _dot(x_ref[...], m_ref[...])
    for i, h_ref in enumerate((hfr, hfi, hbr, hbi)):
        y += _dot(h_ref[...].astype(BF16), r_ref[i])
    y_ref[...] = y


def _pair_blockdiag(a):
    lead, (g, k, n) = a.shape[:-3], a.shape[-3:]
    a = a.reshape(lead + (g // 2, 2, k, n))
    z = jnp.zeros(lead + (g // 2, k, n), a.dtype)
    top = jnp.concatenate([a[..., 0, :, :], z], axis=-1)
    bot = jnp.concatenate([z, a[..., 1, :, :]], axis=-1)
    return jnp.concatenate([top, bot], axis=-2)


def _s5_matrices(lam_re, lam_im, log_step, b_re, b_im, c_re, c_im):
    hi = lax.Precision.HIGHEST
    L = S5_CHUNK
    step = jnp.exp(log_step)[..., None]
    n = jnp.arange(L + 1, dtype=F32)[:, None, None, None]
    mag = jnp.exp(n * (lam_re * step)[None])
    ang = n * (lam_im * step)[None]
    p_re, p_im = mag * jnp.cos(ang), mag * jnp.sin(ang)
    ab_re, ab_im = p_re[1], p_im[1]
    den = lam_re * lam_re + lam_im * lam_im
    coef_re = ((ab_re - 1.0) * lam_re + ab_im * lam_im) / den
    coef_im = (ab_im * lam_re - (ab_re - 1.0) * lam_im) / den
    e_re = p_re[:L] * coef_re - p_im[:L] * coef_im
    e_im = p_re[:L] * coef_im + p_im[:L] * coef_re
    ce_re = c_re[None] * e_re[..., None, :] - c_im[None] * e_im[..., None, :]
    ce_im = c_re[None] * e_im[..., None, :] + c_im[None] * e_re[..., None, :]
    kern = (jnp.einsum('ndgps,dgsq->ndgpq', ce_re, b_re, precision=hi)
            - jnp.einsum('ndgps,dgsq->ndgpq', ce_im, b_im, precision=hi))
    lag = jnp.arange(L)[None, :] - jnp.arange(L)[:, None]
    toe = (lag[None] == jnp.arange(L)[:, None, None]).astype(F32)
    m = jnp.einsum('nst,ndgpq->dgsqtp', toe, kern, precision=hi)
    w_re, w_im = e_re[::-1], e_im[::-1]
    q_re = (w_re[..., None] * b_re[None] - w_im[..., None] * b_im[None]).transpose(1, 2, 0, 4, 3)
    q_im = (w_re[..., None] * b_im[None] + w_im[..., None] * b_re[None]).transpose(1, 2, 0, 4, 3)
    a_re, a_im = p_re[1:], p_im[1:]
    ca_re = c_re[None] * a_re[..., None, :] - c_im[None] * a_im[..., None, :]
    ca_im = c_re[None] * a_im[..., None, :] + c_im[None] * a_re[..., None, :]
    r_re = ca_re.transpose(1, 2, 4, 0, 3)
    r_im = (-ca_im).transpose(1, 2, 4, 0, 3)
    return m, q_re, q_im, r_re, r_im, p_re[L], p_im[L]


def _s5_layer(u_lat, u_ctx, params, *, bsz, seq, ctx_len):
    c = u_lat.shape[1]
    L, pp, ns = S5_CHUNK, S5_GROUP, S5_STATE
    g = c // pp
    m, q_re, q_im, r_re, r_im, at_re, at_im = _s5_matrices(*params)
    lp = L * pp
    m_tot = m[0].reshape(g, lp, lp) + m[1][:, ::-1, :, ::-1, :].reshape(g, lp, lp)
    qs = [q_re[0], q_im[0], q_re[1][:, ::-1], q_im[1][:, ::-1]]
    rs = [r_re[0], r_im[0], r_re[1][:, :, ::-1], r_im[1][:, :, ::-1]]
    m2 = _pair_blockdiag(m_tot).astype(BF16)
    q2 = jnp.concatenate([_pair_blockdiag(q.reshape(g, lp, ns)) for q in qs], axis=-1).astype(BF16)
    r2 = jnp.stack([_pair_blockdiag(r.reshape(g, ns, lp)) for r in rs], axis=0).astype(BF16)

    def to_chunks(u, n):
        x = u.astype(BF16).reshape(bsz, n // L, L, g, pp).transpose(1, 0, 3, 2, 4)
        return x.reshape(n // L * bsz, g * lp)

    def from_chunks(y, n):
        return y.reshape(n // L, bsz, g, L, pp).transpose(1, 0, 3, 2, 4).reshape(bsz * n, c)

    xs = jnp.concatenate([to_chunks(u_ctx, ctx_len), to_chunks(u_lat, seq)], axis=0)
    rows = xs.shape[0]
    gp = g // 2
    kw = 2 * lp
    sw = 2 * ns
    lanes = g * ns
    z = pl.pallas_call(
        _s5_state_kernel,
        out_shape=[jax.ShapeDtypeStruct((rows, lanes), F32)] * 4,
        grid=(gp,),
        in_specs=[pl.BlockSpec((rows, kw), lambda j: (0, j)),
                  pl.BlockSpec((None, kw, 4 * sw), lambda j: (j, 0, 0))],
        out_specs=[pl.BlockSpec((rows, sw), lambda j: (0, j))] * 4,
        compiler_params=_cparams(("parallel",)),
        name="s5_state",
    )(xs, q2)
    a_rows = [jnp.broadcast_to(a.reshape(1, lanes), (8, lanes))
              for a in (at_re[0], at_im[0], at_re[1], at_im[1])]
    tile_rows = 2 * bsz
    h = pl.pallas_call(
        functools.partial(_s5_rec_kernel, n_ctx_tiles=ctx_len // L * bsz // tile_rows, n_tiles=rows // tile_rows,
                          bsz=bsz),
        out_shape=[jax.ShapeDtypeStruct((rows, lanes), F32)] * 4,
        grid=(lanes // S5_LANES,),
        in_specs=[pl.BlockSpec((rows, S5_LANES), lambda j: (0, j))] * 4
        + [pl.BlockSpec((8, S5_LANES), lambda j: (0, j))] * 4,
        out_specs=[pl.BlockSpec((rows, S5_LANES), lambda j: (0, j))] * 4,
        compiler_params=_cparams(("parallel",)),
        name="s5_recurrence",
    )(*z, *a_rows)
    y = pl.pallas_call(
        _s5_out_kernel,
        out_shape=jax.ShapeDtypeStruct((rows, g * lp), F32),
        grid=(gp,),
        in_specs=[pl.BlockSpec((rows, kw), lambda j: (0, j))]
        + [pl.BlockSpec((rows, sw), lambda j: (0, j))] * 4
        + [pl.BlockSpec((None, kw, kw), lambda j: (j, 0, 0)),
           pl.BlockSpec((4, None, sw, kw), lambda j: (0, j, 0, 0))],
        out_specs=pl.BlockSpec((rows, kw), lambda j: (0, j)),
        compiler_params=_cparams(("parallel",)),
        name="s5_out",
    )(xs, *h, m2, r2)
    n_c = ctx_len // L * bsz
    return from_chunks(y[n_c:], seq), from_chunks(y[:n_c], ctx_len)


def _head_norm(o, g_ref, center, dh):
    outs = []
    for h in range(HEADS):
        oh = o[:, h * dh:(h + 1) * dh]
        if center:
            oh = oh - jnp.mean(oh, axis=-1, keepdims=True)
        outs.append(oh * lax.rsqrt(jnp.mean(oh * oh, axis=-1, keepdims=True) + EPS))
    return jnp.concatenate(outs, axis=-1) * g_ref[...]


def _finish(x_ref, a, b, wout_ref, mod_ref, gpost_ref, o_ref, n_r):
    half = a.shape[1]
    b = b.astype(BF16)
    if n_r:
        b = _dot(_perm_matrix(b.shape[0], n_r, False), b).astype(BF16)
    y = _dot(a.astype(BF16), wout_ref[0:half, :]) + _dot(b, wout_ref[half:2 * half, :])
    out = _tile2d(x_ref) + mod_ref[5:6, :] * _rms(y) * gpost_ref[1:2, :]
    o_ref[...] = out.reshape(o_ref.shape)


def _even_merge_kernel(x_ref, ga_ref, oaf_ref, oab_ref, gb_ref, obf_ref, obb_ref, na_ref, nb_ref,
                       wout_ref, mod_ref, gpost_ref, o_ref, *, dh, n_r):
    a = _silu(_tile2d(ga_ref)) * _head_norm(_tile2d(oaf_ref) + _tile2d(oab_ref), na_ref, False, dh)
    b = _sigmoid(_tile2d(gb_ref)) * _head_norm(_tile2d(obf_ref) + _tile2d(obb_ref), nb_ref, True, dh)
    _finish(x_ref, a, b, wout_ref, mod_ref, gpost_ref, o_ref, n_r)


def _odd_merge_kernel(x_ref, u_ref, ys_ref, gb_ref, obf_ref, obb_ref, sd_ref, wglu_ref, bglu_ref, nb_ref,
                      wout_ref, mod_ref, gpost_ref, o_ref, *, dh, n_r):
    y = _tile2d(ys_ref) + sd_ref[...] * _tile2d(u_ref)
    g = jax.nn.gelu(y)
    a = g * _sigmoid(_dot(g.astype(BF16), wglu_ref[...]) + bglu_ref[...])
    b = _silu(_tile2d(gb_ref)) * _head_norm(_tile2d(obf_ref) + _tile2d(obb_ref), nb_ref, True, dh)
    _finish(x_ref, a, b, wout_ref, mod_ref, gpost_ref, o_ref, n_r)


def _merge(kernel, s, rm_lat, cm_lat, rm_ctx, cm_ctx, consts, w_out, mod, gpost, *, bsz, seq, ctx_len, name):
    d = s.shape[1]
    half = d // 2
    rows = seq // GRID_W
    rt = min(MERGE_ROWS, rows)
    n_rh = rows // rt
    ncb = GRID_W // COL_TILE
    per_b = n_rh * ncb
    const_specs = [pl.BlockSpec(a.shape, lambda i, nd=a.ndim: (0,) * nd) for a in consts] \
        + [pl.BlockSpec((d, d), lambda i: (0, 0))]
    gpost_spec = pl.BlockSpec((3, d), lambda i: (0, 0))
    cp = _cparams(("parallel",))

    rm_idx = lambda i: ((i // per_b) * n_rh + (i % per_b) // ncb, i % ncb)
    cm_idx = lambda i: ((i // per_b) * ncb + i % ncb, (i % per_b) // ncb)
    s3 = s.reshape(s.shape[0] // GRID_W, GRID_W, d)
    ins = [s3] + [a.reshape(bsz * rows, GRID_W, a.shape[1]) for a, _ in rm_lat] \
        + [a.reshape(bsz * GRID_W, rows, a.shape[1]) for a, _ in cm_lat]
    in_specs = [pl.BlockSpec((rt, COL_TILE, d), lambda i: rm_idx(i) + (0,))] \
        + [pl.BlockSpec((rt, COL_TILE, half), lambda i, cb=cb: rm_idx(i) + (cb,)) for _, cb in rm_lat] \
        + [pl.BlockSpec((COL_TILE, rt, half), lambda i, cb=cb: cm_idx(i) + (cb,)) for _, cb in cm_lat]
    n_out_groups = s3.shape[0] if rm_ctx is not None else bsz * rows
    out = pl.pallas_call(
        functools.partial(kernel, n_r=rt),
        out_shape=jax.ShapeDtypeStruct((n_out_groups, GRID_W, d), F32),
        grid=(bsz * per_b,),
        in_specs=in_specs + const_specs
        + [pl.BlockSpec((None, N_MOD, d), lambda i: (i // per_b, 0, 0)), gpost_spec],
        out_specs=pl.BlockSpec((rt, COL_TILE, d), lambda i: rm_idx(i) + (0,)),
        compiler_params=cp,
        name=name + "_lat",
    )(*ins, *consts, w_out, mod, gpost)
    out = out.reshape(n_out_groups * GRID_W, d)
    if rm_ctx is None:
        return out

    tm = rt * COL_TILE
    t0 = bsz * seq // tm
    tiles = list(rm_ctx) + list(cm_ctx)
    n_in = 1 + len(tiles) + len(consts) + 3
    inner = functools.partial(kernel, n_r=0)
    return pl.pallas_call(
        lambda *refs: inner(*refs[:n_in], refs[n_in + 1]),
        out_shape=jax.ShapeDtypeStruct(out.shape, F32),
        grid=(bsz * ctx_len // tm,),
        in_specs=[pl.BlockSpec((tm, d), lambda i: (t0 + i, 0))]
        + [pl.BlockSpec((tm, half), lambda i, cb=cb: (i, cb)) for _, cb in tiles] + const_specs
        + [pl.BlockSpec((None, N_MOD, d), lambda i: (bsz, 0, 0)), gpost_spec,
           pl.BlockSpec(memory_space=pl.ANY)],
        out_specs=pl.BlockSpec((tm, d), lambda i: (t0 + i, 0)),
        input_output_aliases={n_in: 0},
        compiler_params=cp,
        name=name + "_ctx",
    )(s, *[a for a, _ in tiles], *consts, w_out, mod, gpost, out)


def kernel(x, c, ctx, c_ctx, w_mod, b_mod, norm_pre, norm_post, ffn_w_gate, ffn_w_up, ffn_w_down, ev_w_in, ev_w_out, gla_w_gate, gla_b_gate, gla_norm, ml_conv_w, ml_conv_b, ml_b_gates, ml_norm, od_w_in, od_w_out, s5_lam_re, s5_lam_im, s5_log_step, s5_b_re, s5_b_im, s5_c_re, s5_c_im, s5_d, s5_w_glu, s5_b_glu, ret_log_decay, ret_norm):
    bsz, seq, d = x.shape
    ctx_len = ctx.shape[1]
    depth = w_mod.shape[0]
    half = d // 2
    dh = half // HEADS
    gla_dk = dh // 2
    d_ff = ffn_w_gate.shape[-1]
    tm = TOKEN_TILE
    rows = seq // GRID_W
    assert seq % tm == 0 and (bsz * ctx_len) % tm == 0 and seq == rows * GRID_W and ctx_len % CHUNK == 0
    assert 2 * bsz == 8 and rows % 8 == 0 and seq % CHUNK == 0 and half == PROJ_TILE
    assert (ctx_len // S5_CHUNK) % 2 == 0 and (seq // S5_CHUNK) % 2 == 0

    tiles_x = bsz * seq // tm
    tiles_all = tiles_x + bsz * ctx_len // tm
    tiles_per_batch = seq // tm
    mod_of_tile = lambda i: jnp.minimum(i // tiles_per_batch, bsz)

    ffp = -(-d_ff // FF_TILE) * FF_TILE
    wg = jnp.pad(ffn_w_gate.astype(BF16), ((0, 0), (0, 0), (0, 0), (0, ffp - d_ff)))
    wu = jnp.pad(ffn_w_up.astype(BF16), ((0, 0), (0, 0), (0, 0), (0, ffp - d_ff)))
    wd = jnp.pad(ffn_w_down.astype(BF16), ((0, 0), (0, 0), (0, ffp - d_ff), (0, 0)))

    cc = jnp.concatenate([c, c_ctx[None], jnp.zeros((8 - bsz - 1, d), F32)], axis=0)
    mod_all = _modulation(cc, w_mod, b_mod).reshape(depth, 8, N_MOD, d)

    s = jnp.concatenate([x.reshape(bsz * seq, d), ctx.reshape(bsz * ctx_len, d)], axis=0)
    dims = dict(bsz=bsz, seq=seq, ctx_len=ctx_len)

    for layer in range(depth):
        need_ctx = layer < depth - 1
        mod = mod_all[layer]
        gpre, gpost = norm_pre[layer], norm_post[layer]
        ffn = functools.partial(_ffn, mod=mod, gpre=gpre, gpost=gpost, wg=wg, wu=wu, wd=wd, layer=layer,
                                mod_of_tile=mod_of_tile)
        s = ffn(s, which=0, sub=0, n_tiles=tiles_all)

        if layer % 2 == 0:
            e = layer // 2
            w = ev_w_in[e]
            qk_w = 2 * HEADS * gla_dk
            n_gate = 4 * HEADS
            c_low = qk_w + 2 * half
            c_ml = c_low + 2 * GLA_RANK
            c_g = c_ml + 4 * half
            w_in = jnp.concatenate([w[:, :c_low], w[:, c_ml:c_g]], axis=1).astype(BF16)
            w_small = jnp.concatenate([w[:, c_low:c_ml], jnp.zeros((d, SMALL_W - 2 * GLA_RANK), F32),
                                       w[:, c_g:], jnp.zeros((d, SMALL_W - n_gate), F32)], axis=1).astype(BF16)
            (pa, pb, psa, psb), (pa_c, pb_c, psa_c, psb_c) = _proj(s, mod, gpre, w_in, w_small,
                                                                   n_a=c_low // PROJ_TILE, **dims)

            st0 = [jnp.zeros((bsz, 2, HEADS, dh, gla_dk), F32)]
            gla_ins = lambda p, ps: [(p, qk_w // 2, 0), (p, qk_w // 2, qk_w // 2), (p, half, qk_w), (ps, SMALL_W, 0)]
            oa, oa_c = _bidir_scan(
                functools.partial(_gla_kernel, L=CHUNK, dk=gla_dk, dv=dh, gated=True,
                                  q_scale=float(gla_dk) ** -0.5, k_scale=1.0),
                gla_ins(pa, psa), gla_ins(pa_c, psa_c), [gla_w_gate[e].astype(BF16), gla_b_gate[e]], st0,
                out_width=half, name="gla", **dims)

            scale = jnp.concatenate([jnp.ones((half,), F32), jnp.full((half,), float(dh) ** -0.5, F32)])
            qk = _conv(pb, ml_conv_w[e], ml_conv_b[e], scale, bsz=bsz, n=seq)
            qk_c = _conv(pb_c, ml_conv_w[e], ml_conv_b[e], scale, bsz=bsz, n=ctx_len)
            bias = jnp.zeros((1, SMALL_W), F32).at[0, :n_gate].set(ml_b_gates[e].reshape(-1))
            ml_ins = lambda qk_, p, ps: [(qk_, half, 0), (qk_, half, half), (p, half, 2 * half), (ps, SMALL_W, 0)]
            st0 = [jnp.zeros((bsz, 2, HEADS, dh, dh), F32), jnp.zeros((bsz, 2, HEADS, 8, dh), F32)]
            ob, ob_c = _bidir_scan(
                functools.partial(_mlstm_kernel, L=CHUNK, dh=dh),
                ml_ins(qk, pb, psb), ml_ins(qk_c, pb_c, psb_c), [bias], st0,
                out_width=half, name="mlstm", **dims)

            ga_cb, gb_cb = (qk_w + half) // half, 3
            s = _merge(
                functools.partial(_even_merge_kernel, dh=dh), s,
                [(pa, ga_cb), (oa[0], 0), (oa[1], 0)], [(pb, gb_cb), (ob[0], 0), (ob[1], 0)],
                [(pa_c, ga_cb), (oa_c[0], 0), (oa_c[1], 0)] if need_ctx else None,
                [(pb_c, gb_cb), (ob_c[0], 0), (ob_c[1], 0)],
                [gla_norm[e].reshape(1, half), ml_norm[e].reshape(1, half)],
                ev_w_out[e].astype(BF16), mod, gpost, name="even_merge", **dims)
        else:
            o = layer // 2
            (pa, pb), (pa_c, pb_c) = _proj(s, mod, gpre, od_w_in[o].astype(BF16), None, n_a=1, **dims)

            ys, ys_c = _s5_layer(pa, pa_c, (s5_lam_re[o], s5_lam_im[o], s5_log_step[o], s5_b_re[o], s5_b_im[o],
                                            s5_c_re[o], s5_c_im[o]), **dims)

            lg = jnp.repeat(-jnp.exp(ret_log_decay[o]), dh, axis=1)
            ret_ins = lambda p: [(p, half, 0), (p, half, half), (p, half, 2 * half)]
            st0 = [jnp.zeros((bsz, 2, HEADS, dh, dh), F32)]
            od, od_c = _bidir_scan(
                functools.partial(_gla_kernel, L=CHUNK, dk=dh, dv=dh, gated=False,
                                  q_scale=1.0, k_scale=float(dh) ** -0.5),
                ret_ins(pb), ret_ins(pb_c), [lg], st0, out_width=half, name="retention", **dims)

            s = _merge(
                functools.partial(_odd_merge_kernel, dh=dh), s,
                [(pa, 0), (ys, 0)], [(pb, 3), (od[0], 0), (od[1], 0)],
                [(pa_c, 0), (ys_c, 0)] if need_ctx else None, [(pb_c, 3), (od_c[0], 0), (od_c[1], 0)],
                [s5_d[o].reshape(1, half), s5_w_glu[o].astype(BF16), s5_b_glu[o].reshape(1, half),
                 ret_norm[o].reshape(1, half)],
                od_w_out[o].astype(BF16), mod, gpost, name="odd_merge", **dims)

        s = ffn(s, which=1, sub=2, n_tiles=tiles_all if need_ctx else tiles_x)

    return s[:bsz * seq].reshape(bsz, seq, d)
```

```python
import functools

import jax
import jax.numpy as jnp
from jax import lax
from jax.experimental import pallas as pl
from jax.experimental.pallas import tpu as pltpu

F32 = jnp.float32
BF16 = jnp.bfloat16

EPS = 1e-6
GRID_W = 64
COL_TILE = 8
CHUNK = 64
FFN_RES = 0.5
N_MOD = 9
HEADS = 4
GLA_RANK = 16
GLA_GATE_NORM = 16.0
S5_GROUP = 16
S5_STATE = 64
S5_CHUNK = 8
S5_LANE_TILE = 128
S5_ROW_TILE = 128
SMALL_W = 128

TOKEN_TILE = 512
MERGE_ROWS = 32
FF_TILE = 512
PROJ_TILE = 1024
MOD_TILE = 1024
CONV_TILE = 256
CONV_ROWS = 256
S5_LANES = 256
VMEM_LIMIT = 56 * 1024 * 1024


def _cparams(sem):
    return pltpu.CompilerParams(dimension_semantics=sem, vmem_limit_bytes=VMEM_LIMIT)


def _rms(x):
    return x * lax.rsqrt(jnp.mean(x * x, axis=-1, keepdims=True) + EPS)


def _sigmoid(x):
    return 1.0 / (1.0 + jnp.exp(-x))


def _silu(x):
    return x * _sigmoid(x)


def _log_sigmoid(x):
    return jnp.minimum(x, 0.0) - jnp.log(1.0 + jnp.exp(-jnp.abs(x)))


def _dot(a, b):
    return jnp.dot(a, b, preferred_element_type=F32)


def _dot_nt(a, b):
    return lax.dot_general(a, b, (((1,), (1,)), ((), ())), preferred_element_type=F32)


def _dot_tn(a, b):
    return lax.dot_general(a, b, (((0,), (0,)), ((), ())), preferred_element_type=F32)


def _dot01(t01, x):
    x0 = x.astype(BF16)
    r1 = x - x0.astype(F32)
    x1 = r1.astype(BF16)
    x2 = (r1 - x1.astype(F32)).astype(BF16)
    return _dot(t01, x0) + _dot(t01, x1) + _dot(t01, x2)


def _perm_matrix(n, n_r, to_col_major):
    i = lax.broadcasted_iota(jnp.int32, (n, n), 0)
    c = lax.broadcasted_iota(jnp.int32, (n, n), 1)
    if to_col_major:
        hit = (i // n_r == c % COL_TILE) & (i % n_r == c // COL_TILE)
    else:
        hit = (i // COL_TILE == c % n_r) & (i % COL_TILE == c // n_r)
    return jnp.where(hit, 1.0, 0.0).astype(BF16)


def _tile2d(ref):
    v = ref[...]
    return v.reshape(-1, v.shape[-1])


def _mod_kernel(c_ref, w_ref, b_ref, o_ref):
    a = _silu(c_ref[...]).astype(BF16)
    o_ref[...] = _dot(a, w_ref[...].astype(BF16)) + b_ref[...]


def _modulation(cc, w_mod, b_mod):
    depth, d, nm = w_mod.shape
    return pl.pallas_call(
        _mod_kernel,
        out_shape=jax.ShapeDtypeStruct((depth, 8, nm), F32),
        grid=(depth, nm // MOD_TILE),
        in_specs=[pl.BlockSpec((8, d), lambda l, j: (0, 0)),
                  pl.BlockSpec((None, d, MOD_TILE), lambda l, j: (l, 0, j)),
                  pl.BlockSpec((None, 1, MOD_TILE), lambda l, j: (l, 0, j))],
        out_specs=pl.BlockSpec((None, 8, MOD_TILE), lambda l, j: (l, 0, j)),
        compiler_params=_cparams(("parallel", "parallel")),
        name="modulation",
    )(cc, w_mod, b_mod.reshape(depth, 1, nm))


def _ffn_kernel(x_ref, mod_ref, gpre_ref, gpost_ref, wg_ref, wu_ref, wd_ref, o_ref, h_ref, acc_ref, *, sub, nff,
                tail):
    j = pl.program_id(1)

    @pl.when(j == 0)
    def _():
        shift = mod_ref[3 * sub:3 * sub + 1, :]
        scale = mod_ref[3 * sub + 1:3 * sub + 2, :]
        h = _rms(x_ref[...]) * gpre_ref[sub:sub + 1, :] * (1.0 + scale) + shift
        h_ref[...] = h.astype(BF16)
        acc_ref[...] = jnp.zeros_like(acc_ref)

    def hidden_step(valid):
        h = h_ref[...]
        t = _silu(_dot(h, wg_ref[...])) * _dot(h, wu_ref[...])
        wd = wd_ref[...]
        if valid is not None:
            t = jnp.where(lax.broadcasted_iota(jnp.int32, t.shape, 1) < valid, t, 0.0)
            wd = jnp.where(lax.broadcasted_iota(jnp.int32, wd.shape, 0) < valid, wd, jnp.zeros_like(wd))
        acc_ref[...] += _dot(t.astype(BF16), wd)

    if tail:
        pl.when(j < nff - 1)(lambda: hidden_step(None))
        pl.when(j == nff - 1)(lambda: hidden_step(tail))
    else:
        hidden_step(None)

    @pl.when(j == nff - 1)
    def _():
        gate = mod_ref[3 * sub + 2:3 * sub + 3, :]
        o_ref[...] = x_ref[...] + FFN_RES * gate * _rms(acc_ref[...]) * gpost_ref[sub:sub + 1, :]


def _ffn(s, mod, gpre, gpost, wg, wu, wd, *, layer, which, sub, n_tiles, mod_of_tile):
    d = s.shape[1]
    d_ff = wg.shape[-1]
    nff = pl.cdiv(d_ff, FF_TILE)
    tm = TOKEN_TILE
    return pl.pallas_call(
        functools.partial(_ffn_kernel, sub=sub, nff=nff, tail=d_ff % FF_TILE),
        out_shape=jax.ShapeDtypeStruct((n_tiles * tm, d), F32),
        grid=(n_tiles, nff),
        in_specs=[pl.BlockSpec((tm, d), lambda i, j: (i, 0)),
                  pl.BlockSpec((None, N_MOD, d), lambda i, j: (mod_of_tile(i), 0, 0)),
                  pl.BlockSpec((3, d), lambda i, j: (0, 0)),
                  pl.BlockSpec((3, d), lambda i, j: (0, 0)),
                  pl.BlockSpec((None, None, d, FF_TILE), lambda i, j: (layer, which, 0, j)),
                  pl.BlockSpec((None, None, d, FF_TILE), lambda i, j: (layer, which, 0, j)),
                  pl.BlockSpec((None, None, FF_TILE, d), lambda i, j: (layer, which, j, 0))],
        out_specs=pl.BlockSpec((tm, d), lambda i, j: (i, 0)),
        scratch_shapes=[pltpu.VMEM((tm, d), BF16), pltpu.VMEM((tm, d), F32)],
        compiler_params=_cparams(("parallel", "arbitrary")),
        name="ffn",
    )(s, mod, gpre, gpost, wg, wu, wd)


def _proj_kernel(*refs, n_a, n_r, has_small):
    if has_small:
        x_ref, mod_ref, gpre_ref, w_ref, ws_ref, pa_ref, pb_ref, psa_ref, psb_ref, h_ref, hp_ref = refs
    else:
        x_ref, mod_ref, gpre_ref, w_ref, pa_ref, pb_ref, h_ref, hp_ref = refs
    j = pl.program_id(1)
    tm = h_ref.shape[0]

    @pl.when(j == 0)
    def _():
        x = _tile2d(x_ref)
        h = (_rms(x) * gpre_ref[1:2, :] * (1.0 + mod_ref[4:5, :]) + mod_ref[3:4, :]).astype(BF16)
        hp = _dot(_perm_matrix(tm, n_r, True), h).astype(BF16) if n_r else h
        h_ref[...] = h
        hp_ref[...] = hp
        if has_small:
            psa_ref[...] = _dot(h, ws_ref[:, :SMALL_W]).reshape(psa_ref.shape)
            psb_ref[...] = _dot(hp, ws_ref[:, SMALL_W:])

    @pl.when(j < n_a)
    def _():
        pa_ref[...] = _dot(h_ref[...], w_ref[...]).reshape(pa_ref.shape)

    @pl.when(j >= n_a)
    def _():
        pb_ref[...] = _dot(hp_ref[...], w_ref[...])


def _proj(s, mod, gpre, w, w_small, *, n_a, bsz, seq, ctx_len):
    d = s.shape[1]
    rows = seq // GRID_W
    tn = PROJ_TILE
    n_tiles_n = w.shape[1] // tn
    n_b = n_tiles_n - n_a
    has_small = w_small is not None
    w_ins = [w] + ([w_small] if has_small else [])
    w_specs = [pl.BlockSpec((d, tn), lambda i, j: (0, j))]
    if has_small:
        w_specs.append(pl.BlockSpec((d, 2 * SMALL_W), lambda i, j: (0, 0)))
    ja = lambda j: jnp.minimum(j, n_a - 1)
    jb = lambda j: jnp.maximum(j - n_a, 0)
    cp = _cparams(("parallel", "arbitrary"))

    tm = rows * COL_TILE
    ncb = GRID_W // COL_TILE
    s3 = s.reshape(s.shape[0] // GRID_W, GRID_W, d)
    out_shape = [jax.ShapeDtypeStruct((bsz * rows, GRID_W, n_a * tn), F32),
                 jax.ShapeDtypeStruct((bsz * seq, n_b * tn), F32)]
    out_specs = [pl.BlockSpec((rows, COL_TILE, tn), lambda i, j: (i // ncb, i % ncb, ja(j))),
                 pl.BlockSpec((tm, tn), lambda i, j: (i, jb(j)))]
    if has_small:
        out_shape += [jax.ShapeDtypeStruct((bsz * rows, GRID_W, SMALL_W), F32),
                      jax.ShapeDtypeStruct((bsz * seq, SMALL_W), F32)]
        out_specs += [pl.BlockSpec((rows, COL_TILE, SMALL_W), lambda i, j: (i // ncb, i % ncb, 0)),
                      pl.BlockSpec((tm, SMALL_W), lambda i, j: (i, 0))]
    lat = pl.pallas_call(
        functools.partial(_proj_kernel, n_a=n_a, n_r=rows, has_small=has_small),
        out_shape=out_shape,
        grid=(bsz * ncb, n_tiles_n),
        in_specs=[pl.BlockSpec((rows, COL_TILE, d), lambda i, j: (i // ncb, i % ncb, 0)),
                  pl.BlockSpec((None, N_MOD, d), lambda i, j: (i // ncb, 0, 0)),
                  pl.BlockSpec((3, d), lambda i, j: (0, 0))] + w_specs,
        out_specs=out_specs,
        scratch_shapes=[pltpu.VMEM((tm, d), BF16), pltpu.VMEM((tm, d), BF16)],
        compiler_params=cp,
        name="in_proj_lat",
    )(s3, mod, gpre, *w_ins)
    lat = [lat[0].reshape(bsz * seq, n_a * tn), lat[1]] \
        + ([lat[2].reshape(bsz * seq, SMALL_W), lat[3]] if has_small else [])

    tc = TOKEN_TILE
    n_c = bsz * ctx_len
    t0 = bsz * seq // tc
    out_shape = [jax.ShapeDtypeStruct((n_c, n_a * tn), F32), jax.ShapeDtypeStruct((n_c, n_b * tn), F32)]
    out_specs = [pl.BlockSpec((tc, tn), lambda i, j: (i, ja(j))), pl.BlockSpec((tc, tn), lambda i, j: (i, jb(j)))]
    if has_small:
        out_shape += [jax.ShapeDtypeStruct((n_c, SMALL_W), F32)] * 2
        out_specs += [pl.BlockSpec((tc, SMALL_W), lambda i, j: (i, 0))] * 2
    ctx = pl.pallas_call(
        functools.partial(_proj_kernel, n_a=n_a, n_r=0, has_small=has_small),
        out_shape=out_shape,
        grid=(n_c // tc, n_tiles_n),
        in_specs=[pl.BlockSpec((tc, d), lambda i, j: (t0 + i, 0)),
                  pl.BlockSpec((None, N_MOD, d), lambda i, j: (bsz, 0, 0)),
                  pl.BlockSpec((3, d), lambda i, j: (0, 0))] + w_specs,
        out_specs=out_specs,
        scratch_shapes=[pltpu.VMEM((tc, d), BF16), pltpu.VMEM((tc, d), BF16)],
        compiler_params=cp,
        name="in_proj_ctx",
    )(s, mod, gpre, *w_ins)
    return lat, list(ctx)


def _conv_kernel(x_ref, w_ref, b_ref, sc_ref, o_ref):
    n, ct = x_ref.shape
    ch = CONV_ROWS
    n_ch = n // ch
    r = lax.broadcasted_iota(jnp.int32, (ch, ct), 0)

    def body(c, carry):
        r0 = pl.multiple_of(c * ch, ch)
        x = x_ref[pl.ds(r0, ch), :]
        before = x_ref[pl.ds(pl.multiple_of(jnp.maximum(r0 - 8, 0), 8), 8), :][7:8, :]
        after = x_ref[pl.ds(pl.multiple_of(jnp.minimum(r0 + ch, n - 8), 8), 8), :][0:1, :]
        before = jnp.where(c == 0, 0.0, before)
        after = jnp.where(c == n_ch - 1, 0.0, after)
        prev = jnp.where(r == 0, before, pltpu.roll(x, 1, 0))
        nxt = jnp.where(r == ch - 1, after, pltpu.roll(x, ch - 1, 0))
        y = w_ref[1:2, :] * x + w_ref[0:1, :] * prev + w_ref[2:3, :] * nxt + b_ref[...]
        o_ref[pl.ds(r0, ch), :] = _silu(y) * sc_ref[...]
        return carry

    lax.fori_loop(0, n_ch, body, 0)


def _conv(p, conv_w, conv_b, scale, *, bsz, n):
    c = conv_w.shape[1]
    ct = CONV_TILE
    return pl.pallas_call(
        _conv_kernel,
        out_shape=jax.ShapeDtypeStruct((bsz * n, c), F32),
        grid=(bsz, c // ct),
        in_specs=[pl.BlockSpec((n, ct), lambda b, j: (b, j)),
                  pl.BlockSpec((3, ct), lambda b, j: (0, j)),
                  pl.BlockSpec((1, ct), lambda b, j: (0, j)),
                  pl.BlockSpec((1, ct), lambda b, j: (0, j))],
        out_specs=pl.BlockSpec((n, ct), lambda b, j: (b, j)),
        compiler_params=_cparams(("parallel", "parallel")),
        name="conv",
    )(p, conv_w, conv_b.reshape(1, c), scale.reshape(1, c))


def _gla_kernel(*refs, L, dk, dv, gated, q_scale, k_scale):
    if gated:
        qf, kf, vf, df, qb, kb, vb, db, wgate_ref, bgate_ref, st_in, of, ob, st_out = refs
    else:
        qf, kf, vf, qb, kb, vb, lg_ref, st_in, of, ob, st_out = refs
        df = db = None

    @pl.when(pl.program_id(1) == 0)
    def _():
        st_out[...] = st_in[...]

    row = lax.broadcasted_iota(jnp.int32, (L, L), 0)
    col = lax.broadcasted_iota(jnp.int32, (L, L), 1)
    trow = lax.broadcasted_iota(jnp.int32, (L, 1), 0)
    for d, (q_ref, k_ref, v_ref, d_ref, o_ref) in enumerate(((qf, kf, vf, df, of), (qb, kb, vb, db, ob))):
        mask = (row >= col) if d == 0 else (row <= col)
        mid = L // 2 if d == 0 else L - 1 - L // 2
        end = L - 1 if d == 0 else 0
        if gated:
            low = d_ref[:, d * GLA_RANK:(d + 1) * GLA_RANK].astype(BF16)
            z = _dot(low, wgate_ref[d]) + bgate_ref[d:d + 1, :]
            la = _log_sigmoid(z) * (1.0 / GLA_GATE_NORM)
            bcum = _dot01(mask.astype(BF16), la)
        else:
            cnt = (trow + 1) if d == 0 else (L - trow)
            bcum = cnt.astype(F32) * lg_ref[d:d + 1, :]
        for h in range(HEADS):
            b = bcum[:, h * dk:(h + 1) * dk]
            b_mid = b[mid:mid + 1, :]
            b_end = b[end:end + 1, :]
            q = q_ref[:, h * dk:(h + 1) * dk] * q_scale
            k = k_ref[:, h * dk:(h + 1) * dk] * k_scale
            v = v_ref[:, h * dv:(h + 1) * dv].astype(BF16)
            qs = (q * jnp.exp(b - b_mid)).astype(BF16)
            ks = (k * jnp.exp(b_mid - b)).astype(BF16)
            scores = jnp.where(mask, _dot_nt(qs, ks), 0.0)
            st = st_out[d, h]
            o = _dot(scores.astype(BF16), v) + _dot_nt((q * jnp.exp(b)).astype(BF16), st.astype(BF16))
            o_ref[:, h * dv:(h + 1) * dv] = o
            kd = (k * jnp.exp(b_end - b)).astype(BF16)
            st_out[d, h] = jnp.exp(b_end) * st + _dot_tn(v, kd)


def _mlstm_kernel(qf, kf, vf, gf, qb, kb, vb, gb, bias_ref, c_in, nm_in, of, ob, c_out, nm_out, *, L, dh):
    @pl.when(pl.program_id(1) == 0)
    def _():
        c_out[...] = c_in[...]
        nm_out[...] = nm_in[...]

    row = lax.broadcasted_iota(jnp.int32, (L, L), 0)
    col = lax.broadcasted_iota(jnp.int32, (L, L), 1)
    eye = row == col
    ones = jnp.ones((L, L), BF16)
    for d, (q_ref, k_ref, v_ref, g_ref, o_ref) in enumerate(((qf, kf, vf, gf, of), (qb, kb, vb, gb, ob))):
        mask = (row >= col) if d == 0 else (row <= col)
        cum_us = (row <= col) if d == 0 else (row >= col)
        end = L - 1 if d == 0 else 0
        g = g_ref[...] + bias_ref[...]
        fl = _log_sigmoid(g)
        bc_all = _dot01(mask.astype(BF16), fl)
        for h in range(HEADS):
            ci = d * 2 * HEADS + h
            cf = ci + HEADS
            i_col = g[:, ci:ci + 1]
            f_col = fl[:, cf:cf + 1]
            b_col = bc_all[:, cf:cf + 1]
            x = jnp.where(cum_us, f_col, 0.0) - jnp.where(eye, i_col, 0.0)
            drow = _dot01(ones, x)
            m_st = nm_out[d, h, 1:2, 0:1]
            dmat = jnp.where(mask, b_col - drow, -jnp.inf)
            inter = b_col + m_st
            m_t = jnp.maximum(inter, jnp.max(dmat, axis=-1, keepdims=True))
            q = q_ref[:, h * dh:(h + 1) * dh]
            k = k_ref[:, h * dh:(h + 1) * dh]
            v = v_ref[:, h * dh:(h + 1) * dh].astype(BF16)
            qh = q.astype(BF16)
            scores = _dot_nt(qh, k.astype(BF16)) * jnp.exp(dmat - m_t)
            w_inter = jnp.exp(inter - m_t)
            c_st = c_out[d, h]
            n_st = nm_out[d, h, 0:1, :]
            num = _dot(scores.astype(BF16), v) + w_inter * _dot(qh, c_st.astype(BF16))
            den = (jnp.sum(scores, axis=-1, keepdims=True)
                   + w_inter * jnp.sum(q * n_st, axis=-1, keepdims=True))
            o_ref[:, h * dh:(h + 1) * dh] = num / jnp.maximum(jnp.abs(den), jnp.exp(-m_t))
            b_end = b_col[end:end + 1, :]
            d_end = b_end - b_col + i_col
            m_new = jnp.maximum(b_end + m_st, jnp.max(d_end, axis=0, keepdims=True))
            kw = k * jnp.exp(d_end - m_new)
            decay = jnp.exp(b_end + m_st - m_new)
            c_out[d, h] = decay * c_st + _dot_tn(kw.astype(BF16), v)
            nm_out[d, h, 0:1, :] = decay * n_st + jnp.sum(kw, axis=0, keepdims=True)
            nm_out[d, h, 1:2, :] = jnp.broadcast_to(m_new, (1, dh))


def _scan_call(kernel, bsz, n, chunk_ins, consts, states, out_width, name):
    L = CHUNK
    chunk = (lambda s: s, lambda s: n - 1 - s)
    ins, in_specs = [], []
    for d in (0, 1):
        for a, width, coloff in chunk_ins:
            ins.append(a)
            in_specs.append(pl.BlockSpec((L, width), lambda b, s, d=d, cb=coloff // width: (b * n + chunk[d](s), cb)))
    for a in consts:
        ins.append(a)
        in_specs.append(pl.BlockSpec(a.shape, lambda b, s, nd=a.ndim: (0,) * nd))
    state_specs = [pl.BlockSpec((None,) + a.shape[1:], lambda b, s, nd=a.ndim: (b,) + (0,) * (nd - 1))
                   for a in states]
    ins += list(states)
    in_specs += state_specs
    out_struct = jax.ShapeDtypeStruct((bsz * n * L, out_width), F32)
    res = pl.pallas_call(
        kernel,
        out_shape=[out_struct, out_struct] + [jax.ShapeDtypeStruct(a.shape, F32) for a in states],
        grid=(bsz, n),
        in_specs=in_specs,
        out_specs=[pl.BlockSpec((L, out_width), lambda b, s, d=d: (b * n + chunk[d](s), 0)) for d in (0, 1)]
        + state_specs,
        compiler_params=_cparams(("parallel", "arbitrary")),
        name=name,
    )(*ins)
    return (res[0], res[1]), list(res[2:])


def _bidir_scan(kernel, ins_lat, ins_ctx, consts, states0, *, bsz, seq, ctx_len, out_width, name):
    o_ctx, st = _scan_call(kernel, bsz, ctx_len // CHUNK, ins_ctx, consts, states0, out_width, name + "_ctx")
    o_lat, _ = _scan_call(kernel, bsz, seq // CHUNK, ins_lat, consts, st, out_width, name + "_lat")
    return o_lat, o_ctx


def _s5_load_chunks(u_ref, kt):
    return jnp.concatenate([u_ref[pl.ds(t, kt, stride=S5_CHUNK), :].astype(BF16) for t in range(S5_CHUNK)],
                           axis=-1)


def _s5_state_kernel(u_ref, q_ref, zfr_ref, zfi_ref, zbr_ref, zbi_ref):
    kt = zfr_ref.shape[0]
    z = _dot(_s5_load_chunks(u_ref, kt), q_ref[...])
    w = z.shape[1] // 4
    for i, ref in enumerate((zfr_ref, zfi_ref, zbr_ref, zbi_ref)):
        ref[...] = z[:, i * w:(i + 1) * w]


def _s5_rec_kernel(*refs, bsz, nc_tiles, nl_tiles):
    zc, zl = refs[0:4], refs[4:8]
    pf_ref, pb_ref = refs[8:10]
    hc, hl = refs[10:14], refs[14:18]
    pw = ((pf_ref[0], pf_ref[1]), (pb_ref[0], pb_ref[1]))
    sub = lax.broadcasted_iota(jnp.int32, pw[0][0].shape, 0)

    def cmul(a, b):
        return a[0] * b[0] - a[1] * b[1], a[0] * b[1] + a[1] * b[0]

    def cadd(a, b):
        return a[0] + b[0], a[1] + b[1]

    def scan_tile(z, carry, d):
        p = pw[d]
        h = z
        for sh in (1, 2, 4):
            prow = sh - 1 if d == 0 else 8 - sh
            keep = (sub >= sh) if d == 0 else (sub < 8 - sh)
            shift = sh if d == 0 else 8 - sh
            moved = tuple(jnp.where(keep, pltpu.roll(v, shift, 0), 0.0) for v in h)
            h = cadd(h, cmul((p[0][prow:prow + 1, :], p[1][prow:prow + 1, :]), moved))
        full = cadd(h, cmul(p, carry))
        edge, shift, last = (0, 1, 7) if d == 0 else (7, 7, 0)
        before = tuple(jnp.where(sub == edge, c, pltpu.roll(v, shift, 0)) for v, c in zip(full, carry))
        return before, tuple(v[last:last + 1, :] for v in full)

    def walk(z_refs, h_refs, n_tiles, carries):
        def body(i, carries):
            out = []
            for b in range(bsz):
                for d in (0, 1):
                    t = i if d == 0 else n_tiles - 1 - i
                    r0 = pl.multiple_of((b * n_tiles + t) * 8, 8)
                    z = (z_refs[2 * d][pl.ds(r0, 8), :], z_refs[2 * d + 1][pl.ds(r0, 8), :])
                    before, new = scan_tile(z, carries[2 * b + d], d)
                    h_refs[2 * d][pl.ds(r0, 8), :] = before[0]
                    h_refs[2 * d + 1][pl.ds(r0, 8), :] = before[1]
                    out.append(new)
            return tuple(out)
        return lax.fori_loop(0, n_tiles, body, carries)

    zero = jnp.zeros((1, sub.shape[1]), F32)
    carries = walk(zc, hc, nc_tiles, tuple((zero, zero) for _ in range(2 * bsz)))
    walk(zl, hl, nl_tiles, carries)


def _s5_out_kernel(u_ref, hfr, hfi, hbr, hbi, m_ref, r_ref, y_ref):
    kt = hfr.shape[0]
    y = _dot(_s5_load_chunks(u_ref, kt), m_ref[...])
    for i, h_ref in enumerate((hfr, hfi, hbr, hbi)):
        y += _dot(h_ref[...].astype(BF16), r_ref[i])
    w = y_ref.shape[1]
    for t in range(S5_CHUNK):
        y_ref[pl.ds(t, kt, stride=S5_CHUNK), :] = y[:, t * w:(t + 1) * w]


def _s5_matrices(lam_re, lam_im, log_step, b_re, b_im, c_re, c_im):
    hi = lax.Precision.HIGHEST
    L = S5_CHUNK
    step = jnp.exp(log_step)[..., None]
    n = jnp.arange(L + 1, dtype=F32)[:, None, None, None]
    mag = jnp.exp(n * (lam_re * step)[None])
    ang = n * (lam_im * step)[None]
    p_re, p_im = mag * jnp.cos(ang), mag * jnp.sin(ang)
    ab_re, ab_im = p_re[1], p_im[1]
    den = lam_re * lam_re + lam_im * lam_im
    coef_re = ((ab_re - 1.0) * lam_re + ab_im * lam_im) / den
    coef_im = (ab_im * lam_re - (ab_re - 1.0) * lam_im) / den
    e_re = p_re[:L] * coef_re - p_im[:L] * coef_im
    e_im = p_re[:L] * coef_im + p_im[:L] * coef_re
    ce_re = c_re[None] * e_re[..., None, :] - c_im[None] * e_im[..., None, :]
    ce_im = c_re[None] * e_im[..., None, :] + c_im[None] * e_re[..., None, :]
    kern = (jnp.einsum('ndgps,dgsq->ndgpq', ce_re, b_re, precision=hi)
            - jnp.einsum('ndgps,dgsq->ndgpq', ce_im, b_im, precision=hi))
    lag = jnp.arange(L)[None, :] - jnp.arange(L)[:, None]
    toe = (lag[None] == jnp.arange(L)[:, None, None]).astype(F32)
    m = jnp.einsum('nst,ndgpq->dgsqtp', toe, kern, precision=hi)
    w_re, w_im = e_re[::-1], e_im[::-1]
    q_re = (w_re[..., None] * b_re[None] - w_im[..., None] * b_im[None]).transpose(1, 2, 0, 4, 3)
    q_im = (w_re[..., None] * b_im[None] + w_im[..., None] * b_re[None]).transpose(1, 2, 0, 4, 3)
    a_re, a_im = p_re[1:], p_im[1:]
    ca_re = c_re[None] * a_re[..., None, :] - c_im[None] * a_im[..., None, :]
    ca_im = c_re[None] * a_im[..., None, :] + c_im[None] * a_re[..., None, :]
    r_re = ca_re.transpose(1, 2, 4, 0, 3)
    r_im = (-ca_im).transpose(1, 2, 4, 0, 3)
    nl = L * jnp.arange(1, 9, dtype=F32)[:, None, None, None]
    t_mag = jnp.exp(nl * (lam_re * step)[None])
    t_ang = nl * (lam_im * step)[None]
    return m, q_re, q_im, r_re, r_im, t_mag * jnp.cos(t_ang), t_mag * jnp.sin(t_ang)


def _s5_layer(u_lat, u_ctx, params, *, bsz, seq, ctx_len):
    c = u_lat.shape[1]
    L, pp, ns = S5_CHUNK, S5_GROUP, S5_STATE
    g = c // pp
    m, q_re, q_im, r_re, r_im, t_re, t_im = _s5_matrices(*params)
    gl = S5_LANE_TILE // pp
    nb = g // gl
    kw = L * S5_LANE_TILE
    sw = gl * ns
    lanes = g * ns
    eye = jnp.eye(gl, dtype=F32)
    m_tot = (m[0] + m[1][:, ::-1, :, ::-1, :]).reshape(nb, gl, L, pp, L, pp)
    m_blk = (m_tot.transpose(0, 2, 1, 3, 4, 5)[:, :, :, :, :, None, :]
             * eye[None, None, :, None, None, :, None]).reshape(nb, kw, kw).astype(BF16)
    qs = [q_re[0], q_im[0], q_re[1][:, ::-1], q_im[1][:, ::-1]]
    q_blk = jnp.concatenate(
        [(q.reshape(nb, gl, L, pp, ns).transpose(0, 2, 1, 3, 4)[:, :, :, :, None, :]
          * eye[None, None, :, None, :, None]).reshape(nb, kw, sw) for q in qs], axis=-1).astype(BF16)
    rs = [r_re[0], r_im[0], r_re[1][:, :, ::-1], r_im[1][:, :, ::-1]]
    r_blk = jnp.stack(
        [(r.reshape(nb, gl, ns, L, pp)[:, :, :, :, None, :]
          * eye[None, :, None, None, :, None]).reshape(nb, sw, kw) for r in rs], axis=1).astype(BF16)
    pf = jnp.stack([t_re[:, 0], t_im[:, 0]]).reshape(2, 8, lanes)
    pb = jnp.stack([t_re[::-1, 1], t_im[::-1, 1]]).reshape(2, 8, lanes)

    kt = S5_ROW_TILE
    u_spec = pl.BlockSpec((kt * L, S5_LANE_TILE), lambda b, i: (i, b))
    h_spec = pl.BlockSpec((kt, sw), lambda b, i: (i, b))
    cp2 = _cparams(("parallel", "parallel"))

    def state(u):
        rows = u.shape[0] // L
        return pl.pallas_call(
            _s5_state_kernel,
            out_shape=[jax.ShapeDtypeStruct((rows, lanes), F32)] * 4,
            grid=(nb, rows // kt),
            in_specs=[u_spec, pl.BlockSpec((None, kw, 4 * sw), lambda b, i: (b, 0, 0))],
            out_specs=[h_spec] * 4,
            compiler_params=cp2,
            name="s5_state",
        )(u, q_blk)

    z_c, z_l = state(u_ctx), state(u_lat)
    lb = S5_LANES
    rows_c, rows_l = z_c[0].shape[0], z_l[0].shape[0]
    h = pl.pallas_call(
        functools.partial(_s5_rec_kernel, bsz=bsz, nc_tiles=rows_c // bsz // 8, nl_tiles=rows_l // bsz // 8),
        out_shape=[jax.ShapeDtypeStruct((rows_c, lanes), F32)] * 4 + [jax.ShapeDtypeStruct((rows_l, lanes), F32)] * 4,
        grid=(lanes // lb,),
        in_specs=[pl.BlockSpec((rows_c, lb), lambda j: (0, j))] * 4
        + [pl.BlockSpec((rows_l, lb), lambda j: (0, j))] * 4
        + [pl.BlockSpec((2, 8, lb), lambda j: (0, 0, j))] * 2,
        out_specs=[pl.BlockSpec((rows_c, lb), lambda j: (0, j))] * 4
        + [pl.BlockSpec((rows_l, lb), lambda j: (0, j))] * 4,
        compiler_params=_cparams(("parallel",)),
        name="s5_recurrence",
    )(*z_c, *z_l, pf, pb)

    def readout(u, hs):
        rows = u.shape[0] // L
        return pl.pallas_call(
            _s5_out_kernel,
            out_shape=jax.ShapeDtypeStruct(u.shape, F32),
            grid=(nb, rows // kt),
            in_specs=[u_spec] + [h_spec] * 4
            + [pl.BlockSpec((None, kw, kw), lambda b, i: (b, 0, 0)),
               pl.BlockSpec((None, 4, sw, kw), lambda b, i: (b, 0, 0, 0))],
            out_specs=u_spec,
            compiler_params=cp2,
            name="s5_out",
        )(u, *hs, m_blk, r_blk)

    return readout(u_lat, h[4:]), readout(u_ctx, h[:4])


def _head_norm(o, g_ref, center, dh):
    outs = []
    for h in range(HEADS):
        oh = o[:, h * dh:(h + 1) * dh]
        if center:
            oh = oh - jnp.mean(oh, axis=-1, keepdims=True)
        outs.append(oh * lax.rsqrt(jnp.mean(oh * oh, axis=-1, keepdims=True) + EPS))
    return jnp.concatenate(outs, axis=-1) * g_ref[...]


def _finish(x_ref, a, b, wout_ref, mod_ref, gpost_ref, o_ref, n_r):
    half = a.shape[1]
    b = b.astype(BF16)
    if n_r:
        b = _dot(_perm_matrix(b.shape[0], n_r, False), b).astype(BF16)
    y = _dot(a.astype(BF16), wout_ref[0:half, :]) + _dot(b, wout_ref[half:2 * half, :])
    out = _tile2d(x_ref) + mod_ref[5:6, :] * _rms(y) * gpost_ref[1:2, :]
    o_ref[...] = out.reshape(o_ref.shape)


def _even_merge_kernel(x_ref, ga_ref, oaf_ref, oab_ref, gb_ref, obf_ref, obb_ref, na_ref, nb_ref,
                       wout_ref, mod_ref, gpost_ref, o_ref, *, dh, n_r):
    a = _silu(_tile2d(ga_ref)) * _head_norm(_tile2d(oaf_ref) + _tile2d(oab_ref), na_ref, False, dh)
    b = _sigmoid(_tile2d(gb_ref)) * _head_norm(_tile2d(obf_ref) + _tile2d(obb_ref), nb_ref, True, dh)
    _finish(x_ref, a, b, wout_ref, mod_ref, gpost_ref, o_ref, n_r)


def _odd_merge_kernel(x_ref, u_ref, ys_ref, gb_ref, obf_ref, obb_ref, sd_ref, wglu_ref, bglu_ref, nb_ref,
                      wout_ref, mod_ref, gpost_ref, o_ref, *, dh, n_r):
    y = _tile2d(ys_ref) + sd_ref[...] * _tile2d(u_ref)
    g = jax.nn.gelu(y)
    a = g * _sigmoid(_dot(g.astype(BF16), wglu_ref[...]) + bglu_ref[...])
    b = _silu(_tile2d(gb_ref)) * _head_norm(_tile2d(obf_ref) + _tile2d(obb_ref), nb_ref, True, dh)
    _finish(x_ref, a, b, wout_ref, mod_ref, gpost_ref, o_ref, n_r)


def _merge(kernel, s, rm_lat, cm_lat, rm_ctx, cm_ctx, consts, w_out, mod, gpost, *, bsz, seq, ctx_len, name):
    d = s.shape[1]
    half = d // 2
    rows = seq // GRID_W
    rt = min(MERGE_ROWS, rows)
    n_rh = rows // rt
    ncb = GRID_W // COL_TILE
    per_b = n_rh * ncb
    const_specs = [pl.BlockSpec(a.shape, lambda i, nd=a.ndim: (0,) * nd) for a in consts] \
        + [pl.BlockSpec((d, d), lambda i: (0, 0))]
    gpost_spec = pl.BlockSpec((3, d), lambda i: (0, 0))
    cp = _cparams(("parallel",))

    rm_idx = lambda i: ((i // per_b) * n_rh + (i % per_b) // ncb, i % ncb)
    cm_idx = lambda i: ((i // per_b) * ncb + i % ncb, (i % per_b) // ncb)
    s3 = s.reshape(s.shape[0] // GRID_W, GRID_W, d)
    ins = [s3] + [a.reshape(bsz * rows, GRID_W, a.shape[1]) for a, _ in rm_lat] \
        + [a.reshape(bsz * GRID_W, rows, a.shape[1]) for a, _ in cm_lat]
    in_specs = [pl.BlockSpec((rt, COL_TILE, d), lambda i: rm_idx(i) + (0,))] \
        + [pl.BlockSpec((rt, COL_TILE, half), lambda i, cb=cb: rm_idx(i) + (cb,)) for _, cb in rm_lat] \
        + [pl.BlockSpec((COL_TILE, rt, half), lambda i, cb=cb: cm_idx(i) + (cb,)) for _, cb in cm_lat]
    n_out_groups = s3.shape[0] if rm_ctx is not None else bsz * rows
    out = pl.pallas_call(
        functools.partial(kernel, n_r=rt),
        out_shape=jax.ShapeDtypeStruct((n_out_groups, GRID_W, d), F32),
        grid=(bsz * per_b,),
        in_specs=in_specs + const_specs
        + [pl.BlockSpec((None, N_MOD, d), lambda i: (i // per_b, 0, 0)), gpost_spec],
        out_specs=pl.BlockSpec((rt, COL_TILE, d), lambda i: rm_idx(i) + (0,)),
        compiler_params=cp,
        name=name + "_lat",
    )(*ins, *consts, w_out, mod, gpost)
    out = out.reshape(n_out_groups * GRID_W, d)
    if rm_ctx is None:
        return out

    tm = rt * COL_TILE
    t0 = bsz * seq // tm
    tiles = list(rm_ctx) + list(cm_ctx)
    n_in = 1 + len(tiles) + len(consts) + 3
    inner = functools.partial(kernel, n_r=0)
    return pl.pallas_call(
        lambda *refs: inner(*refs[:n_in], refs[n_in + 1]),
        out_shape=jax.ShapeDtypeStruct(out.shape, F32),
        grid=(bsz * ctx_len // tm,),
        in_specs=[pl.BlockSpec((tm, d), lambda i: (t0 + i, 0))]
        + [pl.BlockSpec((tm, half), lambda i, cb=cb: (i, cb)) for _, cb in tiles] + const_specs
        + [pl.BlockSpec((None, N_MOD, d), lambda i: (bsz, 0, 0)), gpost_spec,
           pl.BlockSpec(memory_space=pl.ANY)],
        out_specs=pl.BlockSpec((tm, d), lambda i: (t0 + i, 0)),
        input_output_aliases={n_in: 0},
        compiler_params=cp,
        name=name + "_ctx",
    )(s, *[a for a, _ in tiles], *consts, w_out, mod, gpost, out)


def kernel(x, c, ctx, c_ctx, w_mod, b_mod, norm_pre, norm_post, ffn_w_gate, ffn_w_up, ffn_w_down, ev_w_in, ev_w_out, gla_w_gate, gla_b_gate, gla_norm, ml_conv_w, ml_conv_b, ml_b_gates, ml_norm, od_w_in, od_w_out, s5_lam_re, s5_lam_im, s5_log_step, s5_b_re, s5_b_im, s5_c_re, s5_c_im, s5_d, s5_w_glu, s5_b_glu, ret_log_decay, ret_norm):
    bsz, seq, d = x.shape
    ctx_len = ctx.shape[1]
    depth = w_mod.shape[0]
    half = d // 2
    dh = half // HEADS
    gla_dk = dh // 2
    d_ff = ffn_w_gate.shape[-1]
    tm = TOKEN_TILE
    rows = seq // GRID_W
    assert seq % tm == 0 and (bsz * ctx_len) % tm == 0 and seq == rows * GRID_W and ctx_len % CHUNK == 0
    assert 2 * bsz == 8 and rows % 8 == 0 and seq % CHUNK == 0 and half == PROJ_TILE
    s5_tile = S5_CHUNK * S5_ROW_TILE
    assert ctx_len % (8 * S5_CHUNK) == 0 and (bsz * ctx_len) % s5_tile == 0 and (bsz * seq) % s5_tile == 0

    tiles_x = bsz * seq // tm
    tiles_all = tiles_x + bsz * ctx_len // tm
    tiles_per_batch = seq // tm
    mod_of_tile = lambda i: jnp.minimum(i // tiles_per_batch, bsz)

    wg, wu, wd = ffn_w_gate.astype(BF16), ffn_w_up.astype(BF16), ffn_w_down.astype(BF16)

    cc = jnp.concatenate([c, c_ctx[None], jnp.zeros((8 - bsz - 1, d), F32)], axis=0)
    mod_all = _modulation(cc, w_mod, b_mod).reshape(depth, 8, N_MOD, d)

    s = jnp.concatenate([x.reshape(bsz * seq, d), ctx.reshape(bsz * ctx_len, d)], axis=0)
    dims = dict(bsz=bsz, seq=seq, ctx_len=ctx_len)

    for layer in range(depth):
        need_ctx = layer < depth - 1
        mod = mod_all[layer]
        gpre, gpost = norm_pre[layer], norm_post[layer]
        ffn = functools.partial(_ffn, mod=mod, gpre=gpre, gpost=gpost, wg=wg, wu=wu, wd=wd, layer=layer,
                                mod_of_tile=mod_of_tile)
        s = ffn(s, which=0, sub=0, n_tiles=tiles_all)

        if layer % 2 == 0:
            e = layer // 2
            w = ev_w_in[e]
            qk_w = 2 * HEADS * gla_dk
            n_gate = 4 * HEADS
            c_low = qk_w + 2 * half
            c_ml = c_low + 2 * GLA_RANK
            c_g = c_ml + 4 * half
            w_in = jnp.concatenate([w[:, :c_low], w[:, c_ml:c_g]], axis=1).astype(BF16)
            w_small = jnp.concatenate([w[:, c_low:c_ml], jnp.zeros((d, SMALL_W - 2 * GLA_RANK), F32),
                                       w[:, c_g:], jnp.zeros((d, SMALL_W - n_gate), F32)], axis=1).astype(BF16)
            (pa, pb, psa, psb), (pa_c, pb_c, psa_c, psb_c) = _proj(s, mod, gpre, w_in, w_small,
                                                                   n_a=c_low // PROJ_TILE, **dims)

            st0 = [jnp.zeros((bsz, 2, HEADS, dh, gla_dk), F32)]
            gla_ins = lambda p, ps: [(p, qk_w // 2, 0), (p, qk_w // 2, qk_w // 2), (p, half, qk_w), (ps, SMALL_W, 0)]
            oa, oa_c = _bidir_scan(
                functools.partial(_gla_kernel, L=CHUNK, dk=gla_dk, dv=dh, gated=True,
                                  q_scale=float(gla_dk) ** -0.5, k_scale=1.0),
                gla_ins(pa, psa), gla_ins(pa_c, psa_c), [gla_w_gate[e].astype(BF16), gla_b_gate[e]], st0,
                out_width=half, name="gla", **dims)

            scale = jnp.concatenate([jnp.ones((half,), F32), jnp.full((half,), float(dh) ** -0.5, F32)])
            qk = _conv(pb, ml_conv_w[e], ml_conv_b[e], scale, bsz=bsz, n=seq)
            qk_c = _conv(pb_c, ml_conv_w[e], ml_conv_b[e], scale, bsz=bsz, n=ctx_len)
            bias = jnp.zeros((1, SMALL_W), F32).at[0, :n_gate].set(ml_b_gates[e].reshape(-1))
            ml_ins = lambda qk_, p, ps: [(qk_, half, 0), (qk_, half, half), (p, half, 2 * half), (ps, SMALL_W, 0)]
            st0 = [jnp.zeros((bsz, 2, HEADS, dh, dh), F32), jnp.zeros((bsz, 2, HEADS, 8, dh), F32)]
            ob, ob_c = _bidir_scan(
                functools.partial(_mlstm_kernel, L=CHUNK, dh=dh),
                ml_ins(qk, pb, psb), ml_ins(qk_c, pb_c, psb_c), [bias], st0,
                out_width=half, name="mlstm", **dims)

            ga_cb, gb_cb = (qk_w + half) // half, 3
            s = _merge(
                functools.partial(_even_merge_kernel, dh=dh), s,
                [(pa, ga_cb), (oa[0], 0), (oa[1], 0)], [(pb, gb_cb), (ob[0], 0), (ob[1], 0)],
                [(pa_c, ga_cb), (oa_c[0], 0), (oa_c[1], 0)] if need_ctx else None,
                [(pb_c, gb_cb), (ob_c[0], 0), (ob_c[1], 0)],
                [gla_norm[e].reshape(1, half), ml_norm[e].reshape(1, half)],
                ev_w_out[e].astype(BF16), mod, gpost, name="even_merge", **dims)
        else:
            o = layer // 2
            (pa, pb), (pa_c, pb_c) = _proj(s, mod, gpre, od_w_in[o].astype(BF16), None, n_a=1, **dims)

            ys, ys_c = _s5_layer(pa, pa_c, (s5_lam_re[o], s5_lam_im[o], s5_log_step[o], s5_b_re[o], s5_b_im[o],
                                            s5_c_re[o], s5_c_im[o]), **dims)

            lg = jnp.repeat(-jnp.exp(ret_log_decay[o]), dh, axis=1)
            ret_ins = lambda p: [(p, half, 0), (p, half, half), (p, half, 2 * half)]
            st0 = [jnp.zeros((bsz, 2, HEADS, dh, dh), F32)]
            od, od_c = _bidir_scan(
                functools.partial(_gla_kernel, L=CHUNK, dk=dh, dv=dh, gated=False,
                                  q_scale=1.0, k_scale=float(dh) ** -0.5),
                ret_ins(pb), ret_ins(pb_c), [lg], st0, out_width=half, name="retention", **dims)

            s = _merge(
                functools.partial(_odd_merge_kernel, dh=dh), s,
                [(pa, 0), (ys, 0)], [(pb, 3), (od[0], 0), (od[1], 0)],
                [(pa_c, 0), (ys_c, 0)] if need_ctx else None, [(pb_c, 3), (od_c[0], 0), (od_c[1], 0)],
                [s5_d[o].reshape(1, half), s5_w_glu[o].astype(BF16), s5_b_glu[o].reshape(1, half),
                 ret_norm[o].reshape(1, half)],
                od_w_out[o].astype(BF16), mod, gpost, name="odd_merge", **dims)

        s = ffn(s, which=1, sub=2, n_tiles=tiles_all if need_ctx else tiles_x)

    return s[:bsz * seq].reshape(bsz, seq, d)
```

```python
import functools

import jax
import jax.numpy as jnp
from jax import lax
from jax.experimental import pallas as pl
from jax.experimental.pallas import tpu as pltpu

F32 = jnp.float32
BF16 = jnp.bfloat16

EPS = 1e-6
GRID_W = 64
COL_TILE = 8
CHUNK = 64
FFN_RES = 0.5
N_MOD = 9
HEADS = 4
GLA_RANK = 16
GLA_GATE_NORM = 16.0
S5_GROUP = 16
S5_STATE = 64
S5_CHUNK = 8
S5_LANE_TILE = 128
S5_ROW_TILE = 128
SMALL_W = 128

TOKEN_TILE = 512
MERGE_ROWS = 32
FF_TILE = 512
PROJ_TILE = 1024
MOD_TILE = 1024
CONV_TILE = 256
CONV_ROWS = 256
S5_LANES = 256
VMEM_LIMIT = 56 * 1024 * 1024


def _cparams(sem):
    return pltpu.CompilerParams(dimension_semantics=sem, vmem_limit_bytes=VMEM_LIMIT)


def _rms(x):
    return x * lax.rsqrt(jnp.mean(x * x, axis=-1, keepdims=True) + EPS)


def _sigmoid(x):
    return 1.0 / (1.0 + jnp.exp(-x))


def _silu(x):
    return x * _sigmoid(x)


def _log_sigmoid(x):
    return jnp.minimum(x, 0.0) - jnp.log(1.0 + jnp.exp(-jnp.abs(x)))


def _dot(a, b):
    return jnp.dot(a, b, preferred_element_type=F32)


def _dot_nt(a, b):
    return lax.dot_general(a, b, (((1,), (1,)), ((), ())), preferred_element_type=F32)


def _dot_tn(a, b):
    return lax.dot_general(a, b, (((0,), (0,)), ((), ())), preferred_element_type=F32)


def _dot01(t01, x):
    x0 = x.astype(BF16)
    r1 = x - x0.astype(F32)
    x1 = r1.astype(BF16)
    x2 = (r1 - x1.astype(F32)).astype(BF16)
    return _dot(t01, x0) + _dot(t01, x1) + _dot(t01, x2)


def _perm_matrix(n, n_r, to_col_major):
    i = lax.broadcasted_iota(jnp.int32, (n, n), 0)
    c = lax.broadcasted_iota(jnp.int32, (n, n), 1)
    if to_col_major:
        hit = (i // n_r == c % COL_TILE) & (i % n_r == c // COL_TILE)
    else:
        hit = (i // COL_TILE == c % n_r) & (i % COL_TILE == c // n_r)
    return jnp.where(hit, 1.0, 0.0).astype(BF16)


def _tile2d(ref):
    v = ref[...]
    return v.reshape(-1, v.shape[-1])


def _mod_kernel(c_ref, w_ref, b_ref, o_ref):
    a = _silu(c_ref[...]).astype(BF16)
    o_ref[...] = _dot(a, w_ref[...].astype(BF16)) + b_ref[...]


def _modulation(cc, w_mod, b_mod):
    depth, d, nm = w_mod.shape
    return pl.pallas_call(
        _mod_kernel,
        out_shape=jax.ShapeDtypeStruct((depth, 8, nm), F32),
        grid=(depth, nm // MOD_TILE),
        in_specs=[pl.BlockSpec((8, d), lambda l, j: (0, 0)),
                  pl.BlockSpec((None, d, MOD_TILE), lambda l, j: (l, 0, j)),
                  pl.BlockSpec((None, 1, MOD_TILE), lambda l, j: (l, 0, j))],
        out_specs=pl.BlockSpec((None, 8, MOD_TILE), lambda l, j: (l, 0, j)),
        compiler_params=_cparams(("parallel", "parallel")),
        name="modulation",
    )(cc, w_mod, b_mod.reshape(depth, 1, nm))


def _ffn_kernel(x_ref, mod_ref, gpre_ref, gpost_ref, wg_ref, wu_ref, wd_ref, o_ref, h_ref, acc_ref, *, sub, nff,
                tail):
    j = pl.program_id(1)

    @pl.when(j == 0)
    def _():
        shift = mod_ref[3 * sub:3 * sub + 1, :]
        scale = mod_ref[3 * sub + 1:3 * sub + 2, :]
        h = _rms(x_ref[...]) * gpre_ref[sub:sub + 1, :] * (1.0 + scale) + shift
        h_ref[...] = h.astype(BF16)
        acc_ref[...] = jnp.zeros_like(acc_ref)

    def hidden_step(valid):
        h = h_ref[...]
        t = _silu(_dot(h, wg_ref[...])) * _dot(h, wu_ref[...])
        wd = wd_ref[...]
        if valid is not None:
            t = jnp.where(lax.broadcasted_iota(jnp.int32, t.shape, 1) < valid, t, 0.0)
            wd = jnp.where(lax.broadcasted_iota(jnp.int32, wd.shape, 0) < valid, wd, jnp.zeros_like(wd))
        acc_ref[...] += _dot(t.astype(BF16), wd)

    if tail:
        pl.when(j < nff - 1)(lambda: hidden_step(None))
        pl.when(j == nff - 1)(lambda: hidden_step(tail))
    else:
        hidden_step(None)

    @pl.when(j == nff - 1)
    def _():
        gate = mod_ref[3 * sub + 2:3 * sub + 3, :]
        o_ref[...] = x_ref[...] + FFN_RES * gate * _rms(acc_ref[...]) * gpost_ref[sub:sub + 1, :]


def _ffn(s, mod, gpre, gpost, wg, wu, wd, *, layer, which, sub, n_tiles, mod_of_tile, out_tile0=0, out_tiles=None,
         prev=None):
    d = s.shape[1]
    d_ff = wg.shape[-1]
    nff = pl.cdiv(d_ff, FF_TILE)
    tm = TOKEN_TILE
    out_tiles = n_tiles if out_tiles is None else out_tiles
    kernel = functools.partial(_ffn_kernel, sub=sub, nff=nff, tail=d_ff % FF_TILE)
    ins = [s, mod, gpre, gpost, wg, wu, wd]
    in_specs = [pl.BlockSpec((tm, d), lambda i, j: (i, 0)),
                pl.BlockSpec((None, N_MOD, d), lambda i, j: (mod_of_tile(out_tile0 + i), 0, 0)),
                pl.BlockSpec((3, d), lambda i, j: (0, 0)),
                pl.BlockSpec((3, d), lambda i, j: (0, 0)),
                pl.BlockSpec((None, None, d, FF_TILE), lambda i, j: (layer, which, 0, j)),
                pl.BlockSpec((None, None, d, FF_TILE), lambda i, j: (layer, which, 0, j)),
                pl.BlockSpec((None, None, FF_TILE, d), lambda i, j: (layer, which, j, 0))]
    aliases = {}
    if prev is not None:
        n_in = len(ins)
        inner = kernel
        kernel = lambda *refs: inner(*refs[:n_in], *refs[n_in + 1:])
        ins.append(prev)
        in_specs.append(pl.BlockSpec(memory_space=pl.ANY))
        aliases = {n_in: 0}
    return pl.pallas_call(
        kernel,
        out_shape=jax.ShapeDtypeStruct((out_tiles * tm, d), F32),
        grid=(n_tiles, nff),
        in_specs=in_specs,
        out_specs=pl.BlockSpec((tm, d), lambda i, j: (out_tile0 + i, 0)),
        scratch_shapes=[pltpu.VMEM((tm, d), BF16), pltpu.VMEM((tm, d), F32)],
        input_output_aliases=aliases,
        compiler_params=_cparams(("parallel", "arbitrary")),
        name="ffn",
    )(*ins)


def _proj_kernel(*refs, n_a, n_r, has_small):
    if has_small:
        x_ref, mod_ref, gpre_ref, w_ref, ws_ref, pa_ref, pb_ref, psa_ref, psb_ref, h_ref, hp_ref = refs
    else:
        x_ref, mod_ref, gpre_ref, w_ref, pa_ref, pb_ref, h_ref, hp_ref = refs
    j = pl.program_id(1)
    tm = h_ref.shape[0]

    @pl.when(j == 0)
    def _():
        x = _tile2d(x_ref)
        h = (_rms(x) * gpre_ref[1:2, :] * (1.0 + mod_ref[4:5, :]) + mod_ref[3:4, :]).astype(BF16)
        hp = _dot(_perm_matrix(tm, n_r, True), h).astype(BF16) if n_r else h
        h_ref[...] = h
        hp_ref[...] = hp
        if has_small:
            psa_ref[...] = _dot(h, ws_ref[:, :SMALL_W]).reshape(psa_ref.shape)
            psb_ref[...] = _dot(hp, ws_ref[:, SMALL_W:])

    @pl.when(j < n_a)
    def _():
        pa_ref[...] = _dot(h_ref[...], w_ref[...]).reshape(pa_ref.shape)

    @pl.when(j >= n_a)
    def _():
        pb_ref[...] = _dot(hp_ref[...], w_ref[...])


def _proj(s, mod, gpre, w, w_small, *, n_a, bsz, seq, ctx_len):
    d = s.shape[1]
    rows = seq // GRID_W
    tn = PROJ_TILE
    n_tiles_n = w.shape[1] // tn
    n_b = n_tiles_n - n_a
    has_small = w_small is not None
    w_ins = [w] + ([w_small] if has_small else [])
    w_specs = [pl.BlockSpec((d, tn), lambda i, j: (0, j))]
    if has_small:
        w_specs.append(pl.BlockSpec((d, 2 * SMALL_W), lambda i, j: (0, 0)))
    ja = lambda j: jnp.minimum(j, n_a - 1)
    jb = lambda j: jnp.maximum(j - n_a, 0)
    cp = _cparams(("parallel", "arbitrary"))

    tm = rows * COL_TILE
    ncb = GRID_W // COL_TILE
    s3 = s.reshape(s.shape[0] // GRID_W, GRID_W, d)
    out_shape = [jax.ShapeDtypeStruct((bsz * rows, GRID_W, n_a * tn), F32),
                 jax.ShapeDtypeStruct((bsz * seq, n_b * tn), F32)]
    out_specs = [pl.BlockSpec((rows, COL_TILE, tn), lambda i, j: (i // ncb, i % ncb, ja(j))),
                 pl.BlockSpec((tm, tn), lambda i, j: (i, jb(j)))]
    if has_small:
        out_shape += [jax.ShapeDtypeStruct((bsz * rows, GRID_W, SMALL_W), F32),
                      jax.ShapeDtypeStruct((bsz * seq, SMALL_W), F32)]
        out_specs += [pl.BlockSpec((rows, COL_TILE, SMALL_W), lambda i, j: (i // ncb, i % ncb, 0)),
                      pl.BlockSpec((tm, SMALL_W), lambda i, j: (i, 0))]
    lat = pl.pallas_call(
        functools.partial(_proj_kernel, n_a=n_a, n_r=rows, has_small=has_small),
        out_shape=out_shape,
        grid=(bsz * ncb, n_tiles_n),
        in_specs=[pl.BlockSpec((rows, COL_TILE, d), lambda i, j: (i // ncb, i % ncb, 0)),
                  pl.BlockSpec((None, N_MOD, d), lambda i, j: (i // ncb, 0, 0)),
                  pl.BlockSpec((3, d), lambda i, j: (0, 0))] + w_specs,
        out_specs=out_specs,
        scratch_shapes=[pltpu.VMEM((tm, d), BF16), pltpu.VMEM((tm, d), BF16)],
        compiler_params=cp,
        name="in_proj_lat",
    )(s3, mod, gpre, *w_ins)
    lat = [lat[0].reshape(bsz * seq, n_a * tn), lat[1]] \
        + ([lat[2].reshape(bsz * seq, SMALL_W), lat[3]] if has_small else [])

    tc = TOKEN_TILE
    n_c = bsz * ctx_len
    t0 = bsz * seq // tc
    out_shape = [jax.ShapeDtypeStruct((n_c, n_a * tn), F32), jax.ShapeDtypeStruct((n_c, n_b * tn), F32)]
    out_specs = [pl.BlockSpec((tc, tn), lambda i, j: (i, ja(j))), pl.BlockSpec((tc, tn), lambda i, j: (i, jb(j)))]
    if has_small:
        out_shape += [jax.ShapeDtypeStruct((n_c, SMALL_W), F32)] * 2
        out_specs += [pl.BlockSpec((tc, SMALL_W), lambda i, j: (i, 0))] * 2
    ctx = pl.pallas_call(
        functools.partial(_proj_kernel, n_a=n_a, n_r=0, has_small=has_small),
        out_shape=out_shape,
        grid=(n_c // tc, n_tiles_n),
        in_specs=[pl.BlockSpec((tc, d), lambda i, j: (t0 + i, 0)),
                  pl.BlockSpec((None, N_MOD, d), lambda i, j: (bsz, 0, 0)),
                  pl.BlockSpec((3, d), lambda i, j: (0, 0))] + w_specs,
        out_specs=out_specs,
        scratch_shapes=[pltpu.VMEM((tc, d), BF16), pltpu.VMEM((tc, d), BF16)],
        compiler_params=cp,
        name="in_proj_ctx",
    )(s, mod, gpre, *w_ins)
    return lat, list(ctx)


def _conv_kernel(x_ref, w_ref, b_ref, sc_ref, o_ref):
    n, ct = x_ref.shape
    ch = CONV_ROWS
    n_ch = n // ch
    r = lax.broadcasted_iota(jnp.int32, (ch, ct), 0)

    def body(c, carry):
        r0 = pl.multiple_of(c * ch, ch)
        x = x_ref[pl.ds(r0, ch), :]
        before = x_ref[pl.ds(pl.multiple_of(jnp.maximum(r0 - 8, 0), 8), 8), :][7:8, :]
        after = x_ref[pl.ds(pl.multiple_of(jnp.minimum(r0 + ch, n - 8), 8), 8), :][0:1, :]
        before = jnp.where(c == 0, 0.0, before)
        after = jnp.where(c == n_ch - 1, 0.0, after)
        prev = jnp.where(r == 0, before, pltpu.roll(x, 1, 0))
        nxt = jnp.where(r == ch - 1, after, pltpu.roll(x, ch - 1, 0))
        y = w_ref[1:2, :] * x + w_ref[0:1, :] * prev + w_ref[2:3, :] * nxt + b_ref[...]
        o_ref[pl.ds(r0, ch), :] = _silu(y) * sc_ref[...]
        return carry

    lax.fori_loop(0, n_ch, body, 0)


def _conv(p, conv_w, conv_b, scale, *, bsz, n):
    c = conv_w.shape[1]
    ct = CONV_TILE
    return pl.pallas_call(
        _conv_kernel,
        out_shape=jax.ShapeDtypeStruct((bsz * n, c), F32),
        grid=(bsz, c // ct),
        in_specs=[pl.BlockSpec((n, ct), lambda b, j: (b, j)),
                  pl.BlockSpec((3, ct), lambda b, j: (0, j)),
                  pl.BlockSpec((1, ct), lambda b, j: (0, j)),
                  pl.BlockSpec((1, ct), lambda b, j: (0, j))],
        out_specs=pl.BlockSpec((n, ct), lambda b, j: (b, j)),
        compiler_params=_cparams(("parallel", "parallel")),
        name="conv",
    )(p, conv_w, conv_b.reshape(1, c), scale.reshape(1, c))


def _gla_kernel(*refs, L, dk, dv, gated, q_scale, k_scale):
    if gated:
        qf, kf, vf, df, qb, kb, vb, db, wgate_ref, bgate_ref, st_in, of, ob, st_out = refs
    else:
        qf, kf, vf, qb, kb, vb, lg_ref, st_in, of, ob, st_out = refs
        df = db = None

    @pl.when(pl.program_id(1) == 0)
    def _():
        st_out[...] = st_in[...]

    row = lax.broadcasted_iota(jnp.int32, (L, L), 0)
    col = lax.broadcasted_iota(jnp.int32, (L, L), 1)
    trow = lax.broadcasted_iota(jnp.int32, (L, 1), 0)
    for d, (q_ref, k_ref, v_ref, d_ref, o_ref) in enumerate(((qf, kf, vf, df, of), (qb, kb, vb, db, ob))):
        mask = (row >= col) if d == 0 else (row <= col)
        mid = L // 2 if d == 0 else L - 1 - L // 2
        end = L - 1 if d == 0 else 0
        if gated:
            low = d_ref[:, d * GLA_RANK:(d + 1) * GLA_RANK].astype(BF16)
            z = _dot(low, wgate_ref[d]) + bgate_ref[d:d + 1, :]
            la = _log_sigmoid(z) * (1.0 / GLA_GATE_NORM)
            bcum = _dot01(mask.astype(BF16), la)
        else:
            cnt = (trow + 1) if d == 0 else (L - trow)
            bcum = cnt.astype(F32) * lg_ref[d:d + 1, :]
        for h in range(HEADS):
            b = bcum[:, h * dk:(h + 1) * dk]
            b_mid = b[mid:mid + 1, :]
            b_end = b[end:end + 1, :]
            q = q_ref[:, h * dk:(h + 1) * dk] * q_scale
            k = k_ref[:, h * dk:(h + 1) * dk] * k_scale
            v = v_ref[:, h * dv:(h + 1) * dv].astype(BF16)
            qs = (q * jnp.exp(b - b_mid)).astype(BF16)
            ks = (k * jnp.exp(b_mid - b)).astype(BF16)
            scores = jnp.where(mask, _dot_nt(qs, ks), 0.0)
            st = st_out[d, h]
            o = _dot(scores.astype(BF16), v) + _dot_nt((q * jnp.exp(b)).astype(BF16), st.astype(BF16))
            o_ref[:, h * dv:(h + 1) * dv] = o
            kd = (k * jnp.exp(b_end - b)).astype(BF16)
            st_out[d, h] = jnp.exp(b_end) * st + _dot_tn(v, kd)


def _mlstm_kernel(qf, kf, vf, gf, qb, kb, vb, gb, bias_ref, c_in, nm_in, of, ob, c_out, nm_out, *, L, dh):
    @pl.when(pl.program_id(1) == 0)
    def _():
        c_out[...] = c_in[...]
        nm_out[...] = nm_in[...]

    row = lax.broadcasted_iota(jnp.int32, (L, L), 0)
    col = lax.broadcasted_iota(jnp.int32, (L, L), 1)
    eye = row == col
    ones = jnp.ones((L, L), BF16)
    for d, (q_ref, k_ref, v_ref, g_ref, o_ref) in enumerate(((qf, kf, vf, gf, of), (qb, kb, vb, gb, ob))):
        mask = (row >= col) if d == 0 else (row <= col)
        cum_us = (row <= col) if d == 0 else (row >= col)
        end = L - 1 if d == 0 else 0
        g = g_ref[...] + bias_ref[...]
        fl = _log_sigmoid(g)
        bc_all = _dot01(mask.astype(BF16), fl)
        for h in range(HEADS):
            ci = d * 2 * HEADS + h
            cf = ci + HEADS
            i_col = g[:, ci:ci + 1]
            f_col = fl[:, cf:cf + 1]
            b_col = bc_all[:, cf:cf + 1]
            x = jnp.where(cum_us, f_col, 0.0) - jnp.where(eye, i_col, 0.0)
            drow = _dot01(ones, x)
            m_st = nm_out[d, h, 1:2, 0:1]
            dmat = jnp.where(mask, b_col - drow, -jnp.inf)
            inter = b_col + m_st
            m_t = jnp.maximum(inter, jnp.max(dmat, axis=-1, keepdims=True))
            q = q_ref[:, h * dh:(h + 1) * dh]
            k = k_ref[:, h * dh:(h + 1) * dh]
            v = v_ref[:, h * dh:(h + 1) * dh].astype(BF16)
            qh = q.astype(BF16)
            scores = _dot_nt(qh, k.astype(BF16)) * jnp.exp(dmat - m_t)
            w_inter = jnp.exp(inter - m_t)
            c_st = c_out[d, h]
            n_st = nm_out[d, h, 0:1, :]
            num = _dot(scores.astype(BF16), v) + w_inter * _dot(qh, c_st.astype(BF16))
            den = (jnp.sum(scores, axis=-1, keepdims=True)
                   + w_inter * jnp.sum(q * n_st, axis=-1, keepdims=True))
            o_ref[:, h * dh:(h + 1) * dh] = num / jnp.maximum(jnp.abs(den), jnp.exp(-m_t))
            b_end = b_col[end:end + 1, :]
            d_end = b_end - b_col + i_col
            m_new = jnp.maximum(b_end + m_st, jnp.max(d_end, axis=0, keepdims=True))
            kw = k * jnp.exp(d_end - m_new)
            decay = jnp.exp(b_end + m_st - m_new)
            c_out[d, h] = decay * c_st + _dot_tn(kw.astype(BF16), v)
            nm_out[d, h, 0:1, :] = decay * n_st + jnp.sum(kw, axis=0, keepdims=True)
            nm_out[d, h, 1:2, :] = jnp.broadcast_to(m_new, (1, dh))


def _scan_call(kernel, bsz, n, chunk_ins, consts, states, out_width, name):
    L = CHUNK
    chunk = (lambda s: s, lambda s: n - 1 - s)
    ins, in_specs = [], []
    for d in (0, 1):
        for a, width, coloff in chunk_ins:
            ins.append(a)
            in_specs.append(pl.BlockSpec((L, width), lambda b, s, d=d, cb=coloff // width: (b * n + chunk[d](s), cb)))
    for a in consts:
        ins.append(a)
        in_specs.append(pl.BlockSpec(a.shape, lambda b, s, nd=a.ndim: (0,) * nd))
    state_specs = [pl.BlockSpec((None,) + a.shape[1:], lambda b, s, nd=a.ndim: (b,) + (0,) * (nd - 1))
                   for a in states]
    ins += list(states)
    in_specs += state_specs
    out_struct = jax.ShapeDtypeStruct((bsz * n * L, out_width), F32)
    res = pl.pallas_call(
        kernel,
        out_shape=[out_struct, out_struct] + [jax.ShapeDtypeStruct(a.shape, F32) for a in states],
        grid=(bsz, n),
        in_specs=in_specs,
        out_specs=[pl.BlockSpec((L, out_width), lambda b, s, d=d: (b * n + chunk[d](s), 0)) for d in (0, 1)]
        + state_specs,
        compiler_params=_cparams(("parallel", "arbitrary")),
        name=name,
    )(*ins)
    return (res[0], res[1]), list(res[2:])


def _bidir_scan(kernel, ins_lat, ins_ctx, consts, states0, *, bsz, seq, ctx_len, out_width, name):
    o_ctx, st = _scan_call(kernel, bsz, ctx_len // CHUNK, ins_ctx, consts, states0, out_width, name + "_ctx")
    o_lat, _ = _scan_call(kernel, bsz, seq // CHUNK, ins_lat, consts, st, out_width, name + "_lat")
    return o_lat, o_ctx


def _s5_load_chunks(u_ref, kt):
    return jnp.concatenate([u_ref[pl.ds(t, kt, stride=S5_CHUNK), :].astype(BF16) for t in range(S5_CHUNK)],
                           axis=-1)


def _expand_blockdiag(c, rows_per_group, lanes_per_group, lane_period, n_out):
    r_n, k = c.shape
    kk = lax.broadcasted_iota(jnp.int32, (k, n_out), 0)
    jj = lax.broadcasted_iota(jnp.int32, (k, n_out), 1)
    sel = (kk // lanes_per_group == jj // lane_period) & (kk % lanes_per_group == jj % lanes_per_group)
    rr = lax.broadcasted_iota(jnp.int32, (r_n, n_out), 0)
    cc = lax.broadcasted_iota(jnp.int32, (r_n, n_out), 1)
    same = rr // rows_per_group == (cc % lane_period) // lanes_per_group
    return jnp.where(same, _dot(c, jnp.where(sel, 1.0, 0.0).astype(BF16)), 0.0).astype(BF16)


def _s5_state_kernel(u_ref, qc_ref, zfr_ref, zfi_ref, zbr_ref, zbi_ref, w_ref):
    kt = zfr_ref.shape[0]
    lt = S5_LANE_TILE

    @pl.when(pl.program_id(1) == 0)
    def _():
        for s in range(S5_CHUNK):
            w_ref[s * lt:(s + 1) * lt, :] = _expand_blockdiag(qc_ref[s], S5_GROUP, S5_STATE,
                                                              lt // S5_GROUP * S5_STATE, w_ref.shape[1])

    z = _dot(_s5_load_chunks(u_ref, kt), w_ref[...])
    w = z.shape[1] // 4
    for i, ref in enumerate((zfr_ref, zfi_ref, zbr_ref, zbi_ref)):
        ref[...] = z[:, i * w:(i + 1) * w]


def _s5_rec_kernel(*refs, bsz, nc_tiles, nl_tiles):
    zc, zl = refs[0:4], refs[4:8]
    pf_ref, pb_ref = refs[8:10]
    hc, hl = refs[10:14], refs[14:18]
    pw = ((pf_ref[0], pf_ref[1]), (pb_ref[0], pb_ref[1]))
    sub = lax.broadcasted_iota(jnp.int32, pw[0][0].shape, 0)

    def cmul(a, b):
        return a[0] * b[0] - a[1] * b[1], a[0] * b[1] + a[1] * b[0]

    def cadd(a, b):
        return a[0] + b[0], a[1] + b[1]

    def scan_tile(z, carry, d):
        p = pw[d]
        h = z
        for sh in (1, 2, 4):
            prow = sh - 1 if d == 0 else 8 - sh
            keep = (sub >= sh) if d == 0 else (sub < 8 - sh)
            shift = sh if d == 0 else 8 - sh
            moved = tuple(jnp.where(keep, pltpu.roll(v, shift, 0), 0.0) for v in h)
            h = cadd(h, cmul((p[0][prow:prow + 1, :], p[1][prow:prow + 1, :]), moved))
        full = cadd(h, cmul(p, carry))
        edge, shift, last = (0, 1, 7) if d == 0 else (7, 7, 0)
        before = tuple(jnp.where(sub == edge, c, pltpu.roll(v, shift, 0)) for v, c in zip(full, carry))
        return before, tuple(v[last:last + 1, :] for v in full)

    def walk(z_refs, h_refs, n_tiles, carries):
        def body(i, carries):
            out = []
            for b in range(bsz):
                for d in (0, 1):
                    t = i if d == 0 else n_tiles - 1 - i
                    r0 = pl.multiple_of((b * n_tiles + t) * 8, 8)
                    z = (z_refs[2 * d][pl.ds(r0, 8), :], z_refs[2 * d + 1][pl.ds(r0, 8), :])
                    before, new = scan_tile(z, carries[2 * b + d], d)
                    h_refs[2 * d][pl.ds(r0, 8), :] = before[0]
                    h_refs[2 * d + 1][pl.ds(r0, 8), :] = before[1]
                    out.append(new)
            return tuple(out)
        return lax.fori_loop(0, n_tiles, body, carries)

    zero = jnp.zeros((1, sub.shape[1]), F32)
    carries = walk(zc, hc, nc_tiles, tuple((zero, zero) for _ in range(2 * bsz)))
    walk(zl, hl, nl_tiles, carries)


def _s5_out_kernel(u_ref, hfr, hfi, hbr, hbi, bd_ref, rc_ref, y_ref, wm_ref, wr_ref):
    kt = hfr.shape[0]
    lt = S5_LANE_TILE

    @pl.when(pl.program_id(1) == 0)
    def _():
        for s in range(S5_CHUNK):
            for t in range(S5_CHUNK):
                wm_ref[s * lt:(s + 1) * lt, t * lt:(t + 1) * lt] = bd_ref[t - s + S5_CHUNK - 1]
        for i in range(4):
            wr_ref[i] = _expand_blockdiag(rc_ref[i], S5_STATE, S5_GROUP, lt, wr_ref.shape[2])

    y = _dot(_s5_load_chunks(u_ref, kt), wm_ref[...])
    for i, h_ref in enumerate((hfr, hfi, hbr, hbi)):
        y += _dot(h_ref[...].astype(BF16), wr_ref[i])
    w = y_ref.shape[1]
    for t in range(S5_CHUNK):
        y_ref[pl.ds(t, kt, stride=S5_CHUNK), :] = y[:, t * w:(t + 1) * w]


def _s5_matrices(lam_re, lam_im, log_step, b_re, b_im, c_re, c_im):
    hi = lax.Precision.HIGHEST
    L = S5_CHUNK
    step = jnp.exp(log_step)[..., None]
    n = jnp.arange(L + 1, dtype=F32)[:, None, None, None]
    mag = jnp.exp(n * (lam_re * step)[None])
    ang = n * (lam_im * step)[None]
    p_re, p_im = mag * jnp.cos(ang), mag * jnp.sin(ang)
    ab_re, ab_im = p_re[1], p_im[1]
    den = lam_re * lam_re + lam_im * lam_im
    coef_re = ((ab_re - 1.0) * lam_re + ab_im * lam_im) / den
    coef_im = (ab_im * lam_re - (ab_re - 1.0) * lam_im) / den
    e_re = p_re[:L] * coef_re - p_im[:L] * coef_im
    e_im = p_re[:L] * coef_im + p_im[:L] * coef_re
    ce_re = c_re[None] * e_re[..., None, :] - c_im[None] * e_im[..., None, :]
    ce_im = c_re[None] * e_im[..., None, :] + c_im[None] * e_re[..., None, :]
    kern = (jnp.einsum('ndgps,dgsq->ndgpq', ce_re, b_re, precision=hi)
            - jnp.einsum('ndgps,dgsq->ndgpq', ce_im, b_im, precision=hi))
    w_re, w_im = e_re[::-1], e_im[::-1]
    q_re = (w_re[..., None] * b_re[None] - w_im[..., None] * b_im[None]).transpose(1, 2, 0, 4, 3)
    q_im = (w_re[..., None] * b_im[None] + w_im[..., None] * b_re[None]).transpose(1, 2, 0, 4, 3)
    a_re, a_im = p_re[1:], p_im[1:]
    ca_re = c_re[None] * a_re[..., None, :] - c_im[None] * a_im[..., None, :]
    ca_im = c_re[None] * a_im[..., None, :] + c_im[None] * a_re[..., None, :]
    r_re = ca_re.transpose(1, 2, 4, 0, 3)
    r_im = (-ca_im).transpose(1, 2, 4, 0, 3)
    nl = L * jnp.arange(1, 9, dtype=F32)[:, None, None, None]
    t_mag = jnp.exp(nl * (lam_re * step)[None])
    t_ang = nl * (lam_im * step)[None]
    return kern, q_re, q_im, r_re, r_im, t_mag * jnp.cos(t_ang), t_mag * jnp.sin(t_ang)


def _s5_layer(u_lat, u_ctx, params, *, bsz, seq, ctx_len):
    c = u_lat.shape[1]
    L, pp, ns = S5_CHUNK, S5_GROUP, S5_STATE
    g = c // pp
    kern, q_re, q_im, r_re, r_im, t_re, t_im = _s5_matrices(*params)
    gl = S5_LANE_TILE // pp
    nb = g // gl
    lt = S5_LANE_TILE
    kw = L * lt
    sw = gl * ns
    lanes = g * ns
    kt_ = kern.transpose(0, 1, 2, 4, 3)
    lags = jnp.concatenate([kt_[:0:-1, 1], (kt_[0, 0] + kt_[0, 1])[None], kt_[1:, 0]], axis=0)
    eye = jnp.eye(gl, dtype=F32)
    bd = (lags.reshape(2 * L - 1, nb, gl, pp, pp).transpose(1, 0, 2, 3, 4)[:, :, :, :, None, :]
          * eye[None, None, :, None, :, None]).reshape(nb, 2 * L - 1, lt, lt).astype(BF16)
    qs = jnp.stack([q_re[0], q_im[0], q_re[1][:, ::-1], q_im[1][:, ::-1]])
    qc = qs.reshape(4, nb, gl, L, pp, ns).transpose(1, 3, 2, 4, 0, 5).reshape(nb, L, lt, 4 * ns).astype(BF16)
    rs = jnp.stack([r_re[0], r_im[0], r_re[1][:, :, ::-1], r_im[1][:, :, ::-1]])
    rc = rs.reshape(4, nb, gl, ns, L, pp).transpose(1, 0, 2, 3, 4, 5).reshape(nb, 4, sw, L * pp).astype(BF16)
    pf =jnp.stack([t_re[:, 0], t_im[:, 0]]).reshape(2, 8, lanes)
    pb = jnp.stack([t_re[::-1, 1], t_im[::-1, 1]]).reshape(2, 8, lanes)

    kt = S5_ROW_TILE
    u_spec = pl.BlockSpec((kt * L, S5_LANE_TILE), lambda b, i: (i, b))
    h_spec = pl.BlockSpec((kt, sw), lambda b, i: (i, b))
    cp2 = _cparams(("parallel", "arbitrary"))

    def state(u):
        rows = u.shape[0] // L
        return pl.pallas_call(
            _s5_state_kernel,
            out_shape=[jax.ShapeDtypeStruct((rows, lanes), F32)] * 4,
            grid=(nb, rows // kt),
            in_specs=[u_spec, pl.BlockSpec((None, L, lt, 4 * ns), lambda b, i: (b, 0, 0, 0))],
            out_specs=[h_spec] * 4,
            scratch_shapes=[pltpu.VMEM((kw, 4 * sw), BF16)],
            compiler_params=cp2,
            name="s5_state",
        )(u, qc)

    z_c, z_l = state(u_ctx), state(u_lat)
    lb = S5_LANES
    rows_c, rows_l = z_c[0].shape[0], z_l[0].shape[0]
    h = pl.pallas_call(
        functools.partial(_s5_rec_kernel, bsz=bsz, nc_tiles=rows_c // bsz // 8, nl_tiles=rows_l // bsz // 8),
        out_shape=[jax.ShapeDtypeStruct((rows_c, lanes), F32)] * 4 + [jax.ShapeDtypeStruct((rows_l, lanes), F32)] * 4,
        grid=(lanes // lb,),
        in_specs=[pl.BlockSpec((rows_c, lb), lambda j: (0, j))] * 4
        + [pl.BlockSpec((rows_l, lb), lambda j: (0, j))] * 4
        + [pl.BlockSpec((2, 8, lb), lambda j: (0, 0, j))] * 2,
        out_specs=[pl.BlockSpec((rows_c, lb), lambda j: (0, j))] * 4
        + [pl.BlockSpec((rows_l, lb), lambda j: (0, j))] * 4,
        compiler_params=_cparams(("parallel",)),
        name="s5_recurrence",
    )(*z_c, *z_l, pf, pb)

    def readout(u, hs):
        rows = u.shape[0] // L
        return pl.pallas_call(
            _s5_out_kernel,
            out_shape=jax.ShapeDtypeStruct(u.shape, F32),
            grid=(nb, rows // kt),
            in_specs=[u_spec] + [h_spec] * 4
            + [pl.BlockSpec((None, 2 * L - 1, lt, lt), lambda b, i: (b, 0, 0, 0)),
               pl.BlockSpec((None, 4, sw, L * pp), lambda b, i: (b, 0, 0, 0))],
            out_specs=u_spec,
            scratch_shapes=[pltpu.VMEM((kw, kw), BF16), pltpu.VMEM((4, sw, kw), BF16)],
            compiler_params=cp2,
            name="s5_out",
        )(u, *hs, bd, rc)

    return readout(u_lat, h[4:]), readout(u_ctx, h[:4])


def _head_norm(o, g_ref, center, dh):
    outs = []
    for h in range(HEADS):
        oh = o[:, h * dh:(h + 1) * dh]
        if center:
            oh = oh - jnp.mean(oh, axis=-1, keepdims=True)
        outs.append(oh * lax.rsqrt(jnp.mean(oh * oh, axis=-1, keepdims=True) + EPS))
    return jnp.concatenate(outs, axis=-1) * g_ref[...]


def _finish(x_ref, a, b, wout_ref, mod_ref, gpost_ref, o_ref, n_r):
    half = a.shape[1]
    b = b.astype(BF16)
    if n_r:
        b = _dot(_perm_matrix(b.shape[0], n_r, False), b).astype(BF16)
    y = _dot(a.astype(BF16), wout_ref[0:half, :]) + _dot(b, wout_ref[half:2 * half, :])
    out = _tile2d(x_ref) + mod_ref[5:6, :] * _rms(y) * gpost_ref[1:2, :]
    o_ref[...] = out.reshape(o_ref.shape)


def _even_merge_kernel(x_ref, ga_ref, oaf_ref, oab_ref, gb_ref, obf_ref, obb_ref, na_ref, nb_ref,
                       wout_ref, mod_ref, gpost_ref, o_ref, *, dh, n_r):
    a = _silu(_tile2d(ga_ref)) * _head_norm(_tile2d(oaf_ref) + _tile2d(oab_ref), na_ref, False, dh)
    b = _sigmoid(_tile2d(gb_ref)) * _head_norm(_tile2d(obf_ref) + _tile2d(obb_ref), nb_ref, True, dh)
    _finish(x_ref, a, b, wout_ref, mod_ref, gpost_ref, o_ref, n_r)


def _odd_merge_kernel(x_ref, u_ref, ys_ref, gb_ref, obf_ref, obb_ref, sd_ref, wglu_ref, bglu_ref, nb_ref,
                      wout_ref, mod_ref, gpost_ref, o_ref, *, dh, n_r):
    y = _tile2d(ys_ref) + sd_ref[...] * _tile2d(u_ref)
    g = jax.nn.gelu(y)
    a = g * _sigmoid(_dot(g.astype(BF16), wglu_ref[...]) + bglu_ref[...])
    b = _silu(_tile2d(gb_ref)) * _head_norm(_tile2d(obf_ref) + _tile2d(obb_ref), nb_ref, True, dh)
    _finish(x_ref, a, b, wout_ref, mod_ref, gpost_ref, o_ref, n_r)


def _merge(kernel, s, rm_lat, cm_lat, rm_ctx, cm_ctx, consts, w_out, mod, gpost, *, bsz, seq, ctx_len, name):
    d = s.shape[1]
    half = d // 2
    rows = seq // GRID_W
    rt = min(MERGE_ROWS, rows)
    n_rh = rows // rt
    ncb = GRID_W // COL_TILE
    per_b = n_rh * ncb
    const_specs = [pl.BlockSpec(a.shape, lambda i, nd=a.ndim: (0,) * nd) for a in consts] \
        + [pl.BlockSpec((d, d), lambda i: (0, 0))]
    gpost_spec = pl.BlockSpec((3, d), lambda i: (0, 0))
    cp = _cparams(("parallel",))

    rm_idx = lambda i: ((i // per_b) * n_rh + (i % per_b) // ncb, i % ncb)
    cm_idx = lambda i: ((i // per_b) * ncb + i % ncb, (i % per_b) // ncb)
    s3 = s.reshape(s.shape[0] // GRID_W, GRID_W, d)
    ins = [s3] + [a.reshape(bsz * rows, GRID_W, a.shape[1]) for a, _ in rm_lat] \
        + [a.reshape(bsz * GRID_W, rows, a.shape[1]) for a, _ in cm_lat]
    in_specs = [pl.BlockSpec((rt, COL_TILE, d), lambda i: rm_idx(i) + (0,))] \
        + [pl.BlockSpec((rt, COL_TILE, half), lambda i, cb=cb: rm_idx(i) + (cb,)) for _, cb in rm_lat] \
        + [pl.BlockSpec((COL_TILE, rt, half), lambda i, cb=cb: cm_idx(i) + (cb,)) for _, cb in cm_lat]
    n_out_groups = s3.shape[0] if rm_ctx is not None else bsz * rows
    out = pl.pallas_call(
        functools.partial(kernel, n_r=rt),
        out_shape=jax.ShapeDtypeStruct((n_out_groups, GRID_W, d), F32),
        grid=(bsz * per_b,),
        in_specs=in_specs + const_specs
        + [pl.BlockSpec((None, N_MOD, d), lambda i: (i // per_b, 0, 0)), gpost_spec],
        out_specs=pl.BlockSpec((rt, COL_TILE, d), lambda i: rm_idx(i) + (0,)),
        compiler_params=cp,
        name=name + "_lat",
    )(*ins, *consts, w_out, mod, gpost)
    out = out.reshape(n_out_groups * GRID_W, d)
    if rm_ctx is None:
        return out

    tm = rt * COL_TILE
    t0 = bsz * seq // tm
    tiles = list(rm_ctx) + list(cm_ctx)
    n_in = 1 + len(tiles) + len(consts) + 3
    inner = functools.partial(kernel, n_r=0)
    return pl.pallas_call(
        lambda *refs: inner(*refs[:n_in], refs[n_in + 1]),
        out_shape=jax.ShapeDtypeStruct(out.shape, F32),
        grid=(bsz * ctx_len // tm,),
        in_specs=[pl.BlockSpec((tm, d), lambda i: (t0 + i, 0))]
        + [pl.BlockSpec((tm, half), lambda i, cb=cb: (i, cb)) for _, cb in tiles] + const_specs
        + [pl.BlockSpec((None, N_MOD, d), lambda i: (bsz, 0, 0)), gpost_spec,
           pl.BlockSpec(memory_space=pl.ANY)],
        out_specs=pl.BlockSpec((tm, d), lambda i: (t0 + i, 0)),
        input_output_aliases={n_in: 0},
        compiler_params=cp,
        name=name + "_ctx",
    )(s, *[a for a, _ in tiles], *consts, w_out, mod, gpost, out)


def kernel(x, c, ctx, c_ctx, w_mod, b_mod, norm_pre, norm_post, ffn_w_gate, ffn_w_up, ffn_w_down, ev_w_in, ev_w_out, gla_w_gate, gla_b_gate, gla_norm, ml_conv_w, ml_conv_b, ml_b_gates, ml_norm, od_w_in, od_w_out, s5_lam_re, s5_lam_im, s5_log_step, s5_b_re, s5_b_im, s5_c_re, s5_c_im, s5_d, s5_w_glu, s5_b_glu, ret_log_decay, ret_norm):
    bsz, seq, d = x.shape
    ctx_len = ctx.shape[1]
    depth = w_mod.shape[0]
    half = d // 2
    dh = half // HEADS
    gla_dk = dh // 2
    d_ff = ffn_w_gate.shape[-1]
    tm = TOKEN_TILE
    rows = seq // GRID_W
    assert seq % tm == 0 and (bsz * ctx_len) % tm == 0 and seq == rows * GRID_W and ctx_len % CHUNK == 0
    assert 2 * bsz == 8 and rows % 8 == 0 and seq % CHUNK == 0 and half == PROJ_TILE
    s5_tile = S5_CHUNK * S5_ROW_TILE
    assert ctx_len % (8 * S5_CHUNK) == 0 and (bsz * ctx_len) % s5_tile == 0 and (bsz * seq) % s5_tile == 0

    tiles_x = bsz * seq // tm
    tiles_all = tiles_x + bsz * ctx_len // tm
    tiles_per_batch = seq // tm
    mod_of_tile = lambda i: jnp.minimum(i // tiles_per_batch, bsz)

    wg, wu, wd = ffn_w_gate.astype(BF16), ffn_w_up.astype(BF16), ffn_w_down.astype(BF16)

    cc = jnp.concatenate([c, c_ctx[None], jnp.zeros((8 - bsz - 1, d), F32)], axis=0)
    mod_all = _modulation(cc, w_mod, b_mod).reshape(depth, 8, N_MOD, d)

    s = None
    dims = dict(bsz=bsz, seq=seq, ctx_len=ctx_len)

    for layer in range(depth):
        need_ctx = layer < depth - 1
        mod = mod_all[layer]
        gpre, gpost = norm_pre[layer], norm_post[layer]
        ffn = functools.partial(_ffn, mod=mod, gpre=gpre, gpost=gpost, wg=wg, wu=wu, wd=wd, layer=layer,
                                mod_of_tile=mod_of_tile)
        if s is None:
            s = ffn(x.reshape(bsz * seq, d), which=0, sub=0, n_tiles=tiles_x, out_tiles=tiles_all)
            s = ffn(ctx.reshape(bsz * ctx_len, d), which=0, sub=0, n_tiles=tiles_all - tiles_x, out_tile0=tiles_x,
                    out_tiles=tiles_all, prev=s)
        else:
            s = ffn(s, which=0, sub=0, n_tiles=tiles_all)

        if layer % 2 == 0:
            e = layer // 2
            w = ev_w_in[e]
            qk_w = 2 * HEADS * gla_dk
            n_gate = 4 * HEADS
            c_low = qk_w + 2 * half
            c_ml = c_low + 2 * GLA_RANK
            c_g = c_ml + 4 * half
            w_in = jnp.concatenate([w[:, :c_low], w[:, c_ml:c_g]], axis=1).astype(BF16)
            w_small = jnp.concatenate([w[:, c_low:c_ml], jnp.zeros((d, SMALL_W - 2 * GLA_RANK), F32),
                                       w[:, c_g:], jnp.zeros((d, SMALL_W - n_gate), F32)], axis=1).astype(BF16)
            (pa, pb, psa, psb), (pa_c, pb_c, psa_c, psb_c) = _proj(s, mod, gpre, w_in, w_small,
                                                                   n_a=c_low // PROJ_TILE, **dims)

            st0 = [jnp.zeros((bsz, 2, HEADS, dh, gla_dk), F32)]
            gla_ins = lambda p, ps: [(p, qk_w // 2, 0), (p, qk_w // 2, qk_w // 2), (p, half, qk_w), (ps, SMALL_W, 0)]
            oa, oa_c = _bidir_scan(
                functools.partial(_gla_kernel, L=CHUNK, dk=gla_dk, dv=dh, gated=True,
                                  q_scale=float(gla_dk) ** -0.5, k_scale=1.0),
                gla_ins(pa, psa), gla_ins(pa_c, psa_c), [gla_w_gate[e].astype(BF16), gla_b_gate[e]], st0,
                out_width=half, name="gla", **dims)

            scale = jnp.concatenate([jnp.ones((half,), F32), jnp.full((half,), float(dh) ** -0.5, F32)])
            qk = _conv(pb, ml_conv_w[e], ml_conv_b[e], scale, bsz=bsz, n=seq)
            qk_c = _conv(pb_c, ml_conv_w[e], ml_conv_b[e], scale, bsz=bsz, n=ctx_len)
            bias = jnp.zeros((1, SMALL_W), F32).at[0, :n_gate].set(ml_b_gates[e].reshape(-1))
            ml_ins = lambda qk_, p, ps: [(qk_, half, 0), (qk_, half, half), (p, half, 2 * half), (ps, SMALL_W, 0)]
            st0 = [jnp.zeros((bsz, 2, HEADS, dh, dh), F32), jnp.zeros((bsz, 2, HEADS, 8, dh), F32)]
            ob, ob_c = _bidir_scan(
                functools.partial(_mlstm_kernel, L=CHUNK, dh=dh),
                ml_ins(qk, pb, psb), ml_ins(qk_c, pb_c, psb_c), [bias], st0,
                out_width=half, name="mlstm", **dims)

            ga_cb, gb_cb = (qk_w + half) // half, 3
            s = _merge(
                functools.partial(_even_merge_kernel, dh=dh), s,
                [(pa, ga_cb), (oa[0], 0), (oa[1], 0)], [(pb, gb_cb), (ob[0], 0), (ob[1], 0)],
                [(pa_c, ga_cb), (oa_c[0], 0), (oa_c[1], 0)] if need_ctx else None,
                [(pb_c, gb_cb), (ob_c[0], 0), (ob_c[1], 0)],
                [gla_norm[e].reshape(1, half), ml_norm[e].reshape(1, half)],
                ev_w_out[e].astype(BF16), mod, gpost, name="even_merge", **dims)
        else:
            o = layer // 2
            (pa, pb), (pa_c, pb_c) = _proj(s, mod, gpre, od_w_in[o].astype(BF16), None, n_a=1, **dims)

            ys, ys_c = _s5_layer(pa, pa_c, (s5_lam_re[o], s5_lam_im[o], s5_log_step[o], s5_b_re[o], s5_b_im[o],
                                            s5_c_re[o], s5_c_im[o]), **dims)

            lg = jnp.repeat(-jnp.exp(ret_log_decay[o]), dh, axis=1)
            ret_ins = lambda p: [(p, half, 0), (p, half, half), (p, half, 2 * half)]
            st0 = [jnp.zeros((bsz, 2, HEADS, dh, dh), F32)]
            od, od_c = _bidir_scan(
                functools.partial(_gla_kernel, L=CHUNK, dk=dh, dv=dh, gated=False,
                                  q_scale=1.0, k_scale=float(dh) ** -0.5),
                ret_ins(pb), ret_ins(pb_c), [lg], st0, out_width=half, name="retention", **dims)

            s = _merge(
                functools.partial(_odd_merge_kernel, dh=dh), s,
                [(pa, 0), (ys, 0)], [(pb, 3), (od[0], 0), (od[1], 0)],
                [(pa_c, 0), (ys_c, 0)] if need_ctx else None, [(pb_c, 3), (od_c[0], 0), (od_c[1], 0)],
                [s5_d[o].reshape(1, half), s5_w_glu[o].astype(BF16), s5_b_glu[o].reshape(1, half),
                 ret_norm[o].reshape(1, half)],
                od_w_out[o].astype(BF16), mod, gpost, name="odd_merge", **dims)

        s = ffn(s, which=1, sub=2, n_tiles=tiles_all if need_ctx else tiles_x)

    return s[:bsz * seq].reshape(bsz, seq, d)
```

```python
import functools

import jax
import jax.numpy as jnp
from jax import lax
from jax.experimental import pallas as pl
from jax.experimental.pallas import tpu as pltpu

F32 = jnp.float32
BF16 = jnp.bfloat16

EPS = 1e-6
GRID_W = 64
COL_TILE = 8
CHUNK = 64
FFN_RES = 0.5
N_MOD = 9
HEADS = 4
GLA_RANK = 16
GLA_GATE_NORM = 16.0
S5_GROUP = 16
S5_STATE = 64
S5_CHUNK = 8
S5_LANE_TILE = 128
S5_ROW_TILE = 512
SMALL_W = 128

TOKEN_TILE = 512
MERGE_ROWS = 32
FF_TILE = 1024
PROJ_TILE = 1024
MOD_TILE = 1024
CONV_TILE = 256
CONV_ROWS = 256
S5_LANES = 256
VMEM_LIMIT = 56 * 1024 * 1024


def _cparams(sem):
    return pltpu.CompilerParams(dimension_semantics=sem, vmem_limit_bytes=VMEM_LIMIT)


def _rms(x):
    return x * lax.rsqrt(jnp.mean(x * x, axis=-1, keepdims=True) + EPS)


def _sigmoid(x):
    return 1.0 / (1.0 + jnp.exp(-x))


def _silu(x):
    return x * _sigmoid(x)


def _log_sigmoid(x):
    return jnp.minimum(x, 0.0) - jnp.log(1.0 + jnp.exp(-jnp.abs(x)))


def _dot(a, b):
    return jnp.dot(a, b, preferred_element_type=F32)


def _dot_nt(a, b):
    return lax.dot_general(a, b, (((1,), (1,)), ((), ())), preferred_element_type=F32)


def _dot_tn(a, b):
    return lax.dot_general(a, b, (((0,), (0,)), ((), ())), preferred_element_type=F32)


def _dot01(t01, x):
    x0 = x.astype(BF16)
    r1 = x - x0.astype(F32)
    x1 = r1.astype(BF16)
    x2 = (r1 - x1.astype(F32)).astype(BF16)
    return _dot(t01, x0) + _dot(t01, x1) + _dot(t01, x2)


def _perm_matrix(n, n_r, to_col_major):
    i = lax.broadcasted_iota(jnp.int32, (n, n), 0)
    c = lax.broadcasted_iota(jnp.int32, (n, n), 1)
    if to_col_major:
        hit = (i // n_r == c % COL_TILE) & (i % n_r == c // COL_TILE)
    else:
        hit = (i // COL_TILE == c % n_r) & (i % COL_TILE == c // n_r)
    return jnp.where(hit, 1.0, 0.0).astype(BF16)


def _tile2d(ref):
    v = ref[...]
    return v.reshape(-1, v.shape[-1])


def _mod_kernel(c_ref, w_ref, b_ref, o_ref):
    a = _silu(c_ref[...]).astype(BF16)
    o_ref[...] = _dot(a, w_ref[...].astype(BF16)) + b_ref[...]


def _modulation(cc, w_mod, b_mod):
    depth, d, nm = w_mod.shape
    return pl.pallas_call(
        _mod_kernel,
        out_shape=jax.ShapeDtypeStruct((depth, 8, nm), F32),
        grid=(depth, nm // MOD_TILE),
        in_specs=[pl.BlockSpec((8, d), lambda l, j: (0, 0)),
                  pl.BlockSpec((None, d, MOD_TILE), lambda l, j: (l, 0, j)),
                  pl.BlockSpec((None, 1, MOD_TILE), lambda l, j: (l, 0, j))],
        out_specs=pl.BlockSpec((None, 8, MOD_TILE), lambda l, j: (l, 0, j)),
        compiler_params=_cparams(("parallel", "parallel")),
        name="modulation",
    )(cc, w_mod, b_mod.reshape(depth, 1, nm))


def _ffn_kernel(x_ref, mod_ref, gpre_ref, gpost_ref, wg_ref, wu_ref, wd_ref, o_ref, h_ref, acc_ref, *, sub, nff,
                tail):
    j = pl.program_id(1)

    @pl.when(j == 0)
    def _():
        shift = mod_ref[3 * sub:3 * sub + 1, :]
        scale = mod_ref[3 * sub + 1:3 * sub + 2, :]
        h = _rms(x_ref[...]) * gpre_ref[sub:sub + 1, :] * (1.0 + scale) + shift
        h_ref[...] = h.astype(BF16)

    def hidden_step(valid, first):
        h = h_ref[...]
        t = _silu(_dot(h, wg_ref[:, :valid])) * _dot(h, wu_ref[:, :valid])
        y = _dot(t.astype(BF16), wd_ref[:valid, :])
        if first:
            acc_ref[...] = y
        else:
            acc_ref[...] += y

    pl.when(j == 0)(lambda: hidden_step(FF_TILE, True))
    pl.when((j > 0) & (j < nff - 1))(lambda: hidden_step(FF_TILE, False))
    pl.when(j == nff - 1)(lambda: hidden_step(tail or FF_TILE, False))

    @pl.when(j == nff - 1)
    def _():
        gate = mod_ref[3 * sub + 2:3 * sub + 3, :]
        o_ref[...] = x_ref[...] + FFN_RES * gate * _rms(acc_ref[...]) * gpost_ref[sub:sub + 1, :]


def _ffn(s, mod, gpre, gpost, wg, wu, wd, *, layer, which, sub, n_tiles, mod_of_tile, out_tile0=0, out_tiles=None,
         prev=None):
    d = s.shape[1]
    d_ff = wg.shape[-1]
    nff = pl.cdiv(d_ff, FF_TILE)
    assert nff >= 2 and (d_ff % FF_TILE) % 128 == 0
    tm = TOKEN_TILE
    out_tiles = n_tiles if out_tiles is None else out_tiles
    kernel = functools.partial(_ffn_kernel, sub=sub, nff=nff, tail=d_ff % FF_TILE)
    ins = [s, mod, gpre, gpost, wg, wu, wd]
    in_specs = [pl.BlockSpec((tm, d), lambda i, j: (i, 0)),
                pl.BlockSpec((None, N_MOD, d), lambda i, j: (mod_of_tile(out_tile0 + i), 0, 0)),
                pl.BlockSpec((3, d), lambda i, j: (0, 0)),
                pl.BlockSpec((3, d), lambda i, j: (0, 0)),
                pl.BlockSpec((None, None, d, FF_TILE), lambda i, j: (layer, which, 0, j)),
                pl.BlockSpec((None, None, d, FF_TILE), lambda i, j: (layer, which, 0, j)),
                pl.BlockSpec((None, None, FF_TILE, d), lambda i, j: (layer, which, j, 0))]
    aliases = {}
    if prev is not None:
        n_in = len(ins)
        inner = kernel
        kernel = lambda *refs: inner(*refs[:n_in], *refs[n_in + 1:])
        ins.append(prev)
        in_specs.append(pl.BlockSpec(memory_space=pl.ANY))
        aliases = {n_in: 0}
    return pl.pallas_call(
        kernel,
        out_shape=jax.ShapeDtypeStruct((out_tiles * tm, d), F32),
        grid=(n_tiles, nff),
        in_specs=in_specs,
        out_specs=pl.BlockSpec((tm, d), lambda i, j: (out_tile0 + i, 0)),
        scratch_shapes=[pltpu.VMEM((tm, d), BF16), pltpu.VMEM((tm, d), F32)],
        input_output_aliases=aliases,
        compiler_params=_cparams(("parallel", "arbitrary")),
        name="ffn",
    )(*ins)


def _proj_kernel(*refs, n_a, n_r, has_small):
    if has_small:
        x_ref, mod_ref, gpre_ref, w_ref, ws_ref, pa_ref, pb_ref, psa_ref, psb_ref, h_ref, hp_ref = refs
    else:
        x_ref, mod_ref, gpre_ref, w_ref, pa_ref, pb_ref, h_ref, hp_ref = refs
    j = pl.program_id(1)
    tm = h_ref.shape[0]

    @pl.when(j == 0)
    def _():
        x = _tile2d(x_ref)
        h = (_rms(x) * gpre_ref[1:2, :] * (1.0 + mod_ref[4:5, :]) + mod_ref[3:4, :]).astype(BF16)
        hp = _dot(_perm_matrix(tm, n_r, True), h).astype(BF16) if n_r else h
        h_ref[...] = h
        hp_ref[...] = hp
        if has_small:
            psa_ref[...] = _dot(h, ws_ref[:, :SMALL_W]).reshape(psa_ref.shape)
            psb_ref[...] = _dot(hp, ws_ref[:, SMALL_W:])

    @pl.when(j < n_a)
    def _():
        pa_ref[...] = _dot(h_ref[...], w_ref[j]).reshape(pa_ref.shape)

    @pl.when(j >= n_a)
    def _():
        pb_ref[...] = _dot(hp_ref[...], w_ref[j])


def _proj(s, mod, gpre, w, w_small, *, n_a, bsz, seq, ctx_len):
    d = s.shape[1]
    rows = seq // GRID_W
    tn = PROJ_TILE
    n_tiles_n = w.shape[1] // tn
    n_b = n_tiles_n - n_a
    has_small = w_small is not None
    w_ins = [w.reshape(d, n_tiles_n, tn).transpose(1, 0, 2)] + ([w_small] if has_small else [])
    w_specs = [pl.BlockSpec((n_tiles_n, d, tn), lambda i, j: (0, 0, 0), pipeline_mode=pl.Buffered(1))]
    if has_small:
        w_specs.append(pl.BlockSpec((d, 2 * SMALL_W), lambda i, j: (0, 0)))
    ja = lambda j: jnp.minimum(j, n_a - 1)
    jb = lambda j: jnp.maximum(j - n_a, 0)
    cp = _cparams(("parallel", "arbitrary"))

    tm = rows * COL_TILE
    ncb = GRID_W // COL_TILE
    s3 = s.reshape(s.shape[0] // GRID_W, GRID_W, d)
    out_shape = [jax.ShapeDtypeStruct((bsz * rows, GRID_W, n_a * tn), F32),
                 jax.ShapeDtypeStruct((bsz * seq, n_b * tn), F32)]
    out_specs = [pl.BlockSpec((rows, COL_TILE, tn), lambda i, j: (i // ncb, i % ncb, ja(j))),
                 pl.BlockSpec((tm, tn), lambda i, j: (i, jb(j)))]
    if has_small:
        out_shape += [jax.ShapeDtypeStruct((bsz * rows, GRID_W, SMALL_W), F32),
                      jax.ShapeDtypeStruct((bsz * seq, SMALL_W), F32)]
        out_specs += [pl.BlockSpec((rows, COL_TILE, SMALL_W), lambda i, j: (i // ncb, i % ncb, 0)),
                      pl.BlockSpec((tm, SMALL_W), lambda i, j: (i, 0))]
    lat = pl.pallas_call(
        functools.partial(_proj_kernel, n_a=n_a, n_r=rows, has_small=has_small),
        out_shape=out_shape,
        grid=(bsz * ncb, n_tiles_n),
        in_specs=[pl.BlockSpec((rows, COL_TILE, d), lambda i, j: (i // ncb, i % ncb, 0)),
                  pl.BlockSpec((None, N_MOD, d), lambda i, j: (i // ncb, 0, 0)),
                  pl.BlockSpec((3, d), lambda i, j: (0, 0))] + w_specs,
        out_specs=out_specs,
        scratch_shapes=[pltpu.VMEM((tm, d), BF16), pltpu.VMEM((tm, d), BF16)],
        compiler_params=cp,
        name="in_proj_lat",
    )(s3, mod, gpre, *w_ins)
    lat = [lat[0].reshape(bsz * seq, n_a * tn), lat[1]] \
        + ([lat[2].reshape(bsz * seq, SMALL_W), lat[3]] if has_small else [])

    tc = TOKEN_TILE
    n_c = bsz * ctx_len
    t0 = bsz * seq // tc
    out_shape = [jax.ShapeDtypeStruct((n_c, n_a * tn), F32), jax.ShapeDtypeStruct((n_c, n_b * tn), F32)]
    out_specs = [pl.BlockSpec((tc, tn), lambda i, j: (i, ja(j))), pl.BlockSpec((tc, tn), lambda i, j: (i, jb(j)))]
    if has_small:
        out_shape += [jax.ShapeDtypeStruct((n_c, SMALL_W), F32)] * 2
        out_specs += [pl.BlockSpec((tc, SMALL_W), lambda i, j: (i, 0))] * 2
    ctx = pl.pallas_call(
        functools.partial(_proj_kernel, n_a=n_a, n_r=0, has_small=has_small),
        out_shape=out_shape,
        grid=(n_c // tc, n_tiles_n),
        in_specs=[pl.BlockSpec((tc, d), lambda i, j: (t0 + i, 0)),
                  pl.BlockSpec((None, N_MOD, d), lambda i, j: (bsz, 0, 0)),
                  pl.BlockSpec((3, d), lambda i, j: (0, 0))] + w_specs,
        out_specs=out_specs,
        scratch_shapes=[pltpu.VMEM((tc, d), BF16), pltpu.VMEM((tc, d), BF16)],
        compiler_params=cp,
        name="in_proj_ctx",
    )(s, mod, gpre, *w_ins)
    return lat, list(ctx)


def _conv_kernel(x_ref, w_ref, b_ref, sc_ref, o_ref):
    n, ct = x_ref.shape
    ch = CONV_ROWS
    n_ch = n // ch
    r = lax.broadcasted_iota(jnp.int32, (ch, ct), 0)

    def body(c, carry):
        r0 = pl.multiple_of(c * ch, ch)
        x = x_ref[pl.ds(r0, ch), :]
        before = x_ref[pl.ds(pl.multiple_of(jnp.maximum(r0 - 8, 0), 8), 8), :][7:8, :]
        after = x_ref[pl.ds(pl.multiple_of(jnp.minimum(r0 + ch, n - 8), 8), 8), :][0:1, :]
        before = jnp.where(c == 0, 0.0, before)
        after = jnp.where(c == n_ch - 1, 0.0, after)
        prev = jnp.where(r == 0, before, pltpu.roll(x, 1, 0))
        nxt = jnp.where(r == ch - 1, after, pltpu.roll(x, ch - 1, 0))
        y = w_ref[1:2, :] * x + w_ref[0:1, :] * prev + w_ref[2:3, :] * nxt + b_ref[...]
        o_ref[pl.ds(r0, ch), :] = _silu(y) * sc_ref[...]
        return carry

    lax.fori_loop(0, n_ch, body, 0)


def _conv(p, conv_w, conv_b, scale, *, bsz, n):
    c = conv_w.shape[1]
    ct = CONV_TILE
    return pl.pallas_call(
        _conv_kernel,
        out_shape=jax.ShapeDtypeStruct((bsz * n, c), F32),
        grid=(bsz, c // ct),
        in_specs=[pl.BlockSpec((n, ct), lambda b, j: (b, j)),
                  pl.BlockSpec((3, ct), lambda b, j: (0, j)),
                  pl.BlockSpec((1, ct), lambda b, j: (0, j)),
                  pl.BlockSpec((1, ct), lambda b, j: (0, j))],
        out_specs=pl.BlockSpec((n, ct), lambda b, j: (b, j)),
        compiler_params=_cparams(("parallel", "parallel")),
        name="conv",
    )(p, conv_w, conv_b.reshape(1, c), scale.reshape(1, c))


def _gla_kernel(*refs, L, dk, dv, gated, q_scale, k_scale):
    if gated:
        qf, kf, vf, df, qb, kb, vb, db, wgate_ref, bgate_ref, st_in, of, ob, st_out = refs
    else:
        qf, kf, vf, qb, kb, vb, lg_ref, st_in, of, ob, st_out = refs
        df = db = None

    @pl.when(pl.program_id(1) == 0)
    def _():
        st_out[...] = st_in[...]

    row = lax.broadcasted_iota(jnp.int32, (L, L), 0)
    col = lax.broadcasted_iota(jnp.int32, (L, L), 1)
    trow = lax.broadcasted_iota(jnp.int32, (L, 1), 0)
    for d, (q_ref, k_ref, v_ref, d_ref, o_ref) in enumerate(((qf, kf, vf, df, of), (qb, kb, vb, db, ob))):
        mask = (row >= col) if d == 0 else (row <= col)
        mid = L // 2 if d == 0 else L - 1 - L // 2
        end = L - 1 if d == 0 else 0
        if gated:
            low = d_ref[:, d * GLA_RANK:(d + 1) * GLA_RANK].astype(BF16)
            z = _dot(low, wgate_ref[d]) + bgate_ref[d:d + 1, :]
            la = _log_sigmoid(z) * (1.0 / GLA_GATE_NORM)
            bcum = _dot01(mask.astype(BF16), la)
        else:
            cnt = (trow + 1) if d == 0 else (L - trow)
            bcum = cnt.astype(F32) * lg_ref[d:d + 1, :]
        for h in range(HEADS):
            b = bcum[:, h * dk:(h + 1) * dk]
            b_mid = b[mid:mid + 1, :]
            b_end = b[end:end + 1, :]
            q = q_ref[:, h * dk:(h + 1) * dk] * q_scale
            k = k_ref[:, h * dk:(h + 1) * dk] * k_scale
            v = v_ref[:, h * dv:(h + 1) * dv].astype(BF16)
            qs = (q * jnp.exp(b - b_mid)).astype(BF16)
            ks = (k * jnp.exp(b_mid - b)).astype(BF16)
            scores = jnp.where(mask, _dot_nt(qs, ks), 0.0)
            st = st_out[d, h]
            o = _dot(scores.astype(BF16), v) + _dot_nt((q * jnp.exp(b)).astype(BF16), st.astype(BF16))
            o_ref[:, h * dv:(h + 1) * dv] = o
            kd = (k * jnp.exp(b_end - b)).astype(BF16)
            st_out[d, h] = jnp.exp(b_end) * st + _dot_tn(v, kd)


def _mlstm_kernel(qf, kf, vf, gf, qb, kb, vb, gb, bias_ref, c_in, nm_in, of, ob, c_out, nm_out, *, L, dh):
    @pl.when(pl.program_id(1) == 0)
    def _():
        c_out[...] = c_in[...]
        nm_out[...] = nm_in[...]

    row = lax.broadcasted_iota(jnp.int32, (L, L), 0)
    col = lax.broadcasted_iota(jnp.int32, (L, L), 1)
    eye = row == col
    ones = jnp.ones((L, L), BF16)
    for d, (q_ref, k_ref, v_ref, g_ref, o_ref) in enumerate(((qf, kf, vf, gf, of), (qb, kb, vb, gb, ob))):
        mask = (row >= col) if d == 0 else (row <= col)
        cum_us = (row <= col) if d == 0 else (row >= col)
        end = L - 1 if d == 0 else 0
        g = g_ref[...] + bias_ref[...]
        fl = _log_sigmoid(g)
        bc_all = _dot01(mask.astype(BF16), fl)
        for h in range(HEADS):
            ci = d * 2 * HEADS + h
            cf = ci + HEADS
            i_col = g[:, ci:ci + 1]
            f_col = fl[:, cf:cf + 1]
            b_col = bc_all[:, cf:cf + 1]
            x = jnp.where(cum_us, f_col, 0.0) - jnp.where(eye, i_col, 0.0)
            drow = _dot01(ones, x)
            m_st = nm_out[d, h, 1:2, 0:1]
            dmat = jnp.where(mask, b_col - drow, -jnp.inf)
            inter = b_col + m_st
            m_t = jnp.maximum(inter, jnp.max(dmat, axis=-1, keepdims=True))
            q = q_ref[:, h * dh:(h + 1) * dh]
            k = k_ref[:, h * dh:(h + 1) * dh]
            v = v_ref[:, h * dh:(h + 1) * dh].astype(BF16)
            qh = q.astype(BF16)
            scores = _dot_nt(qh, k.astype(BF16)) * jnp.exp(dmat - m_t)
            w_inter = jnp.exp(inter - m_t)
            c_st = c_out[d, h]
            n_st = nm_out[d, h, 0:1, :]
            num = _dot(scores.astype(BF16), v) + w_inter * _dot(qh, c_st.astype(BF16))
            den = (jnp.sum(scores, axis=-1, keepdims=True)
                   + w_inter * jnp.sum(q * n_st, axis=-1, keepdims=True))
            o_ref[:, h * dh:(h + 1) * dh] = num / jnp.maximum(jnp.abs(den), jnp.exp(-m_t))
            b_end = b_col[end:end + 1, :]
            d_end = b_end - b_col + i_col
            m_new = jnp.maximum(b_end + m_st, jnp.max(d_end, axis=0, keepdims=True))
            kw = k * jnp.exp(d_end - m_new)
            decay = jnp.exp(b_end + m_st - m_new)
            c_out[d, h] = decay * c_st + _dot_tn(kw.astype(BF16), v)
            nm_out[d, h, 0:1, :] = decay * n_st + jnp.sum(kw, axis=0, keepdims=True)
            nm_out[d, h, 1:2, :] = jnp.broadcast_to(m_new, (1, dh))


def _scan_call(kernel, bsz, n, chunk_ins, consts, states, out_width, name):
    L = CHUNK
    chunk = (lambda s: s, lambda s: n - 1 - s)
    ins, in_specs = [], []
    for d in (0, 1):
        for a, width, coloff in chunk_ins:
            ins.append(a)
            in_specs.append(pl.BlockSpec((L, width), lambda b, s, d=d, cb=coloff // width: (b * n + chunk[d](s), cb)))
    for a in consts:
        ins.append(a)
        in_specs.append(pl.BlockSpec(a.shape, lambda b, s, nd=a.ndim: (0,) * nd))
    state_specs = [pl.BlockSpec((None,) + a.shape[1:], lambda b, s, nd=a.ndim: (b,) + (0,) * (nd - 1))
                   for a in states]
    ins += list(states)
    in_specs += state_specs
    out_struct = jax.ShapeDtypeStruct((bsz * n * L, out_width), F32)
    res = pl.pallas_call(
        kernel,
        out_shape=[out_struct, out_struct] + [jax.ShapeDtypeStruct(a.shape, F32) for a in states],
        grid=(bsz, n),
        in_specs=in_specs,
        out_specs=[pl.BlockSpec((L, out_width), lambda b, s, d=d: (b * n + chunk[d](s), 0)) for d in (0, 1)]
        + state_specs,
        compiler_params=_cparams(("parallel", "arbitrary")),
        name=name,
    )(*ins)
    return (res[0], res[1]), list(res[2:])


def _bidir_scan(kernel, ins_lat, ins_ctx, consts, states0, *, bsz, seq, ctx_len, out_width, name):
    o_ctx, st = _scan_call(kernel, bsz, ctx_len // CHUNK, ins_ctx, consts, states0, out_width, name + "_ctx")
    o_lat, _ = _scan_call(kernel, bsz, seq // CHUNK, ins_lat, consts, st, out_width, name + "_lat")
    return o_lat, o_ctx


def _s5_load_chunks(u_ref, kt):
    return jnp.concatenate([u_ref[pl.ds(t, kt, stride=S5_CHUNK), :].astype(BF16) for t in range(S5_CHUNK)],
                           axis=-1)


def _expand_blockdiag(c, rows_per_group, lanes_per_group, lane_period, n_out):
    r_n, k = c.shape
    kk = lax.broadcasted_iota(jnp.int32, (k, n_out), 0)
    jj = lax.broadcasted_iota(jnp.int32, (k, n_out), 1)
    sel = (kk // lanes_per_group == jj // lane_period) & (kk % lanes_per_group == jj % lanes_per_group)
    rr = lax.broadcasted_iota(jnp.int32, (r_n, n_out), 0)
    cc = lax.broadcasted_iota(jnp.int32, (r_n, n_out), 1)
    same = rr // rows_per_group == (cc % lane_period) // lanes_per_group
    return jnp.where(same, _dot(c, jnp.where(sel, 1.0, 0.0).astype(BF16)), 0.0).astype(BF16)


def _s5_state_kernel(u_ref, qc_ref, zfr_ref, zfi_ref, zbr_ref, zbi_ref, w_ref):
    kt = zfr_ref.shape[0]
    lt = S5_LANE_TILE

    @pl.when(pl.program_id(1) == 0)
    def _():
        for s in range(S5_CHUNK):
            w_ref[s * lt:(s + 1) * lt, :] = _expand_blockdiag(qc_ref[s], S5_GROUP, S5_STATE,
                                                              lt // S5_GROUP * S5_STATE, w_ref.shape[1])

    z = _dot(_s5_load_chunks(u_ref, kt), w_ref[...])
    w = z.shape[1] // 4
    for i, ref in enumerate((zfr_ref, zfi_ref, zbr_ref, zbi_ref)):
        ref[...] = z[:, i * w:(i + 1) * w]


def _s5_rec_kernel(*refs, bsz, nc_tiles, nl_tiles):
    zc, zl = refs[0:4], refs[4:8]
    pf_ref, pb_ref = refs[8:10]
    hc, hl = refs[10:14], refs[14:18]
    pw = ((pf_ref[0], pf_ref[1]), (pb_ref[0], pb_ref[1]))
    sub = lax.broadcasted_iota(jnp.int32, pw[0][0].shape, 0)

    def cmul(a, b):
        return a[0] * b[0] - a[1] * b[1], a[0] * b[1] + a[1] * b[0]

    def cadd(a, b):
        return a[0] + b[0], a[1] + b[1]

    def scan_tile(z, carry, d):
        p = pw[d]
        h = z
        for sh in (1, 2, 4):
            prow = sh - 1 if d == 0 else 8 - sh
            keep = (sub >= sh) if d == 0 else (sub < 8 - sh)
            shift = sh if d == 0 else 8 - sh
            moved = tuple(jnp.where(keep, pltpu.roll(v, shift, 0), 0.0) for v in h)
            h = cadd(h, cmul((p[0][prow:prow + 1, :], p[1][prow:prow + 1, :]), moved))
        full = cadd(h, cmul(p, carry))
        edge, shift, last = (0, 1, 7) if d == 0 else (7, 7, 0)
        before = tuple(jnp.where(sub == edge, c, pltpu.roll(v, shift, 0)) for v, c in zip(full, carry))
        return before, tuple(v[last:last + 1, :] for v in full)

    def walk(z_refs, h_refs, n_tiles, carries):
        def body(i, carries):
            out = []
            for b in range(bsz):
                for d in (0, 1):
                    t = i if d == 0 else n_tiles - 1 - i
                    r0 = pl.multiple_of((b * n_tiles + t) * 8, 8)
                    z = (z_refs[2 * d][pl.ds(r0, 8), :], z_refs[2 * d + 1][pl.ds(r0, 8), :])
                    before, new = scan_tile(z, carries[2 * b + d], d)
                    h_refs[2 * d][pl.ds(r0, 8), :] = before[0]
                    h_refs[2 * d + 1][pl.ds(r0, 8), :] = before[1]
                    out.append(new)
            return tuple(out)
        return lax.fori_loop(0, n_tiles, body, carries)

    zero = jnp.zeros((1, sub.shape[1]), F32)
    carries = walk(zc, hc, nc_tiles, tuple((zero, zero) for _ in range(2 * bsz)))
    walk(zl, hl, nl_tiles, carries)


def _s5_out_kernel(u_ref, hfr, hfi, hbr, hbi, bd_ref, rc_ref, y_ref, wm_ref, wr_ref):
    kt = hfr.shape[0]
    lt = S5_LANE_TILE

    @pl.when(pl.program_id(1) == 0)
    def _():
        for s in range(S5_CHUNK):
            for t in range(S5_CHUNK):
                wm_ref[s * lt:(s + 1) * lt, t * lt:(t + 1) * lt] = bd_ref[t - s + S5_CHUNK - 1]
        for i in range(4):
            wr_ref[i] = _expand_blockdiag(rc_ref[i], S5_STATE, S5_GROUP, lt, wr_ref.shape[2])

    y = _dot(_s5_load_chunks(u_ref, kt), wm_ref[...])
    for i, h_ref in enumerate((hfr, hfi, hbr, hbi)):
        y += _dot(h_ref[...].astype(BF16), wr_ref[i])
    w = y_ref.shape[1]
    for t in range(S5_CHUNK):
        y_ref[pl.ds(t, kt, stride=S5_CHUNK), :] = y[:, t * w:(t + 1) * w]


def _s5_matrices(lam_re, lam_im, log_step, b_re, b_im, c_re, c_im):
    hi = lax.Precision.HIGHEST
    L = S5_CHUNK
    step = jnp.exp(log_step)[..., None]
    n = jnp.arange(L + 1, dtype=F32)[:, None, None, None]
    mag = jnp.exp(n * (lam_re * step)[None])
    ang = n * (lam_im * step)[None]
    p_re, p_im = mag * jnp.cos(ang), mag * jnp.sin(ang)
    ab_re, ab_im = p_re[1], p_im[1]
    den = lam_re * lam_re + lam_im * lam_im
    coef_re = ((ab_re - 1.0) * lam_re + ab_im * lam_im) / den
    coef_im = (ab_im * lam_re - (ab_re - 1.0) * lam_im) / den
    e_re = p_re[:L] * coef_re - p_im[:L] * coef_im
    e_im = p_re[:L] * coef_im + p_im[:L] * coef_re
    ce_re = c_re[None] * e_re[..., None, :] - c_im[None] * e_im[..., None, :]
    ce_im = c_re[None] * e_im[..., None, :] + c_im[None] * e_re[..., None, :]
    kern = (jnp.einsum('ndgps,dgsq->ndgpq', ce_re, b_re, precision=hi)
            - jnp.einsum('ndgps,dgsq->ndgpq', ce_im, b_im, precision=hi))
    w_re, w_im = e_re[::-1], e_im[::-1]
    q_re = (w_re[..., None] * b_re[None] - w_im[..., None] * b_im[None]).transpose(1, 2, 0, 4, 3)
    q_im = (w_re[..., None] * b_im[None] + w_im[..., None] * b_re[None]).transpose(1, 2, 0, 4, 3)
    a_re, a_im = p_re[1:], p_im[1:]
    ca_re = c_re[None] * a_re[..., None, :] - c_im[None] * a_im[..., None, :]
    ca_im = c_re[None] * a_im[..., None, :] + c_im[None] * a_re[..., None, :]
    r_re = ca_re.transpose(1, 2, 4, 0, 3)
    r_im = (-ca_im).transpose(1, 2, 4, 0, 3)
    nl = L * jnp.arange(1, 9, dtype=F32)[:, None, None, None]
    t_mag = jnp.exp(nl * (lam_re * step)[None])
    t_ang = nl * (lam_im * step)[None]
    return kern, q_re, q_im, r_re, r_im, t_mag * jnp.cos(t_ang), t_mag * jnp.sin(t_ang)


def _s5_layer(u_lat, u_ctx, params, *, bsz, seq, ctx_len):
    c = u_lat.shape[1]
    L, pp, ns = S5_CHUNK, S5_GROUP, S5_STATE
    g = c // pp
    kern, q_re, q_im, r_re, r_im, t_re, t_im = _s5_matrices(*params)
    gl = S5_LANE_TILE // pp
    nb = g // gl
    lt = S5_LANE_TILE
    kw = L * lt
    sw = gl * ns
    lanes = g * ns
    kt_ = kern.transpose(0, 1, 2, 4, 3)
    lags = jnp.concatenate([kt_[:0:-1, 1], (kt_[0, 0] + kt_[0, 1])[None], kt_[1:, 0]], axis=0)
    eye = jnp.eye(gl, dtype=F32)
    bd = (lags.reshape(2 * L - 1, nb, gl, pp, pp).transpose(1, 0, 2, 3, 4)[:, :, :, :, None, :]
          * eye[None, None, :, None, :, None]).reshape(nb, 2 * L - 1, lt, lt).astype(BF16)
    qs = jnp.stack([q_re[0], q_im[0], q_re[1][:, ::-1], q_im[1][:, ::-1]])
    qc = qs.reshape(4, nb, gl, L, pp, ns).transpose(1, 3, 2, 4, 0, 5).reshape(nb, L, lt, 4 * ns).astype(BF16)
    rs = jnp.stack([r_re[0], r_im[0], r_re[1][:, :, ::-1], r_im[1][:, :, ::-1]])
    rc = rs.reshape(4, nb, gl, ns, L, pp).transpose(1, 0, 2, 3, 4, 5).reshape(nb, 4, sw, L * pp).astype(BF16)
    pf =jnp.stack([t_re[:, 0], t_im[:, 0]]).reshape(2, 8, lanes)
    pb = jnp.stack([t_re[::-1, 1], t_im[::-1, 1]]).reshape(2, 8, lanes)

    u_spec = lambda kt: pl.BlockSpec((kt * L, S5_LANE_TILE), lambda b, i: (i, b))
    h_spec = lambda kt: pl.BlockSpec((kt, sw), lambda b, i: (i, b))
    cp2 = _cparams(("parallel", "arbitrary"))

    def state(u):
        rows = u.shape[0] // L
        kt = min(S5_ROW_TILE, rows)
        return pl.pallas_call(
            _s5_state_kernel,
            out_shape=[jax.ShapeDtypeStruct((rows, lanes), F32)] * 4,
            grid=(nb, rows // kt),
            in_specs=[u_spec(kt), pl.BlockSpec((None, L, lt, 4 * ns), lambda b, i: (b, 0, 0, 0))],
            out_specs=[h_spec(kt)] * 4,
            scratch_shapes=[pltpu.VMEM((kw, 4 * sw), BF16)],
            compiler_params=cp2,
            name="s5_state",
        )(u, qc)

    z_c, z_l = state(u_ctx), state(u_lat)
    lb = S5_LANES
    rows_c, rows_l = z_c[0].shape[0], z_l[0].shape[0]
    h = pl.pallas_call(
        functools.partial(_s5_rec_kernel, bsz=bsz, nc_tiles=rows_c // bsz // 8, nl_tiles=rows_l // bsz // 8),
        out_shape=[jax.ShapeDtypeStruct((rows_c, lanes), F32)] * 4 + [jax.ShapeDtypeStruct((rows_l, lanes), F32)] * 4,
        grid=(lanes // lb,),
        in_specs=[pl.BlockSpec((rows_c, lb), lambda j: (0, j))] * 4
        + [pl.BlockSpec((rows_l, lb), lambda j: (0, j))] * 4
        + [pl.BlockSpec((2, 8, lb), lambda j: (0, 0, j))] * 2,
        out_specs=[pl.BlockSpec((rows_c, lb), lambda j: (0, j))] * 4
        + [pl.BlockSpec((rows_l, lb), lambda j: (0, j))] * 4,
        compiler_params=_cparams(("parallel",)),
        name="s5_recurrence",
    )(*z_c, *z_l, pf, pb)

    def readout(u, hs):
        rows = u.shape[0] // L
        kt = min(S5_ROW_TILE, rows)
        return pl.pallas_call(
            _s5_out_kernel,
            out_shape=jax.ShapeDtypeStruct(u.shape, F32),
            grid=(nb, rows // kt),
            in_specs=[u_spec(kt)] + [h_spec(kt)] * 4
            + [pl.BlockSpec((None, 2 * L - 1, lt, lt), lambda b, i: (b, 0, 0, 0)),
               pl.BlockSpec((None, 4, sw, L * pp), lambda b, i: (b, 0, 0, 0))],
            out_specs=u_spec(kt),
            scratch_shapes=[pltpu.VMEM((kw, kw), BF16), pltpu.VMEM((4, sw, kw), BF16)],
            compiler_params=cp2,
            name="s5_out",
        )(u, *hs, bd, rc)

    return readout(u_lat, h[4:]), readout(u_ctx, h[:4])


def _head_norm(o, g_ref, center, dh):
    outs = []
    for h in range(HEADS):
        oh = o[:, h * dh:(h + 1) * dh]
        if center:
            oh = oh - jnp.mean(oh, axis=-1, keepdims=True)
        outs.append(oh * lax.rsqrt(jnp.mean(oh * oh, axis=-1, keepdims=True) + EPS))
    return jnp.concatenate(outs, axis=-1) * g_ref[...]


def _finish(x_ref, a, b, wout_ref, mod_ref, gpost_ref, o_ref, n_r):
    half = a.shape[1]
    b = b.astype(BF16)
    if n_r:
        b = _dot(_perm_matrix(b.shape[0], n_r, False), b).astype(BF16)
    y = _dot(a.astype(BF16), wout_ref[0:half, :]) + _dot(b, wout_ref[half:2 * half, :])
    out = _tile2d(x_ref) + mod_ref[5:6, :] * _rms(y) * gpost_ref[1:2, :]
    o_ref[...] = out.reshape(o_ref.shape)


def _even_merge_kernel(x_ref, ga_ref, oaf_ref, oab_ref, gb_ref, obf_ref, obb_ref, na_ref, nb_ref,
                       wout_ref, mod_ref, gpost_ref, o_ref, *, dh, n_r):
    a = _silu(_tile2d(ga_ref)) * _head_norm(_tile2d(oaf_ref) + _tile2d(oab_ref), na_ref, False, dh)
    b = _sigmoid(_tile2d(gb_ref)) * _head_norm(_tile2d(obf_ref) + _tile2d(obb_ref), nb_ref, True, dh)
    _finish(x_ref, a, b, wout_ref, mod_ref, gpost_ref, o_ref, n_r)


def _odd_merge_kernel(x_ref, u_ref, ys_ref, gb_ref, obf_ref, obb_ref, sd_ref, wglu_ref, bglu_ref, nb_ref,
                      wout_ref, mod_ref, gpost_ref, o_ref, *, dh, n_r):
    y = _tile2d(ys_ref) + sd_ref[...] * _tile2d(u_ref)
    g = jax.nn.gelu(y)
    a = g * _sigmoid(_dot(g.astype(BF16), wglu_ref[...]) + bglu_ref[...])
    b = _silu(_tile2d(gb_ref)) * _head_norm(_tile2d(obf_ref) + _tile2d(obb_ref), nb_ref, True, dh)
    _finish(x_ref, a, b, wout_ref, mod_ref, gpost_ref, o_ref, n_r)


def _merge(kernel, s, rm_lat, cm_lat, rm_ctx, cm_ctx, consts, w_out, mod, gpost, *, bsz, seq, ctx_len, name):
    d = s.shape[1]
    half = d // 2
    rows = seq // GRID_W
    rt = min(MERGE_ROWS, rows)
    n_rh = rows // rt
    ncb = GRID_W // COL_TILE
    per_b = n_rh * ncb
    const_specs = [pl.BlockSpec(a.shape, lambda i, nd=a.ndim: (0,) * nd) for a in consts] \
        + [pl.BlockSpec((d, d), lambda i: (0, 0))]
    gpost_spec = pl.BlockSpec((3, d), lambda i: (0, 0))
    cp = _cparams(("parallel",))

    rm_idx = lambda i: ((i // per_b) * n_rh + (i % per_b) // ncb, i % ncb)
    cm_idx = lambda i: ((i // per_b) * ncb + i % ncb, (i % per_b) // ncb)
    s3 = s.reshape(s.shape[0] // GRID_W, GRID_W, d)
    ins = [s3] + [a.reshape(bsz * rows, GRID_W, a.shape[1]) for a, _ in rm_lat] \
        + [a.reshape(bsz * GRID_W, rows, a.shape[1]) for a, _ in cm_lat]
    in_specs = [pl.BlockSpec((rt, COL_TILE, d), lambda i: rm_idx(i) + (0,))] \
        + [pl.BlockSpec((rt, COL_TILE, half), lambda i, cb=cb: rm_idx(i) + (cb,)) for _, cb in rm_lat] \
        + [pl.BlockSpec((COL_TILE, rt, half), lambda i, cb=cb: cm_idx(i) + (cb,)) for _, cb in cm_lat]
    n_out_groups = s3.shape[0] if rm_ctx is not None else bsz * rows
    out = pl.pallas_call(
        functools.partial(kernel, n_r=rt),
        out_shape=jax.ShapeDtypeStruct((n_out_groups, GRID_W, d), F32),
        grid=(bsz * per_b,),
        in_specs=in_specs + const_specs
        + [pl.BlockSpec((None, N_MOD, d), lambda i: (i // per_b, 0, 0)), gpost_spec],
        out_specs=pl.BlockSpec((rt, COL_TILE, d), lambda i: rm_idx(i) + (0,)),
        compiler_params=cp,
        name=name + "_lat",
    )(*ins, *consts, w_out, mod, gpost)
    out = out.reshape(n_out_groups * GRID_W, d)
    if rm_ctx is None:
        return out

    tm = rt * COL_TILE
    t0 = bsz * seq // tm
    tiles = list(rm_ctx) + list(cm_ctx)
    n_in = 1 + len(tiles) + len(consts) + 3
    inner = functools.partial(kernel, n_r=0)
    return pl.pallas_call(
        lambda *refs: inner(*refs[:n_in], refs[n_in + 1]),
        out_shape=jax.ShapeDtypeStruct(out.shape, F32),
        grid=(bsz * ctx_len // tm,),
        in_specs=[pl.BlockSpec((tm, d), lambda i: (t0 + i, 0))]
        + [pl.BlockSpec((tm, half), lambda i, cb=cb: (i, cb)) for _, cb in tiles] + const_specs
        + [pl.BlockSpec((None, N_MOD, d), lambda i: (bsz, 0, 0)), gpost_spec,
           pl.BlockSpec(memory_space=pl.ANY)],
        out_specs=pl.BlockSpec((tm, d), lambda i: (t0 + i, 0)),
        input_output_aliases={n_in: 0},
        compiler_params=cp,
        name=name + "_ctx",
    )(s, *[a for a, _ in tiles], *consts, w_out, mod, gpost, out)


def kernel(x, c, ctx, c_ctx, w_mod, b_mod, norm_pre, norm_post, ffn_w_gate, ffn_w_up, ffn_w_down, ev_w_in, ev_w_out, gla_w_gate, gla_b_gate, gla_norm, ml_conv_w, ml_conv_b, ml_b_gates, ml_norm, od_w_in, od_w_out, s5_lam_re, s5_lam_im, s5_log_step, s5_b_re, s5_b_im, s5_c_re, s5_c_im, s5_d, s5_w_glu, s5_b_glu, ret_log_decay, ret_norm):
    bsz, seq, d = x.shape
    ctx_len = ctx.shape[1]
    depth = w_mod.shape[0]
    half = d // 2
    dh = half // HEADS
    gla_dk = dh // 2
    d_ff = ffn_w_gate.shape[-1]
    tm = TOKEN_TILE
    rows = seq // GRID_W
    assert seq % tm == 0 and (bsz * ctx_len) % tm == 0 and seq == rows * GRID_W and ctx_len % CHUNK == 0
    assert 2 * bsz == 8 and rows % 8 == 0 and seq % CHUNK == 0 and half == PROJ_TILE
    assert ctx_len % (8 * S5_CHUNK) == 0 and seq % (8 * S5_CHUNK) == 0
    assert all((n // S5_CHUNK) % min(S5_ROW_TILE, n // S5_CHUNK) == 0 for n in (bsz * ctx_len, bsz * seq))

    tiles_x = bsz * seq // tm
    tiles_all = tiles_x + bsz * ctx_len // tm
    tiles_per_batch = seq // tm
    mod_of_tile = lambda i: jnp.minimum(i // tiles_per_batch, bsz)

    wg, wu, wd = ffn_w_gate.astype(BF16), ffn_w_up.astype(BF16), ffn_w_down.astype(BF16)

    cc = jnp.concatenate([c, c_ctx[None], jnp.zeros((8 - bsz - 1, d), F32)], axis=0)
    mod_all = _modulation(cc, w_mod, b_mod).reshape(depth, 8, N_MOD, d)

    s = None
    dims = dict(bsz=bsz, seq=seq, ctx_len=ctx_len)

    for layer in range(depth):
        need_ctx = layer < depth - 1
        mod = mod_all[layer]
        gpre, gpost = norm_pre[layer], norm_post[layer]
        ffn = functools.partial(_ffn, mod=mod, gpre=gpre, gpost=gpost, wg=wg, wu=wu, wd=wd, layer=layer,
                                mod_of_tile=mod_of_tile)
        if s is None:
            s = ffn(x.reshape(bsz * seq, d), which=0, sub=0, n_tiles=tiles_x, out_tiles=tiles_all)
            s = ffn(ctx.reshape(bsz * ctx_len, d), which=0, sub=0, n_tiles=tiles_all - tiles_x, out_tile0=tiles_x,
                    out_tiles=tiles_all, prev=s)
        else:
            s = ffn(s, which=0, sub=0, n_tiles=tiles_all)

        if layer % 2 == 0:
            e = layer // 2
            w = ev_w_in[e]
            qk_w = 2 * HEADS * gla_dk
            n_gate = 4 * HEADS
            c_low = qk_w + 2 * half
            c_ml = c_low + 2 * GLA_RANK
            c_g = c_ml + 4 * half
            w_in = jnp.concatenate([w[:, :c_low], w[:, c_ml:c_g]], axis=1).astype(BF16)
            w_small = jnp.concatenate([w[:, c_low:c_ml], jnp.zeros((d, SMALL_W - 2 * GLA_RANK), F32),
                                       w[:, c_g:], jnp.zeros((d, SMALL_W - n_gate), F32)], axis=1).astype(BF16)
            (pa, pb, psa, psb), (pa_c, pb_c, psa_c, psb_c) = _proj(s, mod, gpre, w_in, w_small,
                                                                   n_a=c_low // PROJ_TILE, **dims)

            st0 = [jnp.zeros((bsz, 2, HEADS, dh, gla_dk), F32)]
            gla_ins = lambda p, ps: [(p, qk_w // 2, 0), (p, qk_w // 2, qk_w // 2), (p, half, qk_w), (ps, SMALL_W, 0)]
            oa, oa_c = _bidir_scan(
                functools.partial(_gla_kernel, L=CHUNK, dk=gla_dk, dv=dh, gated=True,
                                  q_scale=float(gla_dk) ** -0.5, k_scale=1.0),
                gla_ins(pa, psa), gla_ins(pa_c, psa_c), [gla_w_gate[e].astype(BF16), gla_b_gate[e]], st0,
                out_width=half, name="gla", **dims)

            scale = jnp.concatenate([jnp.ones((half,), F32), jnp.full((half,), float(dh) ** -0.5, F32)])
            qk = _conv(pb, ml_conv_w[e], ml_conv_b[e], scale, bsz=bsz, n=seq)
            qk_c = _conv(pb_c, ml_conv_w[e], ml_conv_b[e], scale, bsz=bsz, n=ctx_len)
            bias = jnp.zeros((1, SMALL_W), F32).at[0, :n_gate].set(ml_b_gates[e].reshape(-1))
            ml_ins = lambda qk_, p, ps: [(qk_, half, 0), (qk_, half, half), (p, half, 2 * half), (ps, SMALL_W, 0)]
            st0 = [jnp.zeros((bsz, 2, HEADS, dh, dh), F32), jnp.zeros((bsz, 2, HEADS, 8, dh), F32)]
            ob, ob_c = _bidir_scan(
                functools.partial(_mlstm_kernel, L=CHUNK, dh=dh),
                ml_ins(qk, pb, psb), ml_ins(qk_c, pb_c, psb_c), [bias], st0,
                out_width=half, name="mlstm", **dims)

            ga_cb, gb_cb = (qk_w + half) // half, 3
            s = _merge(
                functools.partial(_even_merge_kernel, dh=dh), s,
                [(pa, ga_cb), (oa[0], 0), (oa[1], 0)], [(pb, gb_cb), (ob[0], 0), (ob[1], 0)],
                [(pa_c, ga_cb), (oa_c[0], 0), (oa_c[1], 0)] if need_ctx else None,
                [(pb_c, gb_cb), (ob_c[0], 0), (ob_c[1], 0)],
                [gla_norm[e].reshape(1, half), ml_norm[e].reshape(1, half)],
                ev_w_out[e].astype(BF16), mod, gpost, name="even_merge", **dims)
        else:
            o = layer // 2
            (pa, pb), (pa_c, pb_c) = _proj(s, mod, gpre, od_w_in[o].astype(BF16), None, n_a=1, **dims)

            ys, ys_c = _s5_layer(pa, pa_c, (s5_lam_re[o], s5_lam_im[o], s5_log_step[o], s5_b_re[o], s5_b_im[o],
                                            s5_c_re[o], s5_c_im[o]), **dims)

            lg = jnp.repeat(-jnp.exp(ret_log_decay[o]), dh, axis=1)
            ret_ins = lambda p: [(p, half, 0), (p, half, half), (p, half, 2 * half)]
            st0 = [jnp.zeros((bsz, 2, HEADS, dh, dh), F32)]
            od, od_c = _bidir_scan(
                functools.partial(_gla_kernel, L=CHUNK, dk=dh, dv=dh, gated=False,
                                  q_scale=1.0, k_scale=float(dh) ** -0.5),
                ret_ins(pb), ret_ins(pb_c), [lg], st0, out_width=half, name="retention", **dims)

            s = _merge(
                functools.partial(_odd_merge_kernel, dh=dh), s,
                [(pa, 0), (ys, 0)], [(pb, 3), (od[0], 0), (od[1], 0)],
                [(pa_c, 0), (ys_c, 0)] if need_ctx else None, [(pb_c, 3), (od_c[0], 0), (od_c[1], 0)],
                [s5_d[o].reshape(1, half), s5_w_glu[o].astype(BF16), s5_b_glu[o].reshape(1, half),
                 ret_norm[o].reshape(1, half)],
                od_w_out[o].astype(BF16), mod, gpost, name="odd_merge", **dims)

        s = ffn(s, which=1, sub=2, n_tiles=tiles_all if need_ctx else tiles_x)

    return s[:bsz * seq].reshape(bsz, seq, d)
```

```python
import functools

import jax
import jax.numpy as jnp
from jax import lax
from jax.experimental import pallas as pl
from jax.experimental.pallas import tpu as pltpu

F32 = jnp.float32
BF16 = jnp.bfloat16

EPS = 1e-6
GRID_W = 64
COL_TILE = 8
CHUNK = 64
SCAN_BATCH = 2
FFN_RES = 0.5
N_MOD = 9
HEADS = 4
GLA_RANK = 16
GLA_GATE_NORM = 16.0
S5_GROUP = 16
S5_STATE = 64
S5_CHUNK = 8
S5_LANE_TILE = 128
S5_ROW_TILE = 512
SMALL_W = 128

TOKEN_TILE = 512
MERGE_ROWS = 32
FF_TILE = 1024
PROJ_TILE = 1024
MOD_TILE = 1024
CONV_TILE = 256
CONV_ROWS = 256
S5_LANES = 256
VMEM_LIMIT = 56 * 1024 * 1024


def _cparams(sem):
    return pltpu.CompilerParams(dimension_semantics=sem, vmem_limit_bytes=VMEM_LIMIT)


def _rms(x):
    return x * lax.rsqrt(jnp.mean(x * x, axis=-1, keepdims=True) + EPS)


def _sigmoid(x):
    return 1.0 / (1.0 + jnp.exp(-x))


def _silu(x):
    return x * _sigmoid(x)


def _log_sigmoid(x):
    return jnp.minimum(x, 0.0) - jnp.log(1.0 + jnp.exp(-jnp.abs(x)))


def _dot(a, b):
    return jnp.dot(a, b, preferred_element_type=F32)


def _dot_nt(a, b):
    return lax.dot_general(a, b, (((1,), (1,)), ((), ())), preferred_element_type=F32)


def _dot_tn(a, b):
    return lax.dot_general(a, b, (((0,), (0,)), ((), ())), preferred_element_type=F32)


def _dot01(t01, x):
    x0 = x.astype(BF16)
    r1 = x - x0.astype(F32)
    x1 = r1.astype(BF16)
    x2 = (r1 - x1.astype(F32)).astype(BF16)
    return _dot(t01, x0) + _dot(t01, x1) + _dot(t01, x2)


def _perm_matrix(n, n_r, to_col_major):
    i = lax.broadcasted_iota(jnp.int32, (n, n), 0)
    c = lax.broadcasted_iota(jnp.int32, (n, n), 1)
    if to_col_major:
        hit = (i // n_r == c % COL_TILE) & (i % n_r == c // COL_TILE)
    else:
        hit = (i // COL_TILE == c % n_r) & (i % COL_TILE == c // n_r)
    return jnp.where(hit, 1.0, 0.0).astype(BF16)


def _tile2d(ref):
    v = ref[...]
    return v.reshape(-1, v.shape[-1]).astype(F32)


def _mod_kernel(c_ref, w_ref, b_ref, o_ref):
    a = _silu(c_ref[...]).astype(BF16)
    o_ref[...] = _dot(a, w_ref[...].astype(BF16)) + b_ref[...]


def _modulation(cc, w_mod, b_mod):
    depth, d, nm = w_mod.shape
    return pl.pallas_call(
        _mod_kernel,
        out_shape=jax.ShapeDtypeStruct((depth, 8, nm), F32),
        grid=(depth, nm // MOD_TILE),
        in_specs=[pl.BlockSpec((8, d), lambda l, j: (0, 0)),
                  pl.BlockSpec((None, d, MOD_TILE), lambda l, j: (l, 0, j)),
                  pl.BlockSpec((None, 1, MOD_TILE), lambda l, j: (l, 0, j))],
        out_specs=pl.BlockSpec((None, 8, MOD_TILE), lambda l, j: (l, 0, j)),
        compiler_params=_cparams(("parallel", "parallel")),
        name="modulation",
    )(cc, w_mod, b_mod.reshape(depth, 1, nm))


def _ffn_kernel(x_ref, mod_ref, gpre_ref, gpost_ref, wg_ref, wu_ref, wd_ref, o_ref, h_ref, acc_ref, *, sub, nff,
                tail):
    j = pl.program_id(1)

    @pl.when(j == 0)
    def _():
        shift = mod_ref[3 * sub:3 * sub + 1, :]
        scale = mod_ref[3 * sub + 1:3 * sub + 2, :]
        h = _rms(x_ref[...]) * gpre_ref[sub:sub + 1, :] * (1.0 + scale) + shift
        h_ref[...] = h.astype(BF16)

    def hidden_step(valid, first):
        h = h_ref[...]
        t = _silu(_dot(h, wg_ref[:, :valid])) * _dot(h, wu_ref[:, :valid])
        y = _dot(t.astype(BF16), wd_ref[:valid, :])
        if first:
            acc_ref[...] = y
        else:
            acc_ref[...] += y

    pl.when(j == 0)(lambda: hidden_step(tail or FF_TILE, True))
    pl.when(j > 0)(lambda: hidden_step(FF_TILE, False))

    @pl.when(j == nff - 1)
    def _():
        gate = mod_ref[3 * sub + 2:3 * sub + 3, :]
        o_ref[...] = x_ref[...] + FFN_RES * gate * _rms(acc_ref[...]) * gpost_ref[sub:sub + 1, :]


def _ffn(s, mod, gpre, gpost, wg, wu, wd, *, layer, which, sub, n_tiles, mod_of_tile, out_tile0=0, out_tiles=None,
         prev=None):
    d = s.shape[1]
    d_ff = wg.shape[-1]
    nff = pl.cdiv(d_ff, FF_TILE)
    assert nff >= 2 and (d_ff % FF_TILE) % 128 == 0
    tm = TOKEN_TILE
    out_tiles = n_tiles if out_tiles is None else out_tiles
    kernel = functools.partial(_ffn_kernel, sub=sub, nff=nff, tail=d_ff % FF_TILE)
    blk = lambda j: (j + nff - 1) % nff
    ins = [s, mod, gpre, gpost, wg, wu, wd]
    in_specs = [pl.BlockSpec((tm, d), lambda i, j: (i, 0)),
                pl.BlockSpec((None, N_MOD, d), lambda i, j: (mod_of_tile(out_tile0 + i), 0, 0)),
                pl.BlockSpec((3, d), lambda i, j: (0, 0)),
                pl.BlockSpec((3, d), lambda i, j: (0, 0)),
                pl.BlockSpec((None, None, d, FF_TILE), lambda i, j: (layer, which, 0, blk(j))),
                pl.BlockSpec((None, None, d, FF_TILE), lambda i, j: (layer, which, 0, blk(j))),
                pl.BlockSpec((None, None, FF_TILE, d), lambda i, j: (layer, which, blk(j), 0))]
    aliases = {}
    if prev is not None:
        n_in = len(ins)
        inner = kernel
        kernel = lambda *refs: inner(*refs[:n_in], *refs[n_in + 1:])
        ins.append(prev)
        in_specs.append(pl.BlockSpec(memory_space=pl.ANY))
        aliases = {n_in: 0}
    return pl.pallas_call(
        kernel,
        out_shape=jax.ShapeDtypeStruct((out_tiles * tm, d), F32),
        grid=(n_tiles, nff),
        in_specs=in_specs,
        out_specs=pl.BlockSpec((tm, d), lambda i, j: (out_tile0 + i, 0)),
        scratch_shapes=[pltpu.VMEM((tm, d), BF16), pltpu.VMEM((tm, d), F32)],
        input_output_aliases=aliases,
        compiler_params=_cparams(("parallel", "arbitrary")),
        name="ffn",
    )(*ins)


def _proj_kernel(*refs, n_a, n_r, has_small):
    if has_small:
        x_ref, mod_ref, gpre_ref, w_ref, ws_ref, pa_ref, pb_ref, psa_ref, psb_ref, h_ref, hp_ref = refs
    else:
        x_ref, mod_ref, gpre_ref, w_ref, pa_ref, pb_ref, h_ref, hp_ref = refs
    j = pl.program_id(1)
    tm = h_ref.shape[0]

    @pl.when(j == 0)
    def _():
        x = _tile2d(x_ref)
        h = (_rms(x) * gpre_ref[1:2, :] * (1.0 + mod_ref[4:5, :]) + mod_ref[3:4, :]).astype(BF16)
        hp = _dot(_perm_matrix(tm, n_r, True), h).astype(BF16) if n_r else h
        h_ref[...] = h
        hp_ref[...] = hp
        if has_small:
            psa_ref[...] = _dot(h, ws_ref[:, :SMALL_W]).reshape(psa_ref.shape)
            psb_ref[...] = _dot(hp, ws_ref[:, SMALL_W:])

    @pl.when(j < n_a)
    def _():
        pa_ref[...] = _dot(h_ref[...], w_ref[j]).reshape(pa_ref.shape)

    @pl.when(j >= n_a)
    def _():
        pb_ref[...] = _dot(hp_ref[...], w_ref[j])


def _proj(s, mod, gpre, w, w_small, *, n_a, bsz, seq, ctx_len):
    d = s.shape[1]
    rows = seq // GRID_W
    tn = PROJ_TILE
    n_tiles_n = w.shape[1] // tn
    n_b = n_tiles_n - n_a
    has_small = w_small is not None
    w_ins = [w.reshape(d, n_tiles_n, tn).transpose(1, 0, 2)] + ([w_small] if has_small else [])
    w_specs = [pl.BlockSpec((n_tiles_n, d, tn), lambda i, j: (0, 0, 0), pipeline_mode=pl.Buffered(1))]
    if has_small:
        w_specs.append(pl.BlockSpec((d, 2 * SMALL_W), lambda i, j: (0, 0)))
    ja = lambda j: jnp.minimum(j, n_a - 1)
    jb = lambda j: jnp.maximum(j - n_a, 0)
    cp = _cparams(("parallel", "arbitrary"))

    tm = rows * COL_TILE
    ncb = GRID_W // COL_TILE
    s3 = s.reshape(s.shape[0] // GRID_W, GRID_W, d)
    out_shape = [jax.ShapeDtypeStruct((bsz * rows, GRID_W, n_a * tn), F32),
                 jax.ShapeDtypeStruct((bsz * seq, n_b * tn), F32)]
    out_specs = [pl.BlockSpec((rows, COL_TILE, tn), lambda i, j: (i // ncb, i % ncb, ja(j))),
                 pl.BlockSpec((tm, tn), lambda i, j: (i, jb(j)))]
    if has_small:
        out_shape += [jax.ShapeDtypeStruct((bsz * rows, GRID_W, SMALL_W), F32),
                      jax.ShapeDtypeStruct((bsz * seq, SMALL_W), F32)]
        out_specs += [pl.BlockSpec((rows, COL_TILE, SMALL_W), lambda i, j: (i // ncb, i % ncb, 0)),
                      pl.BlockSpec((tm, SMALL_W), lambda i, j: (i, 0))]
    lat = pl.pallas_call(
        functools.partial(_proj_kernel, n_a=n_a, n_r=rows, has_small=has_small),
        out_shape=out_shape,
        grid=(bsz * ncb, n_tiles_n),
        in_specs=[pl.BlockSpec((rows, COL_TILE, d), lambda i, j: (i // ncb, i % ncb, 0)),
                  pl.BlockSpec((None, N_MOD, d), lambda i, j: (i // ncb, 0, 0)),
                  pl.BlockSpec((3, d), lambda i, j: (0, 0))] + w_specs,
        out_specs=out_specs,
        scratch_shapes=[pltpu.VMEM((tm, d), BF16), pltpu.VMEM((tm, d), BF16)],
        compiler_params=cp,
        name="in_proj_lat",
    )(s3, mod, gpre, *w_ins)
    lat = [lat[0].reshape(bsz * seq, n_a * tn), lat[1]] \
        + ([lat[2].reshape(bsz * seq, SMALL_W), lat[3]] if has_small else [])

    tc = TOKEN_TILE
    n_c = bsz * ctx_len
    t0 = bsz * seq // tc
    out_shape = [jax.ShapeDtypeStruct((n_c, n_a * tn), F32), jax.ShapeDtypeStruct((n_c, n_b * tn), F32)]
    out_specs = [pl.BlockSpec((tc, tn), lambda i, j: (i, ja(j))), pl.BlockSpec((tc, tn), lambda i, j: (i, jb(j)))]
    if has_small:
        out_shape += [jax.ShapeDtypeStruct((n_c, SMALL_W), F32)] * 2
        out_specs += [pl.BlockSpec((tc, SMALL_W), lambda i, j: (i, 0))] * 2
    ctx = pl.pallas_call(
        functools.partial(_proj_kernel, n_a=n_a, n_r=0, has_small=has_small),
        out_shape=out_shape,
        grid=(n_c // tc, n_tiles_n),
        in_specs=[pl.BlockSpec((tc, d), lambda i, j: (t0 + i, 0)),
                  pl.BlockSpec((None, N_MOD, d), lambda i, j: (bsz, 0, 0)),
                  pl.BlockSpec((3, d), lambda i, j: (0, 0))] + w_specs,
        out_specs=out_specs,
        scratch_shapes=[pltpu.VMEM((tc, d), BF16), pltpu.VMEM((tc, d), BF16)],
        compiler_params=cp,
        name="in_proj_ctx",
    )(s, mod, gpre, *w_ins)
    return lat, list(ctx)


def _conv_kernel(x_ref, w_ref, b_ref, sc_ref, o_ref):
    n, ct = x_ref.shape
    ch = CONV_ROWS
    n_ch = n // ch
    r = lax.broadcasted_iota(jnp.int32, (ch, ct), 0)

    def body(c, carry):
        r0 = pl.multiple_of(c * ch, ch)
        x = x_ref[pl.ds(r0, ch), :]
        before = x_ref[pl.ds(pl.multiple_of(jnp.maximum(r0 - 8, 0), 8), 8), :][7:8, :]
        after = x_ref[pl.ds(pl.multiple_of(jnp.minimum(r0 + ch, n - 8), 8), 8), :][0:1, :]
        before = jnp.where(c == 0, 0.0, before)
        after = jnp.where(c == n_ch - 1, 0.0, after)
        prev = jnp.where(r == 0, before, pltpu.roll(x, 1, 0))
        nxt = jnp.where(r == ch - 1, after, pltpu.roll(x, ch - 1, 0))
        y = w_ref[1:2, :] * x + w_ref[0:1, :] * prev + w_ref[2:3, :] * nxt + b_ref[...]
        o_ref[pl.ds(r0, ch), :] = _silu(y) * sc_ref[...]
        return carry

    lax.fori_loop(0, n_ch, body, 0)


def _conv(p, conv_w, conv_b, scale, *, bsz, n):
    c = conv_w.shape[1]
    ct = CONV_TILE
    return pl.pallas_call(
        _conv_kernel,
        out_shape=jax.ShapeDtypeStruct((bsz * n, c), F32),
        grid=(bsz, c // ct),
        in_specs=[pl.BlockSpec((n, ct), lambda b, j: (b, j)),
                  pl.BlockSpec((3, ct), lambda b, j: (0, j)),
                  pl.BlockSpec((1, ct), lambda b, j: (0, j)),
                  pl.BlockSpec((1, ct), lambda b, j: (0, j))],
        out_specs=pl.BlockSpec((n, ct), lambda b, j: (b, j)),
        compiler_params=_cparams(("parallel", "parallel")),
        name="conv",
    )(p, conv_w, conv_b.reshape(1, c), scale.reshape(1, c))


def _gla_kernel(*refs, L, dk, dv, gated, q_scale, k_scale):
    if gated:
        qf, kf, vf, df, qb, kb, vb, db, wgate_ref, bgate_ref, st_in, of, ob, st_out = refs
    else:
        qf, kf, vf, qb, kb, vb, lg_ref, st_in, of, ob, st_out = refs
        df = db = None

    @pl.when(pl.program_id(1) == 0)
    def _():
        st_out[...] = st_in[...]

    row = lax.broadcasted_iota(jnp.int32, (L, L), 0)
    col = lax.broadcasted_iota(jnp.int32, (L, L), 1)
    trow = lax.broadcasted_iota(jnp.int32, (L, 1), 0)
    for d, (q_ref, k_ref, v_ref, d_ref, o_ref) in enumerate(((qf, kf, vf, df, of), (qb, kb, vb, db, ob))):
        mask = (row >= col) if d == 0 else (row <= col)
        mid = L // 2 if d == 0 else L - 1 - L // 2
        end = L - 1 if d == 0 else 0
        if gated:
            low = d_ref[:, d * GLA_RANK:(d + 1) * GLA_RANK].astype(BF16)
            z = _dot(low, wgate_ref[d]) + bgate_ref[d:d + 1, :]
            la = _log_sigmoid(z) * (1.0 / GLA_GATE_NORM)
            bcum = _dot01(mask.astype(BF16), la)
        else:
            cnt = (trow + 1) if d == 0 else (L - trow)
            bcum = cnt.astype(F32) * lg_ref[d:d + 1, :]
        for h in range(HEADS):
            b = bcum[:, h * dk:(h + 1) * dk]
            b_mid = b[mid:mid + 1, :]
            b_end = b[end:end + 1, :]
            q = q_ref[:, h * dk:(h + 1) * dk] * q_scale
            k = k_ref[:, h * dk:(h + 1) * dk] * k_scale
            v = v_ref[:, h * dv:(h + 1) * dv].astype(BF16)
            qs = (q * jnp.exp(b - b_mid)).astype(BF16)
            ks = (k * jnp.exp(b_mid - b)).astype(BF16)
            scores = jnp.where(mask, _dot_nt(qs, ks), 0.0)
            st = st_out[d, h]
            o = _dot(scores.astype(BF16), v) + _dot_nt((q * jnp.exp(b)).astype(BF16), st.astype(BF16))
            o_ref[:, h * dv:(h + 1) * dv] = o.astype(o_ref.dtype)
            kd = (k * jnp.exp(b_end - b)).astype(BF16)
            st_out[d, h] = jnp.exp(b_end) * st + _dot_tn(v, kd)


def _mlstm_kernel(qf, kf, vf, gf, qb, kb, vb, gb, bias_ref, c_in, nm_in, of, ob, c_out, nm_out, *, L, dh):
    @pl.when(pl.program_id(1) == 0)
    def _():
        c_out[...] = c_in[...]
        nm_out[...] = nm_in[...]

    row = lax.broadcasted_iota(jnp.int32, (L, L), 0)
    col = lax.broadcasted_iota(jnp.int32, (L, L), 1)
    eye = row == col
    ones = jnp.ones((L, L), BF16)
    for d, (q_ref, k_ref, v_ref, g_ref, o_ref) in enumerate(((qf, kf, vf, gf, of), (qb, kb, vb, gb, ob))):
        mask = (row >= col) if d == 0 else (row <= col)
        cum_us = (row <= col) if d == 0 else (row >= col)
        end = L - 1 if d == 0 else 0
        g = g_ref[...] + bias_ref[...]
        fl = _log_sigmoid(g)
        bc_all = _dot01(mask.astype(BF16), fl)
        for h in range(HEADS):
            ci = d * 2 * HEADS + h
            cf = ci + HEADS
            i_col = g[:, ci:ci + 1]
            f_col = fl[:, cf:cf + 1]
            b_col = bc_all[:, cf:cf + 1]
            x = jnp.where(cum_us, f_col, 0.0) - jnp.where(eye, i_col, 0.0)
            drow = _dot01(ones, x)
            m_st = nm_out[d, h, 1:2, 0:1]
            dmat = jnp.where(mask, b_col - drow, -jnp.inf)
            inter = b_col + m_st
            m_t = jnp.maximum(inter, jnp.max(dmat, axis=-1, keepdims=True))
            q = q_ref[:, h * dh:(h + 1) * dh]
            k = k_ref[:, h * dh:(h + 1) * dh]
            v = v_ref[:, h * dh:(h + 1) * dh].astype(BF16)
            qh = q.astype(BF16)
            scores = _dot_nt(qh, k.astype(BF16)) * jnp.exp(dmat - m_t)
            w_inter = jnp.exp(inter - m_t)
            c_st = c_out[d, h]
            n_st = nm_out[d, h, 0:1, :]
            num = _dot(scores.astype(BF16), v) + w_inter * _dot(qh, c_st.astype(BF16))
            den = (jnp.sum(scores, axis=-1, keepdims=True)
                   + w_inter * jnp.sum(q * n_st, axis=-1, keepdims=True))
            hout = num / jnp.maximum(jnp.abs(den), jnp.exp(-m_t))
            o_ref[:, h * dh:(h + 1) * dh] = hout.astype(o_ref.dtype)
            b_end = b_col[end:end + 1, :]
            d_end = b_end - b_col + i_col
            m_new = jnp.maximum(b_end + m_st, jnp.max(d_end, axis=0, keepdims=True))
            kw = k * jnp.exp(d_end - m_new)
            decay = jnp.exp(b_end + m_st - m_new)
            c_out[d, h] = decay * c_st + _dot_tn(kw.astype(BF16), v)
            nm_out[d, h, 0:1, :] = decay * n_st + jnp.sum(kw, axis=0, keepdims=True)
            nm_out[d, h, 1:2, :] = jnp.broadcast_to(m_new, (1, dh))


def _scan_call(kernel, bsz, n, chunk_ins, consts, states, out_width, name, out_dtype=F32):
    L = CHUNK
    bg = SCAN_BATCH
    chunk = (lambda s: s, lambda s: n - 1 - s)
    ins, in_specs, batched = [], [], []
    for d in (0, 1):
        for a, width, coloff in chunk_ins:
            ins.append(a.reshape(bsz, n * L, a.shape[1]))
            in_specs.append(pl.BlockSpec((bg, L, width), lambda b, s, d=d, cb=coloff // width: (b, chunk[d](s), cb)))
            batched.append(True)
    for a in consts:
        ins.append(a)
        in_specs.append(pl.BlockSpec(a.shape, lambda b, s, nd=a.ndim: (0,) * nd))
        batched.append(False)
    state_specs = [pl.BlockSpec((bg,) + a.shape[1:], lambda b, s, nd=a.ndim: (b,) + (0,) * (nd - 1))
                   for a in states]
    ins += list(states)
    in_specs += state_specs
    batched += [True] * (len(states) + 2 + len(states))

    def grouped(*refs):
        for bi in range(bg):
            kernel(*[r.at[bi] if is_b else r for r, is_b in zip(refs, batched)])

    out_struct = jax.ShapeDtypeStruct((bsz, n * L, out_width), out_dtype)
    res = pl.pallas_call(
        grouped,
        out_shape=[out_struct, out_struct] + [jax.ShapeDtypeStruct(a.shape, F32) for a in states],
        grid=(bsz // bg, n),
        in_specs=in_specs,
        out_specs=[pl.BlockSpec((bg, L, out_width), lambda b, s, d=d: (b, chunk[d](s), 0)) for d in (0, 1)]
        + state_specs,
        compiler_params=_cparams(("parallel", "arbitrary")),
        name=name,
    )(*ins)
    return (res[0].reshape(bsz * n * L, out_width), res[1].reshape(bsz * n * L, out_width)), list(res[2:])


def _bidir_scan(kernel, ins_lat, ins_ctx, consts, states0, *, bsz, seq, ctx_len, out_width, name, out_dtype=F32):
    o_ctx, st = _scan_call(kernel, bsz, ctx_len // CHUNK, ins_ctx, consts, states0, out_width, name + "_ctx",
                           out_dtype)
    o_lat, _ = _scan_call(kernel, bsz, seq // CHUNK, ins_lat, consts, st, out_width, name + "_lat", out_dtype)
    return o_lat, o_ctx


def _s5_load_chunks(u_ref, kt):
    return jnp.concatenate([u_ref[pl.ds(t, kt, stride=S5_CHUNK), :].astype(BF16) for t in range(S5_CHUNK)],
                           axis=-1)


def _expand_blockdiag(c, rows_per_group, lanes_per_group, lane_period, n_out):
    r_n, k = c.shape
    kk = lax.broadcasted_iota(jnp.int32, (k, n_out), 0)
    jj = lax.broadcasted_iota(jnp.int32, (k, n_out), 1)
    sel = (kk // lanes_per_group == jj // lane_period) & (kk % lanes_per_group == jj % lanes_per_group)
    rr = lax.broadcasted_iota(jnp.int32, (r_n, n_out), 0)
    cc = lax.broadcasted_iota(jnp.int32, (r_n, n_out), 1)
    same = rr // rows_per_group == (cc % lane_period) // lanes_per_group
    return jnp.where(same, _dot(c, jnp.where(sel, 1.0, 0.0).astype(BF16)), 0.0).astype(BF16)


def _s5_state_kernel(u_ref, qc_ref, zfr_ref, zfi_ref, zbr_ref, zbi_ref, w_ref):
    kt = zfr_ref.shape[0]
    lt = S5_LANE_TILE

    @pl.when(pl.program_id(1) == 0)
    def _():
        for s in range(S5_CHUNK):
            w_ref[s * lt:(s + 1) * lt, :] = _expand_blockdiag(qc_ref[s], S5_GROUP, S5_STATE,
                                                              lt // S5_GROUP * S5_STATE, w_ref.shape[1])

    z = _dot(_s5_load_chunks(u_ref, kt), w_ref[...])
    w = z.shape[1] // 4
    for i, ref in enumerate((zfr_ref, zfi_ref, zbr_ref, zbi_ref)):
        ref[...] = z[:, i * w:(i + 1) * w]


def _s5_rec_kernel(*refs, bsz, nc_tiles, nl_tiles):
    zc, zl = refs[0:4], refs[4:8]
    pf_ref, pb_ref = refs[8:10]
    hc, hl = refs[10:14], refs[14:18]
    pw = ((pf_ref[0], pf_ref[1]), (pb_ref[0], pb_ref[1]))
    sub = lax.broadcasted_iota(jnp.int32, pw[0][0].shape, 0)

    def cmul(a, b):
        return a[0] * b[0] - a[1] * b[1], a[0] * b[1] + a[1] * b[0]

    def cadd(a, b):
        return a[0] + b[0], a[1] + b[1]

    def scan_tile(z, carry, d):
        p = pw[d]
        h = z
        for sh in (1, 2, 4):
            prow = sh - 1 if d == 0 else 8 - sh
            keep = (sub >= sh) if d == 0 else (sub < 8 - sh)
            shift = sh if d == 0 else 8 - sh
            moved = tuple(jnp.where(keep, pltpu.roll(v, shift, 0), 0.0) for v in h)
            h = cadd(h, cmul((p[0][prow:prow + 1, :], p[1][prow:prow + 1, :]), moved))
        full = cadd(h, cmul(p, carry))
        edge, shift, last = (0, 1, 7) if d == 0 else (7, 7, 0)
        before = tuple(jnp.where(sub == edge, c, pltpu.roll(v, shift, 0)) for v, c in zip(full, carry))
        return before, tuple(v[last:last + 1, :] for v in full)

    def walk(z_refs, h_refs, n_tiles, carries):
        def body(i, carries):
            out = []
            for b in range(bsz):
                for d in (0, 1):
                    t = i if d == 0 else n_tiles - 1 - i
                    r0 = pl.multiple_of((b * n_tiles + t) * 8, 8)
                    z = (z_refs[2 * d][pl.ds(r0, 8), :], z_refs[2 * d + 1][pl.ds(r0, 8), :])
                    before, new = scan_tile(z, carries[2 * b + d], d)
                    h_refs[2 * d][pl.ds(r0, 8), :] = before[0]
                    h_refs[2 * d + 1][pl.ds(r0, 8), :] = before[1]
                    out.append(new)
            return tuple(out)
        return lax.fori_loop(0, n_tiles, body, carries)

    zero = jnp.zeros((1, sub.shape[1]), F32)
    carries = walk(zc, hc, nc_tiles, tuple((zero, zero) for _ in range(2 * bsz)))
    walk(zl, hl, nl_tiles, carries)


def _s5_out_kernel(u_ref, hfr, hfi, hbr, hbi, bd_ref, rc_ref, y_ref, wm_ref, wr_ref):
    kt = hfr.shape[0]
    lt = S5_LANE_TILE

    @pl.when(pl.program_id(1) == 0)
    def _():
        for s in range(S5_CHUNK):
            for t in range(S5_CHUNK):
                wm_ref[s * lt:(s + 1) * lt, t * lt:(t + 1) * lt] = bd_ref[t - s + S5_CHUNK - 1]
        for i in range(4):
            wr_ref[i] = _expand_blockdiag(rc_ref[i], S5_STATE, S5_GROUP, lt, wr_ref.shape[2])

    y = _dot(_s5_load_chunks(u_ref, kt), wm_ref[...])
    for i, h_ref in enumerate((hfr, hfi, hbr, hbi)):
        y += _dot(h_ref[...].astype(BF16), wr_ref[i])
    w = y_ref.shape[1]
    for t in range(S5_CHUNK):
        y_ref[pl.ds(t, kt, stride=S5_CHUNK), :] = y[:, t * w:(t + 1) * w]


def _s5_matrices(lam_re, lam_im, log_step, b_re, b_im, c_re, c_im):
    hi = lax.Precision.HIGHEST
    L = S5_CHUNK
    step = jnp.exp(log_step)[..., None]
    n = jnp.arange(L + 1, dtype=F32)[:, None, None, None]
    mag = jnp.exp(n * (lam_re * step)[None])
    ang = n * (lam_im * step)[None]
    p_re, p_im = mag * jnp.cos(ang), mag * jnp.sin(ang)
    ab_re, ab_im = p_re[1], p_im[1]
    den = lam_re * lam_re + lam_im * lam_im
    coef_re = ((ab_re - 1.0) * lam_re + ab_im * lam_im) / den
    coef_im = (ab_im * lam_re - (ab_re - 1.0) * lam_im) / den
    e_re = p_re[:L] * coef_re - p_im[:L] * coef_im
    e_im = p_re[:L] * coef_im + p_im[:L] * coef_re
    ce_re = c_re[None] * e_re[..., None, :] - c_im[None] * e_im[..., None, :]
    ce_im = c_re[None] * e_im[..., None, :] + c_im[None] * e_re[..., None, :]
    kern = (jnp.einsum('ndgps,dgsq->ndgpq', ce_re, b_re, precision=hi)
            - jnp.einsum('ndgps,dgsq->ndgpq', ce_im, b_im, precision=hi))
    w_re, w_im = e_re[::-1], e_im[::-1]
    q_re = (w_re[..., None] * b_re[None] - w_im[..., None] * b_im[None]).transpose(1, 2, 0, 4, 3)
    q_im = (w_re[..., None] * b_im[None] + w_im[..., None] * b_re[None]).transpose(1, 2, 0, 4, 3)
    a_re, a_im = p_re[1:], p_im[1:]
    ca_re = c_re[None] * a_re[..., None, :] - c_im[None] * a_im[..., None, :]
    ca_im = c_re[None] * a_im[..., None, :] + c_im[None] * a_re[..., None, :]
    r_re = ca_re.transpose(1, 2, 4, 0, 3)
    r_im = (-ca_im).transpose(1, 2, 4, 0, 3)
    nl = L * jnp.arange(1, 9, dtype=F32)[:, None, None, None]
    t_mag = jnp.exp(nl * (lam_re * step)[None])
    t_ang = nl * (lam_im * step)[None]
    return kern, q_re, q_im, r_re, r_im, t_mag * jnp.cos(t_ang), t_mag * jnp.sin(t_ang)


def _s5_layer(u_lat, u_ctx, params, *, bsz, seq, ctx_len):
    c = u_lat.shape[1]
    L, pp, ns = S5_CHUNK, S5_GROUP, S5_STATE
    g = c // pp
    kern, q_re, q_im, r_re, r_im, t_re, t_im = _s5_matrices(*params)
    gl = S5_LANE_TILE // pp
    nb = g // gl
    lt = S5_LANE_TILE
    kw = L * lt
    sw = gl * ns
    lanes = g * ns
    kt_ = kern.transpose(0, 1, 2, 4, 3)
    lags = jnp.concatenate([kt_[:0:-1, 1], (kt_[0, 0] + kt_[0, 1])[None], kt_[1:, 0]], axis=0)
    eye = jnp.eye(gl, dtype=F32)
    bd = (lags.reshape(2 * L - 1, nb, gl, pp, pp).transpose(1, 0, 2, 3, 4)[:, :, :, :, None, :]
          * eye[None, None, :, None, :, None]).reshape(nb, 2 * L - 1, lt, lt).astype(BF16)
    qs = jnp.stack([q_re[0], q_im[0], q_re[1][:, ::-1], q_im[1][:, ::-1]])
    qc = qs.reshape(4, nb, gl, L, pp, ns).transpose(1, 3, 2, 4, 0, 5).reshape(nb, L, lt, 4 * ns).astype(BF16)
    rs = jnp.stack([r_re[0], r_im[0], r_re[1][:, :, ::-1], r_im[1][:, :, ::-1]])
    rc = rs.reshape(4, nb, gl, ns, L, pp).transpose(1, 0, 2, 3, 4, 5).reshape(nb, 4, sw, L * pp).astype(BF16)
    pf =jnp.stack([t_re[:, 0], t_im[:, 0]]).reshape(2, 8, lanes)
    pb = jnp.stack([t_re[::-1, 1], t_im[::-1, 1]]).reshape(2, 8, lanes)

    u_spec = lambda kt: pl.BlockSpec((kt * L, S5_LANE_TILE), lambda b, i: (i, b))
    h_spec = lambda kt: pl.BlockSpec((kt, sw), lambda b, i: (i, b))
    cp2 = _cparams(("parallel", "arbitrary"))

    def state(u):
        rows = u.shape[0] // L
        kt = min(S5_ROW_TILE, rows)
        return pl.pallas_call(
            _s5_state_kernel,
            out_shape=[jax.ShapeDtypeStruct((rows, lanes), F32)] * 4,
            grid=(nb, rows // kt),
            in_specs=[u_spec(kt), pl.BlockSpec((None, L, lt, 4 * ns), lambda b, i: (b, 0, 0, 0))],
            out_specs=[h_spec(kt)] * 4,
            scratch_shapes=[pltpu.VMEM((kw, 4 * sw), BF16)],
            compiler_params=cp2,
            name="s5_state",
        )(u, qc)

    z_c, z_l = state(u_ctx), state(u_lat)
    lb = S5_LANES
    rows_c, rows_l = z_c[0].shape[0], z_l[0].shape[0]
    h = pl.pallas_call(
        functools.partial(_s5_rec_kernel, bsz=bsz, nc_tiles=rows_c // bsz // 8, nl_tiles=rows_l // bsz // 8),
        out_shape=[jax.ShapeDtypeStruct((rows_c, lanes), F32)] * 4 + [jax.ShapeDtypeStruct((rows_l, lanes), F32)] * 4,
        grid=(lanes // lb,),
        in_specs=[pl.BlockSpec((rows_c, lb), lambda j: (0, j))] * 4
        + [pl.BlockSpec((rows_l, lb), lambda j: (0, j))] * 4
        + [pl.BlockSpec((2, 8, lb), lambda j: (0, 0, j))] * 2,
        out_specs=[pl.BlockSpec((rows_c, lb), lambda j: (0, j))] * 4
        + [pl.BlockSpec((rows_l, lb), lambda j: (0, j))] * 4,
        compiler_params=_cparams(("parallel",)),
        name="s5_recurrence",
    )(*z_c, *z_l, pf, pb)

    def readout(u, hs):
        rows = u.shape[0] // L
        kt = min(S5_ROW_TILE, rows)
        return pl.pallas_call(
            _s5_out_kernel,
            out_shape=jax.ShapeDtypeStruct(u.shape, F32),
            grid=(nb, rows // kt),
            in_specs=[u_spec(kt)] + [h_spec(kt)] * 4
            + [pl.BlockSpec((None, 2 * L - 1, lt, lt), lambda b, i: (b, 0, 0, 0)),
               pl.BlockSpec((None, 4, sw, L * pp), lambda b, i: (b, 0, 0, 0))],
            out_specs=u_spec(kt),
            scratch_shapes=[pltpu.VMEM((kw, kw), BF16), pltpu.VMEM((4, sw, kw), BF16)],
            compiler_params=cp2,
            name="s5_out",
        )(u, *hs, bd, rc)

    return readout(u_lat, h[4:]), readout(u_ctx, h[:4])


def _head_norm(o, g_ref, center, dh):
    outs = []
    for h in range(HEADS):
        oh = o[:, h * dh:(h + 1) * dh]
        if center:
            oh = oh - jnp.mean(oh, axis=-1, keepdims=True)
        outs.append(oh * lax.rsqrt(jnp.mean(oh * oh, axis=-1, keepdims=True) + EPS))
    return jnp.concatenate(outs, axis=-1) * g_ref[...]


def _finish(x_ref, a, b, wout_ref, mod_ref, gpost_ref, o_ref, n_r):
    half = a.shape[1]
    b = b.astype(BF16)
    if n_r:
        b = _dot(_perm_matrix(b.shape[0], n_r, False), b).astype(BF16)
    y = _dot(a.astype(BF16), wout_ref[0:half, :]) + _dot(b, wout_ref[half:2 * half, :])
    out = _tile2d(x_ref) + mod_ref[5:6, :] * _rms(y) * gpost_ref[1:2, :]
    o_ref[...] = out.reshape(o_ref.shape)


def _even_merge_kernel(x_ref, ga_ref, oaf_ref, oab_ref, gb_ref, obf_ref, obb_ref, na_ref, nb_ref,
                       wout_ref, mod_ref, gpost_ref, o_ref, *, dh, n_r):
    a = _silu(_tile2d(ga_ref)) * _head_norm(_tile2d(oaf_ref) + _tile2d(oab_ref), na_ref, False, dh)
    b = _sigmoid(_tile2d(gb_ref)) * _head_norm(_tile2d(obf_ref) + _tile2d(obb_ref), nb_ref, True, dh)
    _finish(x_ref, a, b, wout_ref, mod_ref, gpost_ref, o_ref, n_r)


def _odd_merge_kernel(x_ref, u_ref, ys_ref, gb_ref, obf_ref, obb_ref, sd_ref, wglu_ref, bglu_ref, nb_ref,
                      wout_ref, mod_ref, gpost_ref, o_ref, *, dh, n_r):
    y = _tile2d(ys_ref) + sd_ref[...] * _tile2d(u_ref)
    g = jax.nn.gelu(y)
    a = g * _sigmoid(_dot(g.astype(BF16), wglu_ref[...]) + bglu_ref[...])
    b = _silu(_tile2d(gb_ref)) * _head_norm(_tile2d(obf_ref) + _tile2d(obb_ref), nb_ref, True, dh)
    _finish(x_ref, a, b, wout_ref, mod_ref, gpost_ref, o_ref, n_r)


def _merge(kernel, s, rm_lat, cm_lat, rm_ctx, cm_ctx, consts, w_out, mod, gpost, *, bsz, seq, ctx_len, name):
    d = s.shape[1]
    half = d // 2
    rows = seq // GRID_W
    rt = min(MERGE_ROWS, rows)
    n_rh = rows // rt
    ncb = GRID_W // COL_TILE
    per_b = n_rh * ncb
    const_specs = [pl.BlockSpec(a.shape, lambda i, nd=a.ndim: (0,) * nd) for a in consts] \
        + [pl.BlockSpec((d, d), lambda i: (0, 0))]
    gpost_spec = pl.BlockSpec((3, d), lambda i: (0, 0))
    cp = _cparams(("parallel",))

    rm_idx = lambda i: ((i // per_b) * n_rh + (i % per_b) // ncb, i % ncb)
    cm_idx = lambda i: ((i // per_b) * ncb + i % ncb, (i % per_b) // ncb)
    s3 = s.reshape(s.shape[0] // GRID_W, GRID_W, d)
    ins = [s3] + [a.reshape(bsz * rows, GRID_W, a.shape[1]) for a, _ in rm_lat] \
        + [a.reshape(bsz * GRID_W, rows, a.shape[1]) for a, _ in cm_lat]
    in_specs = [pl.BlockSpec((rt, COL_TILE, d), lambda i: rm_idx(i) + (0,))] \
        + [pl.BlockSpec((rt, COL_TILE, half), lambda i, cb=cb: rm_idx(i) + (cb,)) for _, cb in rm_lat] \
        + [pl.BlockSpec((COL_TILE, rt, half), lambda i, cb=cb: cm_idx(i) + (cb,)) for _, cb in cm_lat]
    n_out_groups = s3.shape[0] if rm_ctx is not None else bsz * rows
    out = pl.pallas_call(
        functools.partial(kernel, n_r=rt),
        out_shape=jax.ShapeDtypeStruct((n_out_groups, GRID_W, d), F32),
        grid=(bsz * per_b,),
        in_specs=in_specs + const_specs
        + [pl.BlockSpec((None, N_MOD, d), lambda i: (i // per_b, 0, 0)), gpost_spec],
        out_specs=pl.BlockSpec((rt, COL_TILE, d), lambda i: rm_idx(i) + (0,)),
        compiler_params=cp,
        name=name + "_lat",
    )(*ins, *consts, w_out, mod, gpost)
    out = out.reshape(n_out_groups * GRID_W, d)
    if rm_ctx is None:
        return out

    tm = rt * COL_TILE
    t0 = bsz * seq // tm
    tiles = list(rm_ctx) + list(cm_ctx)
    n_in = 1 + len(tiles) + len(consts) + 3
    inner = functools.partial(kernel, n_r=0)
    return pl.pallas_call(
        lambda *refs: inner(*refs[:n_in], refs[n_in + 1]),
        out_shape=jax.ShapeDtypeStruct(out.shape, F32),
        grid=(bsz * ctx_len // tm,),
        in_specs=[pl.BlockSpec((tm, d), lambda i: (t0 + i, 0))]
        + [pl.BlockSpec((tm, half), lambda i, cb=cb: (i, cb)) for _, cb in tiles] + const_specs
        + [pl.BlockSpec((None, N_MOD, d), lambda i: (bsz, 0, 0)), gpost_spec,
           pl.BlockSpec(memory_space=pl.ANY)],
        out_specs=pl.BlockSpec((tm, d), lambda i: (t0 + i, 0)),
        input_output_aliases={n_in: 0},
        compiler_params=cp,
        name=name + "_ctx",
    )(s, *[a for a, _ in tiles], *consts, w_out, mod, gpost, out)


def kernel(x, c, ctx, c_ctx, w_mod, b_mod, norm_pre, norm_post, ffn_w_gate, ffn_w_up, ffn_w_down, ev_w_in, ev_w_out, gla_w_gate, gla_b_gate, gla_norm, ml_conv_w, ml_conv_b, ml_b_gates, ml_norm, od_w_in, od_w_out, s5_lam_re, s5_lam_im, s5_log_step, s5_b_re, s5_b_im, s5_c_re, s5_c_im, s5_d, s5_w_glu, s5_b_glu, ret_log_decay, ret_norm):
    bsz, seq, d = x.shape
    ctx_len = ctx.shape[1]
    depth = w_mod.shape[0]
    half = d // 2
    dh = half // HEADS
    gla_dk = dh // 2
    d_ff = ffn_w_gate.shape[-1]
    tm = TOKEN_TILE
    rows = seq // GRID_W
    assert seq % tm == 0 and (bsz * ctx_len) % tm == 0 and seq == rows * GRID_W and ctx_len % CHUNK == 0
    assert 2 * bsz == 8 and rows % 8 == 0 and seq % CHUNK == 0 and half == PROJ_TILE
    assert ctx_len % (8 * S5_CHUNK) == 0 and seq % (8 * S5_CHUNK) == 0 and bsz % SCAN_BATCH == 0
    assert all((n // S5_CHUNK) % min(S5_ROW_TILE, n // S5_CHUNK) == 0 for n in (bsz * ctx_len, bsz * seq))

    tiles_x = bsz * seq // tm
    tiles_all = tiles_x + bsz * ctx_len // tm
    tiles_per_batch = seq // tm
    mod_of_tile = lambda i: jnp.minimum(i // tiles_per_batch, bsz)

    wg, wu, wd = ffn_w_gate.astype(BF16), ffn_w_up.astype(BF16), ffn_w_down.astype(BF16)

    cc = jnp.concatenate([c, c_ctx[None], jnp.zeros((8 - bsz - 1, d), F32)], axis=0)
    mod_all = _modulation(cc, w_mod, b_mod).reshape(depth, 8, N_MOD, d)

    s = None
    dims = dict(bsz=bsz, seq=seq, ctx_len=ctx_len)

    for layer in range(depth):
        need_ctx = layer < depth - 1
        mod = mod_all[layer]
        gpre, gpost = norm_pre[layer], norm_post[layer]
        ffn = functools.partial(_ffn, mod=mod, gpre=gpre, gpost=gpost, wg=wg, wu=wu, wd=wd, layer=layer,
                                mod_of_tile=mod_of_tile)
        if s is None:
            s = ffn(x.reshape(bsz * seq, d), which=0, sub=0, n_tiles=tiles_x, out_tiles=tiles_all)
            s = ffn(ctx.reshape(bsz * ctx_len, d), which=0, sub=0, n_tiles=tiles_all - tiles_x, out_tile0=tiles_x,
                    out_tiles=tiles_all, prev=s)
        else:
            s = ffn(s, which=0, sub=0, n_tiles=tiles_all)

        if layer % 2 == 0:
            e = layer // 2
            w = ev_w_in[e]
            qk_w = 2 * HEADS * gla_dk
            n_gate = 4 * HEADS
            c_low = qk_w + 2 * half
            c_ml = c_low + 2 * GLA_RANK
            c_g = c_ml + 4 * half
            w_in = jnp.concatenate([w[:, :c_low], w[:, c_ml:c_g]], axis=1).astype(BF16)
            w_small = jnp.concatenate([w[:, c_low:c_ml], jnp.zeros((d, SMALL_W - 2 * GLA_RANK), F32),
                                       w[:, c_g:], jnp.zeros((d, SMALL_W - n_gate), F32)], axis=1).astype(BF16)
            (pa, pb, psa, psb), (pa_c, pb_c, psa_c, psb_c) = _proj(s, mod, gpre, w_in, w_small,
                                                                   n_a=c_low // PROJ_TILE, **dims)

            st0 = [jnp.zeros((bsz, 2, HEADS, dh, gla_dk), F32)]
            gla_ins = lambda p, ps: [(p, qk_w // 2, 0), (p, qk_w // 2, qk_w // 2), (p, half, qk_w), (ps, SMALL_W, 0)]
            oa, oa_c = _bidir_scan(
                functools.partial(_gla_kernel, L=CHUNK, dk=gla_dk, dv=dh, gated=True,
                                  q_scale=float(gla_dk) ** -0.5, k_scale=1.0),
                gla_ins(pa, psa), gla_ins(pa_c, psa_c), [gla_w_gate[e].astype(BF16), gla_b_gate[e]], st0,
                out_width=half, name="gla", **dims)

            scale = jnp.concatenate([jnp.ones((half,), F32), jnp.full((half,), float(dh) ** -0.5, F32)])
            qk = _conv(pb, ml_conv_w[e], ml_conv_b[e], scale, bsz=bsz, n=seq)
            qk_c = _conv(pb_c, ml_conv_w[e], ml_conv_b[e], scale, bsz=bsz, n=ctx_len)
            bias = jnp.zeros((1, SMALL_W), F32).at[0, :n_gate].set(ml_b_gates[e].reshape(-1))
            ml_ins = lambda qk_, p, ps: [(qk_, half, 0), (qk_, half, half), (p, half, 2 * half), (ps, SMALL_W, 0)]
            st0 = [jnp.zeros((bsz, 2, HEADS, dh, dh), F32), jnp.zeros((bsz, 2, HEADS, 8, dh), F32)]
            ob, ob_c = _bidir_scan(
                functools.partial(_mlstm_kernel, L=CHUNK, dh=dh),
                ml_ins(qk, pb, psb), ml_ins(qk_c, pb_c, psb_c), [bias], st0,
                out_width=half, name="mlstm", out_dtype=BF16, **dims)

            ga_cb, gb_cb = (qk_w + half) // half, 3
            s = _merge(
                functools.partial(_even_merge_kernel, dh=dh), s,
                [(pa, ga_cb), (oa[0], 0), (oa[1], 0)], [(pb, gb_cb), (ob[0], 0), (ob[1], 0)],
                [(pa_c, ga_cb), (oa_c[0], 0), (oa_c[1], 0)] if need_ctx else None,
                [(pb_c, gb_cb), (ob_c[0], 0), (ob_c[1], 0)],
                [gla_norm[e].reshape(1, half), ml_norm[e].reshape(1, half)],
                ev_w_out[e].astype(BF16), mod, gpost, name="even_merge", **dims)
        else:
            o = layer // 2
            (pa, pb), (pa_c, pb_c) = _proj(s, mod, gpre, od_w_in[o].astype(BF16), None, n_a=1, **dims)

            ys, ys_c = _s5_layer(pa, pa_c, (s5_lam_re[o], s5_lam_im[o], s5_log_step[o], s5_b_re[o], s5_b_im[o],
                                            s5_c_re[o], s5_c_im[o]), **dims)

            lg = jnp.repeat(-jnp.exp(ret_log_decay[o]), dh, axis=1)
            ret_ins = lambda p: [(p, half, 0), (p, half, half), (p, half, 2 * half)]
            st0 = [jnp.zeros((bsz, 2, HEADS, dh, dh), F32)]
            od, od_c = _bidir_scan(
                functools.partial(_gla_kernel, L=CHUNK, dk=dh, dv=dh, gated=False,
                                  q_scale=1.0, k_scale=float(dh) ** -0.5),
                ret_ins(pb), ret_ins(pb_c), [lg], st0, out_width=half, name="retention", out_dtype=BF16, **dims)

            s = _merge(
                functools.partial(_odd_merge_kernel, dh=dh), s,
                [(pa, 0), (ys, 0)], [(pb, 3), (od[0], 0), (od[1], 0)],
                [(pa_c, 0), (ys_c, 0)] if need_ctx else None, [(pb_c, 3), (od_c[0], 0), (od_c[1], 0)],
                [s5_d[o].reshape(1, half), s5_w_glu[o].astype(BF16), s5_b_glu[o].reshape(1, half),
                 ret_norm[o].reshape(1, half)],
                od_w_out[o].astype(BF16), mod, gpost, name="odd_merge", **dims)

        s = ffn(s, which=1, sub=2, n_tiles=tiles_all if need_ctx else tiles_x)

    return s[:bsz * seq].reshape(bsz, seq, d)
```

```python
import functools

import jax
import jax.numpy as jnp
from jax import lax
from jax.experimental import pallas as pl
from jax.experimental.pallas import tpu as pltpu

F32 = jnp.float32
BF16 = jnp.bfloat16

EPS = 1e-6
GRID_W = 64
COL_TILE = 8
CHUNK = 64
WIDE_CHUNK = 128
SCAN_BATCH = 2
FFN_RES = 0.5
N_MOD = 9
HEADS = 4
GLA_RANK = 16
GLA_GATE_NORM = 16.0
S5_GROUP = 16
S5_STATE = 64
S5_CHUNK = 8
S5_LANE_TILE = 128
S5_ROW_TILE = 512
SMALL_W = 128

TOKEN_TILE = 512
MERGE_ROWS = 32
FF_TILE = 1024
PROJ_TILE = 1024
MOD_TILE = 1024
CONV_TILE = 256
CONV_ROWS = 256
S5_LANES = 256
VMEM_LIMIT = 56 * 1024 * 1024


def _cparams(sem):
    return pltpu.CompilerParams(dimension_semantics=sem, vmem_limit_bytes=VMEM_LIMIT)


def _rms(x):
    return x * lax.rsqrt(jnp.mean(x * x, axis=-1, keepdims=True) + EPS)


def _sigmoid(x):
    return 1.0 / (1.0 + jnp.exp(-x))


def _silu(x):
    return x * _sigmoid(x)


def _log_sigmoid(x):
    return jnp.minimum(x, 0.0) - jnp.log(1.0 + jnp.exp(-jnp.abs(x)))


def _dot(a, b):
    return jnp.dot(a, b, preferred_element_type=F32)


def _dot_nt(a, b):
    return lax.dot_general(a, b, (((1,), (1,)), ((), ())), preferred_element_type=F32)


def _dot_tn(a, b):
    return lax.dot_general(a, b, (((0,), (0,)), ((), ())), preferred_element_type=F32)


def _dot01(t01, x):
    x0 = x.astype(BF16)
    r1 = x - x0.astype(F32)
    x1 = r1.astype(BF16)
    x2 = (r1 - x1.astype(F32)).astype(BF16)
    return _dot(t01, x0) + _dot(t01, x1) + _dot(t01, x2)


def _perm_matrix(n, n_r, to_col_major):
    i = lax.broadcasted_iota(jnp.int32, (n, n), 0)
    c = lax.broadcasted_iota(jnp.int32, (n, n), 1)
    if to_col_major:
        hit = (i // n_r == c % COL_TILE) & (i % n_r == c // COL_TILE)
    else:
        hit = (i // COL_TILE == c % n_r) & (i % COL_TILE == c // n_r)
    return jnp.where(hit, 1.0, 0.0).astype(BF16)


def _tile2d(ref):
    v = ref[...]
    return v.reshape(-1, v.shape[-1]).astype(F32)


def _mod_kernel(c_ref, w_ref, b_ref, o_ref):
    a = _silu(c_ref[...]).astype(BF16)
    o_ref[...] = _dot(a, w_ref[...].astype(BF16)) + b_ref[...]


def _modulation(cc, w_mod, b_mod):
    depth, d, nm = w_mod.shape
    return pl.pallas_call(
        _mod_kernel,
        out_shape=jax.ShapeDtypeStruct((depth, 8, nm), F32),
        grid=(depth, nm // MOD_TILE),
        in_specs=[pl.BlockSpec((8, d), lambda l, j: (0, 0)),
                  pl.BlockSpec((None, d, MOD_TILE), lambda l, j: (l, 0, j)),
                  pl.BlockSpec((None, 1, MOD_TILE), lambda l, j: (l, 0, j))],
        out_specs=pl.BlockSpec((None, 8, MOD_TILE), lambda l, j: (l, 0, j)),
        compiler_params=_cparams(("parallel", "parallel")),
        name="modulation",
    )(cc, w_mod, b_mod.reshape(depth, 1, nm))


def _ffn_kernel(x_ref, mod_ref, gpre_ref, gpost_ref, wg_ref, wu_ref, wd_ref, o_ref, h_ref, acc_ref, *, sub, nff,
                tail):
    j = pl.program_id(1)

    @pl.when(j == 0)
    def _():
        shift = mod_ref[3 * sub:3 * sub + 1, :]
        scale = mod_ref[3 * sub + 1:3 * sub + 2, :]
        h = _rms(x_ref[...]) * gpre_ref[sub:sub + 1, :] * (1.0 + scale) + shift
        h_ref[...] = h.astype(BF16)

    def hidden_step(valid, first):
        h = h_ref[...]
        t = _silu(_dot(h, wg_ref[:, :valid])) * _dot(h, wu_ref[:, :valid])
        y = _dot(t.astype(BF16), wd_ref[:valid, :])
        if first:
            acc_ref[...] = y
        else:
            acc_ref[...] += y

    pl.when(j == 0)(lambda: hidden_step(tail or FF_TILE, True))
    pl.when(j > 0)(lambda: hidden_step(FF_TILE, False))

    @pl.when(j == nff - 1)
    def _():
        gate = mod_ref[3 * sub + 2:3 * sub + 3, :]
        o_ref[...] = x_ref[...] + FFN_RES * gate * _rms(acc_ref[...]) * gpost_ref[sub:sub + 1, :]


def _ffn(s, mod, gpre, gpost, wg, wu, wd, *, layer, which, sub, n_tiles, mod_of_tile, out_tile0=0, out_tiles=None,
         prev=None):
    d = s.shape[1]
    d_ff = wg.shape[-1]
    nff = pl.cdiv(d_ff, FF_TILE)
    assert nff >= 2 and (d_ff % FF_TILE) % 128 == 0
    tm = TOKEN_TILE
    out_tiles = n_tiles if out_tiles is None else out_tiles
    kernel = functools.partial(_ffn_kernel, sub=sub, nff=nff, tail=d_ff % FF_TILE)
    blk = lambda j: (j + nff - 1) % nff
    ins = [s, mod, gpre, gpost, wg, wu, wd]
    in_specs = [pl.BlockSpec((tm, d), lambda i, j: (i, 0)),
                pl.BlockSpec((None, N_MOD, d), lambda i, j: (mod_of_tile(out_tile0 + i), 0, 0)),
                pl.BlockSpec((3, d), lambda i, j: (0, 0)),
                pl.BlockSpec((3, d), lambda i, j: (0, 0)),
                pl.BlockSpec((None, None, d, FF_TILE), lambda i, j: (layer, which, 0, blk(j))),
                pl.BlockSpec((None, None, d, FF_TILE), lambda i, j: (layer, which, 0, blk(j))),
                pl.BlockSpec((None, None, FF_TILE, d), lambda i, j: (layer, which, blk(j), 0))]
    aliases = {}
    if prev is not None:
        n_in = len(ins)
        inner = kernel
        kernel = lambda *refs: inner(*refs[:n_in], *refs[n_in + 1:])
        ins.append(prev)
        in_specs.append(pl.BlockSpec(memory_space=pl.ANY))
        aliases = {n_in: 0}
    return pl.pallas_call(
        kernel,
        out_shape=jax.ShapeDtypeStruct((out_tiles * tm, d), F32),
        grid=(n_tiles, nff),
        in_specs=in_specs,
        out_specs=pl.BlockSpec((tm, d), lambda i, j: (out_tile0 + i, 0)),
        scratch_shapes=[pltpu.VMEM((tm, d), BF16), pltpu.VMEM((tm, d), F32)],
        input_output_aliases=aliases,
        compiler_params=_cparams(("parallel", "arbitrary")),
        name="ffn",
    )(*ins)


def _proj_kernel(*refs, n_a, n_r, has_small):
    if has_small:
        x_ref, mod_ref, gpre_ref, w_ref, ws_ref, pa_ref, pb_ref, psa_ref, psb_ref, h_ref, hp_ref = refs
    else:
        x_ref, mod_ref, gpre_ref, w_ref, pa_ref, pb_ref, h_ref, hp_ref = refs
    j = pl.program_id(1)
    tm = h_ref.shape[0]

    @pl.when(j == 0)
    def _():
        x = _tile2d(x_ref)
        h = (_rms(x) * gpre_ref[1:2, :] * (1.0 + mod_ref[4:5, :]) + mod_ref[3:4, :]).astype(BF16)
        hp = _dot(_perm_matrix(tm, n_r, True), h).astype(BF16) if n_r else h
        h_ref[...] = h
        hp_ref[...] = hp
        if has_small:
            psa_ref[...] = _dot(h, ws_ref[:, :SMALL_W]).reshape(psa_ref.shape)
            psb_ref[...] = _dot(hp, ws_ref[:, SMALL_W:])

    @pl.when(j < n_a)
    def _():
        pa_ref[...] = _dot(h_ref[...], w_ref[j]).reshape(pa_ref.shape)

    @pl.when(j >= n_a)
    def _():
        pb_ref[...] = _dot(hp_ref[...], w_ref[j])


def _proj(s, mod, gpre, w, w_small, *, n_a, bsz, seq, ctx_len):
    d = s.shape[1]
    rows = seq // GRID_W
    tn = PROJ_TILE
    n_tiles_n = w.shape[1] // tn
    n_b = n_tiles_n - n_a
    has_small = w_small is not None
    w_ins = [w.reshape(d, n_tiles_n, tn).transpose(1, 0, 2)] + ([w_small] if has_small else [])
    w_specs = [pl.BlockSpec((n_tiles_n, d, tn), lambda i, j: (0, 0, 0), pipeline_mode=pl.Buffered(1))]
    if has_small:
        w_specs.append(pl.BlockSpec((d, 2 * SMALL_W), lambda i, j: (0, 0)))
    ja = lambda j: jnp.minimum(j, n_a - 1)
    jb = lambda j: jnp.maximum(j - n_a, 0)
    cp = _cparams(("parallel", "arbitrary"))

    tm = rows * COL_TILE
    ncb = GRID_W // COL_TILE
    s3 = s.reshape(s.shape[0] // GRID_W, GRID_W, d)
    out_shape = [jax.ShapeDtypeStruct((bsz * rows, GRID_W, n_a * tn), F32),
                 jax.ShapeDtypeStruct((bsz * seq, n_b * tn), F32)]
    out_specs = [pl.BlockSpec((rows, COL_TILE, tn), lambda i, j: (i // ncb, i % ncb, ja(j))),
                 pl.BlockSpec((tm, tn), lambda i, j: (i, jb(j)))]
    if has_small:
        out_shape += [jax.ShapeDtypeStruct((bsz * rows, GRID_W, SMALL_W), F32),
                      jax.ShapeDtypeStruct((bsz * seq, SMALL_W), F32)]
        out_specs += [pl.BlockSpec((rows, COL_TILE, SMALL_W), lambda i, j: (i // ncb, i % ncb, 0)),
                      pl.BlockSpec((tm, SMALL_W), lambda i, j: (i, 0))]
    lat = pl.pallas_call(
        functools.partial(_proj_kernel, n_a=n_a, n_r=rows, has_small=has_small),
        out_shape=out_shape,
        grid=(bsz * ncb, n_tiles_n),
        in_specs=[pl.BlockSpec((rows, COL_TILE, d), lambda i, j: (i // ncb, i % ncb, 0)),
                  pl.BlockSpec((None, N_MOD, d), lambda i, j: (i // ncb, 0, 0)),
                  pl.BlockSpec((3, d), lambda i, j: (0, 0))] + w_specs,
        out_specs=out_specs,
        scratch_shapes=[pltpu.VMEM((tm, d), BF16), pltpu.VMEM((tm, d), BF16)],
        compiler_params=cp,
        name="in_proj_lat",
    )(s3, mod, gpre, *w_ins)
    lat = [lat[0].reshape(bsz * seq, n_a * tn), lat[1]] \
        + ([lat[2].reshape(bsz * seq, SMALL_W), lat[3]] if has_small else [])

    tc = TOKEN_TILE
    n_c = bsz * ctx_len
    t0 = bsz * seq // tc
    out_shape = [jax.ShapeDtypeStruct((n_c, n_a * tn), F32), jax.ShapeDtypeStruct((n_c, n_b * tn), F32)]
    out_specs = [pl.BlockSpec((tc, tn), lambda i, j: (i, ja(j))), pl.BlockSpec((tc, tn), lambda i, j: (i, jb(j)))]
    if has_small:
        out_shape += [jax.ShapeDtypeStruct((n_c, SMALL_W), F32)] * 2
        out_specs += [pl.BlockSpec((tc, SMALL_W), lambda i, j: (i, 0))] * 2
    ctx = pl.pallas_call(
        functools.partial(_proj_kernel, n_a=n_a, n_r=0, has_small=has_small),
        out_shape=out_shape,
        grid=(n_c // tc, n_tiles_n),
        in_specs=[pl.BlockSpec((tc, d), lambda i, j: (t0 + i, 0)),
                  pl.BlockSpec((None, N_MOD, d), lambda i, j: (bsz, 0, 0)),
                  pl.BlockSpec((3, d), lambda i, j: (0, 0))] + w_specs,
        out_specs=out_specs,
        scratch_shapes=[pltpu.VMEM((tc, d), BF16), pltpu.VMEM((tc, d), BF16)],
        compiler_params=cp,
        name="in_proj_ctx",
    )(s, mod, gpre, *w_ins)
    return lat, list(ctx)


def _conv_kernel(x_ref, w_ref, b_ref, sc_ref, o_ref):
    n, ct = x_ref.shape
    ch = CONV_ROWS
    n_ch = n // ch
    r = lax.broadcasted_iota(jnp.int32, (ch, ct), 0)

    def body(c, carry):
        r0 = pl.multiple_of(c * ch, ch)
        x = x_ref[pl.ds(r0, ch), :]
        before = x_ref[pl.ds(pl.multiple_of(jnp.maximum(r0 - 8, 0), 8), 8), :][7:8, :]
        after = x_ref[pl.ds(pl.multiple_of(jnp.minimum(r0 + ch, n - 8), 8), 8), :][0:1, :]
        before = jnp.where(c == 0, 0.0, before)
        after = jnp.where(c == n_ch - 1, 0.0, after)
        prev = jnp.where(r == 0, before, pltpu.roll(x, 1, 0))
        nxt = jnp.where(r == ch - 1, after, pltpu.roll(x, ch - 1, 0))
        y = w_ref[1:2, :] * x + w_ref[0:1, :] * prev + w_ref[2:3, :] * nxt + b_ref[...]
        o_ref[pl.ds(r0, ch), :] = _silu(y) * sc_ref[...]
        return carry

    lax.fori_loop(0, n_ch, body, 0)


def _conv(p, conv_w, conv_b, scale, *, bsz, n):
    c = conv_w.shape[1]
    ct = CONV_TILE
    return pl.pallas_call(
        _conv_kernel,
        out_shape=jax.ShapeDtypeStruct((bsz * n, c), F32),
        grid=(bsz, c // ct),
        in_specs=[pl.BlockSpec((n, ct), lambda b, j: (b, j)),
                  pl.BlockSpec((3, ct), lambda b, j: (0, j)),
                  pl.BlockSpec((1, ct), lambda b, j: (0, j)),
                  pl.BlockSpec((1, ct), lambda b, j: (0, j))],
        out_specs=pl.BlockSpec((n, ct), lambda b, j: (b, j)),
        compiler_params=_cparams(("parallel", "parallel")),
        name="conv",
    )(p, conv_w, conv_b.reshape(1, c), scale.reshape(1, c))


def _gla_kernel(*refs, L, dk, dv, gated, q_scale, k_scale):
    if gated:
        qf, kf, vf, df, qb, kb, vb, db, wgate_ref, bgate_ref, st_in, of, ob, st_out = refs
    else:
        qf, kf, vf, qb, kb, vb, lg_ref, st_in, of, ob, st_out = refs
        df = db = None

    @pl.when(pl.program_id(1) == 0)
    def _():
        st_out[...] = st_in[...]

    row = lax.broadcasted_iota(jnp.int32, (L, L), 0)
    col = lax.broadcasted_iota(jnp.int32, (L, L), 1)
    trow = lax.broadcasted_iota(jnp.int32, (L, 1), 0)
    for d, (q_ref, k_ref, v_ref, d_ref, o_ref) in enumerate(((qf, kf, vf, df, of), (qb, kb, vb, db, ob))):
        mask = (row >= col) if d == 0 else (row <= col)
        mid = L // 2 if d == 0 else L - 1 - L // 2
        end = L - 1 if d == 0 else 0
        if gated:
            low = d_ref[:, d * GLA_RANK:(d + 1) * GLA_RANK].astype(BF16)
            z = _dot(low, wgate_ref[d]) + bgate_ref[d:d + 1, :]
            la = _log_sigmoid(z) * (1.0 / GLA_GATE_NORM)
            bcum = _dot01(mask.astype(BF16), la)
        else:
            cnt = (trow + 1) if d == 0 else (L - trow)
            bcum = cnt.astype(F32) * lg_ref[d:d + 1, :]
            dist = jnp.abs(row - col).astype(F32)
        for h in range(HEADS):
            b = bcum[:, h * dk:(h + 1) * dk]
            b_end = b[end:end + 1, :]
            q = q_ref[:, h * dk:(h + 1) * dk] * q_scale
            k = k_ref[:, h * dk:(h + 1) * dk] * k_scale
            v = v_ref[:, h * dv:(h + 1) * dv].astype(BF16)
            if gated:
                b_mid = b[mid:mid + 1, :]
                qs = (q * jnp.exp(b - b_mid)).astype(BF16)
                ks = (k * jnp.exp(b_mid - b)).astype(BF16)
                scores = jnp.where(mask, _dot_nt(qs, ks), 0.0)
            else:
                lg_h = lg_ref[d:d + 1, h * dk:h * dk + 1]
                scores = _dot_nt(q.astype(BF16), k.astype(BF16)) * jnp.where(mask, jnp.exp(dist * lg_h), 0.0)
            st = st_out[d, h]
            o = _dot(scores.astype(BF16), v) + _dot_nt((q * jnp.exp(b)).astype(BF16), st.astype(BF16))
            o_ref[:, h * dv:(h + 1) * dv] = o.astype(o_ref.dtype)
            kd = (k * jnp.exp(b_end - b)).astype(BF16)
            st_out[d, h] = jnp.exp(b_end) * st + _dot_tn(v, kd)


def _mlstm_kernel(qf, kf, vf, gf, qb, kb, vb, gb, bias_ref, c_in, nm_in, of, ob, c_out, nm_out, *, L, dh):
    @pl.when(pl.program_id(1) == 0)
    def _():
        c_out[...] = c_in[...]
        nm_out[...] = nm_in[...]

    row = lax.broadcasted_iota(jnp.int32, (L, L), 0)
    col = lax.broadcasted_iota(jnp.int32, (L, L), 1)
    eye = row == col
    ones = jnp.ones((L, L), BF16)
    for d, (q_ref, k_ref, v_ref, g_ref, o_ref) in enumerate(((qf, kf, vf, gf, of), (qb, kb, vb, gb, ob))):
        mask = (row >= col) if d == 0 else (row <= col)
        cum_us = (row <= col) if d == 0 else (row >= col)
        end = L - 1 if d == 0 else 0
        g = g_ref[...] + bias_ref[...]
        fl = _log_sigmoid(g)
        bc_all = _dot01(mask.astype(BF16), fl)
        for h in range(HEADS):
            ci = d * 2 * HEADS + h
            cf = ci + HEADS
            i_col = g[:, ci:ci + 1]
            f_col = fl[:, cf:cf + 1]
            b_col = bc_all[:, cf:cf + 1]
            x = jnp.where(cum_us, f_col, 0.0) - jnp.where(eye, i_col, 0.0)
            drow = _dot01(ones, x)
            m_st = nm_out[d, h, 1:2, 0:1]
            dmat = jnp.where(mask, b_col - drow, -jnp.inf)
            inter = b_col + m_st
            m_t = jnp.maximum(inter, jnp.max(dmat, axis=-1, keepdims=True))
            q = q_ref[:, h * dh:(h + 1) * dh]
            k = k_ref[:, h * dh:(h + 1) * dh]
            v = v_ref[:, h * dh:(h + 1) * dh].astype(BF16)
            qh = q.astype(BF16)
            scores = _dot_nt(qh, k.astype(BF16)) * jnp.exp(dmat - m_t)
            w_inter = jnp.exp(inter - m_t)
            c_st = c_out[d, h]
            n_st = nm_out[d, h, 0:1, :]
            num = _dot(scores.astype(BF16), v) + w_inter * _dot(qh, c_st.astype(BF16))
            den = (jnp.sum(scores, axis=-1, keepdims=True)
                   + w_inter * jnp.sum(q * n_st, axis=-1, keepdims=True))
            hout = num / jnp.maximum(jnp.abs(den), jnp.exp(-m_t))
            o_ref[:, h * dh:(h + 1) * dh] = hout.astype(o_ref.dtype)
            b_end = b_col[end:end + 1, :]
            d_end = b_end - b_col + i_col
            m_new = jnp.maximum(b_end + m_st, jnp.max(d_end, axis=0, keepdims=True))
            kw = k * jnp.exp(d_end - m_new)
            decay = jnp.exp(b_end + m_st - m_new)
            c_out[d, h] = decay * c_st + _dot_tn(kw.astype(BF16), v)
            nm_out[d, h, 0:1, :] = decay * n_st + jnp.sum(kw, axis=0, keepdims=True)
            nm_out[d, h, 1:2, :] = jnp.broadcast_to(m_new, (1, dh))


def _scan_call(kernel, bsz, n, L, chunk_ins, consts, states, out_width, name, out_dtype=F32):
    bg = SCAN_BATCH
    chunk = (lambda s: s, lambda s: n - 1 - s)
    ins, in_specs, batched = [], [], []
    for d in (0, 1):
        for a, width, coloff in chunk_ins:
            ins.append(a.reshape(bsz, n * L, a.shape[1]))
            in_specs.append(pl.BlockSpec((bg, L, width), lambda b, s, d=d, cb=coloff // width: (b, chunk[d](s), cb)))
            batched.append(True)
    for a in consts:
        ins.append(a)
        in_specs.append(pl.BlockSpec(a.shape, lambda b, s, nd=a.ndim: (0,) * nd))
        batched.append(False)
    state_specs = [pl.BlockSpec((bg,) + a.shape[1:], lambda b, s, nd=a.ndim: (b,) + (0,) * (nd - 1))
                   for a in states]
    ins += list(states)
    in_specs += state_specs
    batched += [True] * (len(states) + 2 + len(states))

    def grouped(*refs):
        for bi in range(bg):
            kernel(*[r.at[bi] if is_b else r for r, is_b in zip(refs, batched)])

    out_struct = jax.ShapeDtypeStruct((bsz, n * L, out_width), out_dtype)
    res = pl.pallas_call(
        grouped,
        out_shape=[out_struct, out_struct] + [jax.ShapeDtypeStruct(a.shape, F32) for a in states],
        grid=(bsz // bg, n),
        in_specs=in_specs,
        out_specs=[pl.BlockSpec((bg, L, out_width), lambda b, s, d=d: (b, chunk[d](s), 0)) for d in (0, 1)]
        + state_specs,
        compiler_params=_cparams(("parallel", "arbitrary")),
        name=name,
    )(*ins)
    return (res[0].reshape(bsz * n * L, out_width), res[1].reshape(bsz * n * L, out_width)), list(res[2:])


def _bidir_scan(make_kernel, chunk, ins_lat, ins_ctx, consts, states0, *, bsz, seq, ctx_len, out_width, name,
                out_dtype=F32):
    kernel = functools.partial(make_kernel, L=chunk)
    o_ctx, st = _scan_call(kernel, bsz, ctx_len // chunk, chunk, ins_ctx, consts, states0, out_width,
                           name + "_ctx", out_dtype)
    o_lat, _ = _scan_call(kernel, bsz, seq // chunk, chunk, ins_lat, consts, st, out_width, name + "_lat",
                          out_dtype)
    return o_lat, o_ctx


def _s5_load_chunks(u_ref, kt):
    return jnp.concatenate([u_ref[pl.ds(t, kt, stride=S5_CHUNK), :].astype(BF16) for t in range(S5_CHUNK)],
                           axis=-1)


def _expand_blockdiag(c, rows_per_group, lanes_per_group, lane_period, n_out):
    r_n, k = c.shape
    kk = lax.broadcasted_iota(jnp.int32, (k, n_out), 0)
    jj = lax.broadcasted_iota(jnp.int32, (k, n_out), 1)
    sel = (kk // lanes_per_group == jj // lane_period) & (kk % lanes_per_group == jj % lanes_per_group)
    rr = lax.broadcasted_iota(jnp.int32, (r_n, n_out), 0)
    cc = lax.broadcasted_iota(jnp.int32, (r_n, n_out), 1)
    same = rr // rows_per_group == (cc % lane_period) // lanes_per_group
    return jnp.where(same, _dot(c, jnp.where(sel, 1.0, 0.0).astype(BF16)), 0.0).astype(BF16)


def _s5_state_kernel(u_ref, qc_ref, zfr_ref, zfi_ref, zbr_ref, zbi_ref, w_ref):
    kt = zfr_ref.shape[0]
    lt = S5_LANE_TILE

    @pl.when(pl.program_id(1) == 0)
    def _():
        for s in range(S5_CHUNK):
            w_ref[s * lt:(s + 1) * lt, :] = _expand_blockdiag(qc_ref[s], S5_GROUP, S5_STATE,
                                                              lt // S5_GROUP * S5_STATE, w_ref.shape[1])

    z = _dot(_s5_load_chunks(u_ref, kt), w_ref[...])
    w = z.shape[1] // 4
    for i, ref in enumerate((zfr_ref, zfi_ref, zbr_ref, zbi_ref)):
        ref[...] = z[:, i * w:(i + 1) * w]


def _s5_rec_kernel(*refs, bsz, nc_tiles, nl_tiles):
    zc, zl = refs[0:4], refs[4:8]
    pf_ref, pb_ref = refs[8:10]
    hc, hl = refs[10:14], refs[14:18]
    pw = ((pf_ref[0], pf_ref[1]), (pb_ref[0], pb_ref[1]))
    sub = lax.broadcasted_iota(jnp.int32, pw[0][0].shape, 0)

    def cmul(a, b):
        return a[0] * b[0] - a[1] * b[1], a[0] * b[1] + a[1] * b[0]

    def cadd(a, b):
        return a[0] + b[0], a[1] + b[1]

    def scan_tile(z, carry, d):
        p = pw[d]
        h = z
        for sh in (1, 2, 4):
            prow = sh - 1 if d == 0 else 8 - sh
            keep = (sub >= sh) if d == 0 else (sub < 8 - sh)
            shift = sh if d == 0 else 8 - sh
            moved = tuple(jnp.where(keep, pltpu.roll(v, shift, 0), 0.0) for v in h)
            h = cadd(h, cmul((p[0][prow:prow + 1, :], p[1][prow:prow + 1, :]), moved))
        full = cadd(h, cmul(p, carry))
        edge, shift, last = (0, 1, 7) if d == 0 else (7, 7, 0)
        before = tuple(jnp.where(sub == edge, c, pltpu.roll(v, shift, 0)) for v, c in zip(full, carry))
        return before, tuple(v[last:last + 1, :] for v in full)

    def walk(z_refs, h_refs, n_tiles, carries):
        def body(i, carries):
            out = []
            for b in range(bsz):
                for d in (0, 1):
                    t = i if d == 0 else n_tiles - 1 - i
                    r0 = pl.multiple_of((b * n_tiles + t) * 8, 8)
                    z = (z_refs[2 * d][pl.ds(r0, 8), :], z_refs[2 * d + 1][pl.ds(r0, 8), :])
                    before, new = scan_tile(z, carries[2 * b + d], d)
                    h_refs[2 * d][pl.ds(r0, 8), :] = before[0]
                    h_refs[2 * d + 1][pl.ds(r0, 8), :] = before[1]
                    out.append(new)
            return tuple(out)
        return lax.fori_loop(0, n_tiles, body, carries)

    zero = jnp.zeros((1, sub.shape[1]), F32)
    carries = walk(zc, hc, nc_tiles, tuple((zero, zero) for _ in range(2 * bsz)))
    walk(zl, hl, nl_tiles, carries)


def _s5_out_kernel(u_ref, hfr, hfi, hbr, hbi, bd_ref, rc_ref, y_ref, wm_ref, wr_ref):
    kt = hfr.shape[0]
    lt = S5_LANE_TILE

    @pl.when(pl.program_id(1) == 0)
    def _():
        for s in range(S5_CHUNK):
            for t in range(S5_CHUNK):
                wm_ref[s * lt:(s + 1) * lt, t * lt:(t + 1) * lt] = bd_ref[t - s + S5_CHUNK - 1]
        for i in range(4):
            wr_ref[i] = _expand_blockdiag(rc_ref[i], S5_STATE, S5_GROUP, lt, wr_ref.shape[2])

    y = _dot(_s5_load_chunks(u_ref, kt), wm_ref[...])
    for i, h_ref in enumerate((hfr, hfi, hbr, hbi)):
        y += _dot(h_ref[...].astype(BF16), wr_ref[i])
    w = y_ref.shape[1]
    for t in range(S5_CHUNK):
        y_ref[pl.ds(t, kt, stride=S5_CHUNK), :] = y[:, t * w:(t + 1) * w]


def _s5_matrices(lam_re, lam_im, log_step, b_re, b_im, c_re, c_im):
    hi = lax.Precision.HIGHEST
    L = S5_CHUNK
    step = jnp.exp(log_step)[..., None]
    n = jnp.arange(L + 1, dtype=F32)[:, None, None, None]
    mag = jnp.exp(n * (lam_re * step)[None])
    ang = n * (lam_im * step)[None]
    p_re, p_im = mag * jnp.cos(ang), mag * jnp.sin(ang)
    ab_re, ab_im = p_re[1], p_im[1]
    den = lam_re * lam_re + lam_im * lam_im
    coef_re = ((ab_re - 1.0) * lam_re + ab_im * lam_im) / den
    coef_im = (ab_im * lam_re - (ab_re - 1.0) * lam_im) / den
    e_re = p_re[:L] * coef_re - p_im[:L] * coef_im
    e_im = p_re[:L] * coef_im + p_im[:L] * coef_re
    ce_re = c_re[None] * e_re[..., None, :] - c_im[None] * e_im[..., None, :]
    ce_im = c_re[None] * e_im[..., None, :] + c_im[None] * e_re[..., None, :]
    kern = (jnp.einsum('ndgps,dgsq->ndgpq', ce_re, b_re, precision=hi)
            - jnp.einsum('ndgps,dgsq->ndgpq', ce_im, b_im, precision=hi))
    w_re, w_im = e_re[::-1], e_im[::-1]
    q_re = (w_re[..., None] * b_re[None] - w_im[..., None] * b_im[None]).transpose(1, 2, 0, 4, 3)
    q_im = (w_re[..., None] * b_im[None] + w_im[..., None] * b_re[None]).transpose(1, 2, 0, 4, 3)
    a_re, a_im = p_re[1:], p_im[1:]
    ca_re = c_re[None] * a_re[..., None, :] - c_im[None] * a_im[..., None, :]
    ca_im = c_re[None] * a_im[..., None, :] + c_im[None] * a_re[..., None, :]
    r_re = ca_re.transpose(1, 2, 4, 0, 3)
    r_im = (-ca_im).transpose(1, 2, 4, 0, 3)
    nl = L * jnp.arange(1, 9, dtype=F32)[:, None, None, None]
    t_mag = jnp.exp(nl * (lam_re * step)[None])
    t_ang = nl * (lam_im * step)[None]
    return kern, q_re, q_im, r_re, r_im, t_mag * jnp.cos(t_ang), t_mag * jnp.sin(t_ang)


def _s5_layer(u_lat, u_ctx, params, *, bsz, seq, ctx_len):
    c = u_lat.shape[1]
    L, pp, ns = S5_CHUNK, S5_GROUP, S5_STATE
    g = c // pp
    kern, q_re, q_im, r_re, r_im, t_re, t_im = _s5_matrices(*params)
    gl = S5_LANE_TILE // pp
    nb = g // gl
    lt = S5_LANE_TILE
    kw = L * lt
    sw = gl * ns
    lanes = g * ns
    kt_ = kern.transpose(0, 1, 2, 4, 3)
    lags = jnp.concatenate([kt_[:0:-1, 1], (kt_[0, 0] + kt_[0, 1])[None], kt_[1:, 0]], axis=0)
    eye = jnp.eye(gl, dtype=F32)
    bd = (lags.reshape(2 * L - 1, nb, gl, pp, pp).transpose(1, 0, 2, 3, 4)[:, :, :, :, None, :]
          * eye[None, None, :, None, :, None]).reshape(nb, 2 * L - 1, lt, lt).astype(BF16)
    qs = jnp.stack([q_re[0], q_im[0], q_re[1][:, ::-1], q_im[1][:, ::-1]])
    qc = qs.reshape(4, nb, gl, L, pp, ns).transpose(1, 3, 2, 4, 0, 5).reshape(nb, L, lt, 4 * ns).astype(BF16)
    rs = jnp.stack([r_re[0], r_im[0], r_re[1][:, :, ::-1], r_im[1][:, :, ::-1]])
    rc = rs.reshape(4, nb, gl, ns, L, pp).transpose(1, 0, 2, 3, 4, 5).reshape(nb, 4, sw, L * pp).astype(BF16)
    pf =jnp.stack([t_re[:, 0], t_im[:, 0]]).reshape(2, 8, lanes)
    pb = jnp.stack([t_re[::-1, 1], t_im[::-1, 1]]).reshape(2, 8, lanes)

    u_spec = lambda kt: pl.BlockSpec((kt * L, S5_LANE_TILE), lambda b, i: (i, b))
    h_spec = lambda kt: pl.BlockSpec((kt, sw), lambda b, i: (i, b))
    cp2 = _cparams(("parallel", "arbitrary"))

    def state(u):
        rows = u.shape[0] // L
        kt = min(S5_ROW_TILE, rows)
        return pl.pallas_call(
            _s5_state_kernel,
            out_shape=[jax.ShapeDtypeStruct((rows, lanes), F32)] * 4,
            grid=(nb, rows // kt),
            in_specs=[u_spec(kt), pl.BlockSpec((None, L, lt, 4 * ns), lambda b, i: (b, 0, 0, 0))],
            out_specs=[h_spec(kt)] * 4,
            scratch_shapes=[pltpu.VMEM((kw, 4 * sw), BF16)],
            compiler_params=cp2,
            name="s5_state",
        )(u, qc)

    z_c, z_l = state(u_ctx), state(u_lat)
    lb = S5_LANES
    rows_c, rows_l = z_c[0].shape[0], z_l[0].shape[0]
    h = pl.pallas_call(
        functools.partial(_s5_rec_kernel, bsz=bsz, nc_tiles=rows_c // bsz // 8, nl_tiles=rows_l // bsz // 8),
        out_shape=[jax.ShapeDtypeStruct((rows_c, lanes), F32)] * 4 + [jax.ShapeDtypeStruct((rows_l, lanes), F32)] * 4,
        grid=(lanes // lb,),
        in_specs=[pl.BlockSpec((rows_c, lb), lambda j: (0, j))] * 4
        + [pl.BlockSpec((rows_l, lb), lambda j: (0, j))] * 4
        + [pl.BlockSpec((2, 8, lb), lambda j: (0, 0, j))] * 2,
        out_specs=[pl.BlockSpec((rows_c, lb), lambda j: (0, j))] * 4
        + [pl.BlockSpec((rows_l, lb), lambda j: (0, j))] * 4,
        compiler_params=_cparams(("parallel",)),
        name="s5_recurrence",
    )(*z_c, *z_l, pf, pb)

    def readout(u, hs):
        rows = u.shape[0] // L
        kt = min(S5_ROW_TILE, rows)
        return pl.pallas_call(
            _s5_out_kernel,
            out_shape=jax.ShapeDtypeStruct(u.shape, F32),
            grid=(nb, rows // kt),
            in_specs=[u_spec(kt)] + [h_spec(kt)] * 4
            + [pl.BlockSpec((None, 2 * L - 1, lt, lt), lambda b, i: (b, 0, 0, 0)),
               pl.BlockSpec((None, 4, sw, L * pp), lambda b, i: (b, 0, 0, 0))],
            out_specs=u_spec(kt),
            scratch_shapes=[pltpu.VMEM((kw, kw), BF16), pltpu.VMEM((4, sw, kw), BF16)],
            compiler_params=cp2,
            name="s5_out",
        )(u, *hs, bd, rc)

    return readout(u_lat, h[4:]), readout(u_ctx, h[:4])


def _head_norm(o, g_ref, center, dh):
    outs = []
    for h in range(HEADS):
        oh = o[:, h * dh:(h + 1) * dh]
        if center:
            oh = oh - jnp.mean(oh, axis=-1, keepdims=True)
        outs.append(oh * lax.rsqrt(jnp.mean(oh * oh, axis=-1, keepdims=True) + EPS))
    return jnp.concatenate(outs, axis=-1) * g_ref[...]


def _finish(x_ref, a, b, wout_ref, mod_ref, gpost_ref, o_ref, n_r):
    half = a.shape[1]
    b = b.astype(BF16)
    if n_r:
        b = _dot(_perm_matrix(b.shape[0], n_r, False), b).astype(BF16)
    y = _dot(a.astype(BF16), wout_ref[0:half, :]) + _dot(b, wout_ref[half:2 * half, :])
    out = _tile2d(x_ref) + mod_ref[5:6, :] * _rms(y) * gpost_ref[1:2, :]
    o_ref[...] = out.reshape(o_ref.shape)


def _even_merge_kernel(x_ref, ga_ref, oaf_ref, oab_ref, gb_ref, obf_ref, obb_ref, na_ref, nb_ref,
                       wout_ref, mod_ref, gpost_ref, o_ref, *, dh, n_r):
    a = _silu(_tile2d(ga_ref)) * _head_norm(_tile2d(oaf_ref) + _tile2d(oab_ref), na_ref, False, dh)
    b = _sigmoid(_tile2d(gb_ref)) * _head_norm(_tile2d(obf_ref) + _tile2d(obb_ref), nb_ref, True, dh)
    _finish(x_ref, a, b, wout_ref, mod_ref, gpost_ref, o_ref, n_r)


def _odd_merge_kernel(x_ref, u_ref, ys_ref, gb_ref, obf_ref, obb_ref, sd_ref, wglu_ref, bglu_ref, nb_ref,
                      wout_ref, mod_ref, gpost_ref, o_ref, *, dh, n_r):
    y = _tile2d(ys_ref) + sd_ref[...] * _tile2d(u_ref)
    g = jax.nn.gelu(y)
    a = g * _sigmoid(_dot(g.astype(BF16), wglu_ref[...]) + bglu_ref[...])
    b = _silu(_tile2d(gb_ref)) * _head_norm(_tile2d(obf_ref) + _tile2d(obb_ref), nb_ref, True, dh)
    _finish(x_ref, a, b, wout_ref, mod_ref, gpost_ref, o_ref, n_r)


def _merge(kernel, s, rm_lat, cm_lat, rm_ctx, cm_ctx, consts, w_out, mod, gpost, *, bsz, seq, ctx_len, name):
    d = s.shape[1]
    half = d // 2
    rows = seq // GRID_W
    rt = min(MERGE_ROWS, rows)
    n_rh = rows // rt
    ncb = GRID_W // COL_TILE
    per_b = n_rh * ncb
    const_specs = [pl.BlockSpec(a.shape, lambda i, nd=a.ndim: (0,) * nd) for a in consts] \
        + [pl.BlockSpec((d, d), lambda i: (0, 0))]
    gpost_spec = pl.BlockSpec((3, d), lambda i: (0, 0))
    cp = _cparams(("parallel",))

    rm_idx = lambda i: ((i // per_b) * n_rh + (i % per_b) // ncb, i % ncb)
    cm_idx = lambda i: ((i // per_b) * ncb + i % ncb, (i % per_b) // ncb)
    s3 = s.reshape(s.shape[0] // GRID_W, GRID_W, d)
    ins = [s3] + [a.reshape(bsz * rows, GRID_W, a.shape[1]) for a, _ in rm_lat] \
        + [a.reshape(bsz * GRID_W, rows, a.shape[1]) for a, _ in cm_lat]
    in_specs = [pl.BlockSpec((rt, COL_TILE, d), lambda i: rm_idx(i) + (0,))] \
        + [pl.BlockSpec((rt, COL_TILE, half), lambda i, cb=cb: rm_idx(i) + (cb,)) for _, cb in rm_lat] \
        + [pl.BlockSpec((COL_TILE, rt, half), lambda i, cb=cb: cm_idx(i) + (cb,)) for _, cb in cm_lat]
    n_out_groups = s3.shape[0] if rm_ctx is not None else bsz * rows
    out = pl.pallas_call(
        functools.partial(kernel, n_r=rt),
        out_shape=jax.ShapeDtypeStruct((n_out_groups, GRID_W, d), F32),
        grid=(bsz * per_b,),
        in_specs=in_specs + const_specs
        + [pl.BlockSpec((None, N_MOD, d), lambda i: (i // per_b, 0, 0)), gpost_spec],
        out_specs=pl.BlockSpec((rt, COL_TILE, d), lambda i: rm_idx(i) + (0,)),
        compiler_params=cp,
        name=name + "_lat",
    )(*ins, *consts, w_out, mod, gpost)
    out = out.reshape(n_out_groups * GRID_W, d)
    if rm_ctx is None:
        return out

    tm = rt * COL_TILE
    t0 = bsz * seq // tm
    tiles = list(rm_ctx) + list(cm_ctx)
    n_in = 1 + len(tiles) + len(consts) + 3
    inner = functools.partial(kernel, n_r=0)
    return pl.pallas_call(
        lambda *refs: inner(*refs[:n_in], refs[n_in + 1]),
        out_shape=jax.ShapeDtypeStruct(out.shape, F32),
        grid=(bsz * ctx_len // tm,),
        in_specs=[pl.BlockSpec((tm, d), lambda i: (t0 + i, 0))]
        + [pl.BlockSpec((tm, half), lambda i, cb=cb: (i, cb)) for _, cb in tiles] + const_specs
        + [pl.BlockSpec((None, N_MOD, d), lambda i: (bsz, 0, 0)), gpost_spec,
           pl.BlockSpec(memory_space=pl.ANY)],
        out_specs=pl.BlockSpec((tm, d), lambda i: (t0 + i, 0)),
        input_output_aliases={n_in: 0},
        compiler_params=cp,
        name=name + "_ctx",
    )(s, *[a for a, _ in tiles], *consts, w_out, mod, gpost, out)


def kernel(x, c, ctx, c_ctx, w_mod, b_mod, norm_pre, norm_post, ffn_w_gate, ffn_w_up, ffn_w_down, ev_w_in, ev_w_out, gla_w_gate, gla_b_gate, gla_norm, ml_conv_w, ml_conv_b, ml_b_gates, ml_norm, od_w_in, od_w_out, s5_lam_re, s5_lam_im, s5_log_step, s5_b_re, s5_b_im, s5_c_re, s5_c_im, s5_d, s5_w_glu, s5_b_glu, ret_log_decay, ret_norm):
    bsz, seq, d = x.shape
    ctx_len = ctx.shape[1]
    depth = w_mod.shape[0]
    half = d // 2
    dh = half // HEADS
    gla_dk = dh // 2
    d_ff = ffn_w_gate.shape[-1]
    tm = TOKEN_TILE
    rows = seq // GRID_W
    assert seq % tm == 0 and (bsz * ctx_len) % tm == 0 and seq == rows * GRID_W and ctx_len % CHUNK == 0
    assert 2 * bsz == 8 and rows % 8 == 0 and seq % CHUNK == 0 and half == PROJ_TILE
    assert ctx_len % WIDE_CHUNK == 0 and seq % WIDE_CHUNK == 0
    assert ctx_len % (8 * S5_CHUNK) == 0 and seq % (8 * S5_CHUNK) == 0 and bsz % SCAN_BATCH == 0
    assert all((n // S5_CHUNK) % min(S5_ROW_TILE, n // S5_CHUNK) == 0 for n in (bsz * ctx_len, bsz * seq))

    tiles_x = bsz * seq // tm
    tiles_all = tiles_x + bsz * ctx_len // tm
    tiles_per_batch = seq // tm
    mod_of_tile = lambda i: jnp.minimum(i // tiles_per_batch, bsz)

    wg, wu, wd = ffn_w_gate.astype(BF16), ffn_w_up.astype(BF16), ffn_w_down.astype(BF16)

    cc = jnp.concatenate([c, c_ctx[None], jnp.zeros((8 - bsz - 1, d), F32)], axis=0)
    mod_all = _modulation(cc, w_mod, b_mod).reshape(depth, 8, N_MOD, d)

    s = None
    dims = dict(bsz=bsz, seq=seq, ctx_len=ctx_len)

    for layer in range(depth):
        need_ctx = layer < depth - 1
        mod = mod_all[layer]
        gpre, gpost = norm_pre[layer], norm_post[layer]
        ffn = functools.partial(_ffn, mod=mod, gpre=gpre, gpost=gpost, wg=wg, wu=wu, wd=wd, layer=layer,
                                mod_of_tile=mod_of_tile)
        if s is None:
            s = ffn(x.reshape(bsz * seq, d), which=0, sub=0, n_tiles=tiles_x, out_tiles=tiles_all)
            s = ffn(ctx.reshape(bsz * ctx_len, d), which=0, sub=0, n_tiles=tiles_all - tiles_x, out_tile0=tiles_x,
                    out_tiles=tiles_all, prev=s)
        else:
            s = ffn(s, which=0, sub=0, n_tiles=tiles_all)

        if layer % 2 == 0:
            e = layer // 2
            w = ev_w_in[e]
            qk_w = 2 * HEADS * gla_dk
            n_gate = 4 * HEADS
            c_low = qk_w + 2 * half
            c_ml = c_low + 2 * GLA_RANK
            c_g = c_ml + 4 * half
            w_in = jnp.concatenate([w[:, :c_low], w[:, c_ml:c_g]], axis=1).astype(BF16)
            w_small = jnp.concatenate([w[:, c_low:c_ml], jnp.zeros((d, SMALL_W - 2 * GLA_RANK), F32),
                                       w[:, c_g:], jnp.zeros((d, SMALL_W - n_gate), F32)], axis=1).astype(BF16)
            (pa, pb, psa, psb), (pa_c, pb_c, psa_c, psb_c) = _proj(s, mod, gpre, w_in, w_small,
                                                                   n_a=c_low // PROJ_TILE, **dims)

            st0 = [jnp.zeros((bsz, 2, HEADS, dh, gla_dk), F32)]
            gla_ins = lambda p, ps: [(p, qk_w // 2, 0), (p, qk_w // 2, qk_w // 2), (p, half, qk_w), (ps, SMALL_W, 0)]
            oa, oa_c = _bidir_scan(
                functools.partial(_gla_kernel, dk=gla_dk, dv=dh, gated=True,
                                  q_scale=float(gla_dk) ** -0.5, k_scale=1.0), CHUNK,
                gla_ins(pa, psa), gla_ins(pa_c, psa_c), [gla_w_gate[e].astype(BF16), gla_b_gate[e]], st0,
                out_width=half, name="gla", **dims)

            scale = jnp.concatenate([jnp.ones((half,), F32), jnp.full((half,), float(dh) ** -0.5, F32)])
            qk = _conv(pb, ml_conv_w[e], ml_conv_b[e], scale, bsz=bsz, n=seq)
            qk_c = _conv(pb_c, ml_conv_w[e], ml_conv_b[e], scale, bsz=bsz, n=ctx_len)
            bias = jnp.zeros((1, SMALL_W), F32).at[0, :n_gate].set(ml_b_gates[e].reshape(-1))
            ml_ins = lambda qk_, p, ps: [(qk_, half, 0), (qk_, half, half), (p, half, 2 * half), (ps, SMALL_W, 0)]
            st0 = [jnp.zeros((bsz, 2, HEADS, dh, dh), F32), jnp.zeros((bsz, 2, HEADS, 8, dh), F32)]
            ob, ob_c = _bidir_scan(
                functools.partial(_mlstm_kernel, dh=dh), WIDE_CHUNK,
                ml_ins(qk, pb, psb), ml_ins(qk_c, pb_c, psb_c), [bias], st0,
                out_width=half, name="mlstm", out_dtype=BF16, **dims)

            ga_cb, gb_cb = (qk_w + half) // half, 3
            s = _merge(
                functools.partial(_even_merge_kernel, dh=dh), s,
                [(pa, ga_cb), (oa[0], 0), (oa[1], 0)], [(pb, gb_cb), (ob[0], 0), (ob[1], 0)],
                [(pa_c, ga_cb), (oa_c[0], 0), (oa_c[1], 0)] if need_ctx else None,
                [(pb_c, gb_cb), (ob_c[0], 0), (ob_c[1], 0)],
                [gla_norm[e].reshape(1, half), ml_norm[e].reshape(1, half)],
                ev_w_out[e].astype(BF16), mod, gpost, name="even_merge", **dims)
        else:
            o = layer // 2
            (pa, pb), (pa_c, pb_c) = _proj(s, mod, gpre, od_w_in[o].astype(BF16), None, n_a=1, **dims)

            ys, ys_c = _s5_layer(pa, pa_c, (s5_lam_re[o], s5_lam_im[o], s5_log_step[o], s5_b_re[o], s5_b_im[o],
                                            s5_c_re[o], s5_c_im[o]), **dims)

            lg = jnp.repeat(-jnp.exp(ret_log_decay[o]), dh, axis=1)
            ret_ins = lambda p: [(p, half, 0), (p, half, half), (p, half, 2 * half)]
            st0 = [jnp.zeros((bsz, 2, HEADS, dh, dh), F32)]
            od, od_c = _bidir_scan(
                functools.partial(_gla_kernel, dk=dh, dv=dh, gated=False,
                                  q_scale=1.0, k_scale=float(dh) ** -0.5), WIDE_CHUNK,
                ret_ins(pb), ret_ins(pb_c), [lg], st0, out_width=half, name="retention", out_dtype=BF16, **dims)

            s = _merge(
                functools.partial(_odd_merge_kernel, dh=dh), s,
                [(pa, 0), (ys, 0)], [(pb, 3), (od[0], 0), (od[1], 0)],
                [(pa_c, 0), (ys_c, 0)] if need_ctx else None, [(pb_c, 3), (od_c[0], 0), (od_c[1], 0)],
                [s5_d[o].reshape(1, half), s5_w_glu[o].astype(BF16), s5_b_glu[o].reshape(1, half),
                 ret_norm[o].reshape(1, half)],
                od_w_out[o].astype(BF16), mod, gpost, name="odd_merge", **dims)

        s = ffn(s, which=1, sub=2, n_tiles=tiles_all if need_ctx else tiles_x)

    return s[:bsz * seq].reshape(bsz, seq, d)
```

```python
import functools

import jax
import jax.numpy as jnp
from jax import lax
from jax.experimental import pallas as pl
from jax.experimental.pallas import tpu as pltpu

F32 = jnp.float32
BF16 = jnp.bfloat16

EPS = 1e-6
GRID_W = 64
COL_TILE = 8
CHUNK = 64
WIDE_CHUNK = 128
SCAN_BATCH = 2
FFN_RES = 0.5
N_MOD = 9
HEADS = 4
GLA_RANK = 16
GLA_GATE_NORM = 16.0
S5_GROUP = 16
S5_STATE = 64
S5_CHUNK = 8
S5_LANE_TILE = 128
S5_ROW_TILE = 512
SMALL_W = 128

TOKEN_TILE = 512
FFN_TILE = 1024
FFN_VMEM_LIMIT = 62 * 1024 * 1024
MERGE_ROWS = 32
FF_TILE = 256
PROJ_TILE = 1024
MOD_TILE = 1024
CONV_TILE = 256
CONV_ROWS = 256
S5_LANES = 256
VMEM_LIMIT = 56 * 1024 * 1024


def _cparams(sem):
    return pltpu.CompilerParams(dimension_semantics=sem, vmem_limit_bytes=VMEM_LIMIT)


def _rms(x):
    return x * lax.rsqrt(jnp.mean(x * x, axis=-1, keepdims=True) + EPS)


def _sigmoid(x):
    return 1.0 / (1.0 + jnp.exp(-x))


def _silu(x):
    return x * _sigmoid(x)


def _log_sigmoid(x):
    return jnp.minimum(x, 0.0) - jnp.log(1.0 + jnp.exp(-jnp.abs(x)))


def _dot(a, b):
    return jnp.dot(a, b, preferred_element_type=F32)


def _dot_nt(a, b):
    return lax.dot_general(a, b, (((1,), (1,)), ((), ())), preferred_element_type=F32)


def _dot_tn(a, b):
    return lax.dot_general(a, b, (((0,), (0,)), ((), ())), preferred_element_type=F32)


def _dot01(t01, x):
    x0 = x.astype(BF16)
    r1 = x - x0.astype(F32)
    x1 = r1.astype(BF16)
    x2 = (r1 - x1.astype(F32)).astype(BF16)
    return _dot(t01, x0) + _dot(t01, x1) + _dot(t01, x2)


def _perm_matrix(n, n_r, to_col_major):
    i = lax.broadcasted_iota(jnp.int32, (n, n), 0)
    c = lax.broadcasted_iota(jnp.int32, (n, n), 1)
    if to_col_major:
        hit = (i // n_r == c % COL_TILE) & (i % n_r == c // COL_TILE)
    else:
        hit = (i // COL_TILE == c % n_r) & (i % COL_TILE == c // n_r)
    return jnp.where(hit, 1.0, 0.0).astype(BF16)


def _tile2d(ref):
    v = ref[...]
    return v.reshape(-1, v.shape[-1]).astype(F32)


def _mod_kernel(c_ref, w_ref, b_ref, o_ref):
    a = _silu(c_ref[...]).astype(BF16)
    o_ref[...] = _dot(a, w_ref[...].astype(BF16)) + b_ref[...]


def _modulation(cc, w_mod, b_mod):
    depth, d, nm = w_mod.shape
    return pl.pallas_call(
        _mod_kernel,
        out_shape=jax.ShapeDtypeStruct((depth, 8, nm), F32),
        grid=(depth, nm // MOD_TILE),
        in_specs=[pl.BlockSpec((8, d), lambda l, j: (0, 0)),
                  pl.BlockSpec((None, d, MOD_TILE), lambda l, j: (l, 0, j)),
                  pl.BlockSpec((None, 1, MOD_TILE), lambda l, j: (l, 0, j))],
        out_specs=pl.BlockSpec((None, 8, MOD_TILE), lambda l, j: (l, 0, j)),
        compiler_params=_cparams(("parallel", "parallel")),
        name="modulation",
    )(cc, w_mod, b_mod.reshape(depth, 1, nm))


def _ffn_kernel(x_ref, mod_ref, gpre_ref, gpost_ref, wg_ref, wu_ref, wd_ref, o_ref, h_ref, acc_ref, *, sub, nff,
                tail):
    j = pl.program_id(1)

    @pl.when(j == 0)
    def _():
        shift = mod_ref[3 * sub:3 * sub + 1, :]
        scale = mod_ref[3 * sub + 1:3 * sub + 2, :]
        h = _rms(x_ref[...]) * gpre_ref[sub:sub + 1, :] * (1.0 + scale) + shift
        h_ref[...] = h.astype(BF16)

    def hidden_step(valid, first):
        h = h_ref[...]
        t = _silu(_dot(h, wg_ref[:, :valid])) * _dot(h, wu_ref[:, :valid])
        y = _dot(t.astype(BF16), wd_ref[:valid, :])
        if first:
            acc_ref[...] = y
        else:
            acc_ref[...] += y

    pl.when(j == 0)(lambda: hidden_step(tail or FF_TILE, True))
    pl.when(j > 0)(lambda: hidden_step(FF_TILE, False))

    @pl.when(j == nff - 1)
    def _():
        gate = mod_ref[3 * sub + 2:3 * sub + 3, :]
        o_ref[...] = x_ref[...] + FFN_RES * gate * _rms(acc_ref[...]) * gpost_ref[sub:sub + 1, :]


def _ffn(s, mod, gpre, gpost, wg, wu, wd, *, layer, which, sub, n_tiles, mod_of_tile, out_tile0=0, out_tiles=None,
         prev=None):
    d = s.shape[1]
    d_ff = wg.shape[-1]
    nff = pl.cdiv(d_ff, FF_TILE)
    assert nff >= 2 and (d_ff % FF_TILE) % 128 == 0
    tm = FFN_TILE
    out_tiles = n_tiles if out_tiles is None else out_tiles
    kernel = functools.partial(_ffn_kernel, sub=sub, nff=nff, tail=d_ff % FF_TILE)
    blk = lambda j: (j + nff - 1) % nff
    ins = [s, mod, gpre, gpost, wg, wu, wd]
    in_specs = [pl.BlockSpec((tm, d), lambda i, j: (i, 0)),
                pl.BlockSpec((None, N_MOD, d), lambda i, j: (mod_of_tile(out_tile0 + i), 0, 0)),
                pl.BlockSpec((3, d), lambda i, j: (0, 0)),
                pl.BlockSpec((3, d), lambda i, j: (0, 0)),
                pl.BlockSpec((None, None, d, FF_TILE), lambda i, j: (layer, which, 0, blk(j))),
                pl.BlockSpec((None, None, d, FF_TILE), lambda i, j: (layer, which, 0, blk(j))),
                pl.BlockSpec((None, None, FF_TILE, d), lambda i, j: (layer, which, blk(j), 0))]
    aliases = {}
    if prev is not None:
        n_in = len(ins)
        inner = kernel
        kernel = lambda *refs: inner(*refs[:n_in], *refs[n_in + 1:])
        ins.append(prev)
        in_specs.append(pl.BlockSpec(memory_space=pl.ANY))
        aliases = {n_in: 0}
    return pl.pallas_call(
        kernel,
        out_shape=jax.ShapeDtypeStruct((out_tiles * tm, d), F32),
        grid=(n_tiles, nff),
        in_specs=in_specs,
        out_specs=pl.BlockSpec((tm, d), lambda i, j: (out_tile0 + i, 0)),
        scratch_shapes=[pltpu.VMEM((tm, d), BF16), pltpu.VMEM((tm, d), F32)],
        input_output_aliases=aliases,
        compiler_params=pltpu.CompilerParams(dimension_semantics=("parallel", "arbitrary"),
                                             vmem_limit_bytes=FFN_VMEM_LIMIT),
        name="ffn",
    )(*ins)


def _proj_kernel(*refs, n_a, n_r, has_small):
    if has_small:
        x_ref, mod_ref, gpre_ref, w_ref, ws_ref, pa_ref, pb_ref, psa_ref, psb_ref, h_ref, hp_ref = refs
    else:
        x_ref, mod_ref, gpre_ref, w_ref, pa_ref, pb_ref, h_ref, hp_ref = refs
    j = pl.program_id(1)
    tm = h_ref.shape[0]

    @pl.when(j == 0)
    def _():
        x = _tile2d(x_ref)
        h = (_rms(x) * gpre_ref[1:2, :] * (1.0 + mod_ref[4:5, :]) + mod_ref[3:4, :]).astype(BF16)
        hp = _dot(_perm_matrix(tm, n_r, True), h).astype(BF16) if n_r else h
        h_ref[...] = h
        hp_ref[...] = hp
        if has_small:
            psa_ref[...] = _dot(h, ws_ref[:, :SMALL_W]).reshape(psa_ref.shape)
            psb_ref[...] = _dot(hp, ws_ref[:, SMALL_W:])

    @pl.when(j < n_a)
    def _():
        pa_ref[...] = _dot(h_ref[...], w_ref[j]).reshape(pa_ref.shape)

    @pl.when(j >= n_a)
    def _():
        pb_ref[...] = _dot(hp_ref[...], w_ref[j])


def _proj(s, mod, gpre, w, w_small, *, n_a, bsz, seq, ctx_len):
    d = s.shape[1]
    rows = seq // GRID_W
    tn = PROJ_TILE
    n_tiles_n = w.shape[1] // tn
    n_b = n_tiles_n - n_a
    has_small = w_small is not None
    w_ins = [w.reshape(d, n_tiles_n, tn).transpose(1, 0, 2)] + ([w_small] if has_small else [])
    w_specs = [pl.BlockSpec((n_tiles_n, d, tn), lambda i, j: (0, 0, 0), pipeline_mode=pl.Buffered(1))]
    if has_small:
        w_specs.append(pl.BlockSpec((d, 2 * SMALL_W), lambda i, j: (0, 0)))
    ja = lambda j: jnp.minimum(j, n_a - 1)
    jb = lambda j: jnp.maximum(j - n_a, 0)
    cp = _cparams(("parallel", "arbitrary"))

    tm = rows * COL_TILE
    ncb = GRID_W // COL_TILE
    s3 = s.reshape(s.shape[0] // GRID_W, GRID_W, d)
    out_shape = [jax.ShapeDtypeStruct((bsz * rows, GRID_W, n_a * tn), F32),
                 jax.ShapeDtypeStruct((bsz * seq, n_b * tn), F32)]
    out_specs = [pl.BlockSpec((rows, COL_TILE, tn), lambda i, j: (i // ncb, i % ncb, ja(j))),
                 pl.BlockSpec((tm, tn), lambda i, j: (i, jb(j)))]
    if has_small:
        out_shape += [jax.ShapeDtypeStruct((bsz * rows, GRID_W, SMALL_W), F32),
                      jax.ShapeDtypeStruct((bsz * seq, SMALL_W), F32)]
        out_specs += [pl.BlockSpec((rows, COL_TILE, SMALL_W), lambda i, j: (i // ncb, i % ncb, 0)),
                      pl.BlockSpec((tm, SMALL_W), lambda i, j: (i, 0))]
    lat = pl.pallas_call(
        functools.partial(_proj_kernel, n_a=n_a, n_r=rows, has_small=has_small),
        out_shape=out_shape,
        grid=(bsz * ncb, n_tiles_n),
        in_specs=[pl.BlockSpec((rows, COL_TILE, d), lambda i, j: (i // ncb, i % ncb, 0)),
                  pl.BlockSpec((None, N_MOD, d), lambda i, j: (i // ncb, 0, 0)),
                  pl.BlockSpec((3, d), lambda i, j: (0, 0))] + w_specs,
        out_specs=out_specs,
        scratch_shapes=[pltpu.VMEM((tm, d), BF16), pltpu.VMEM((tm, d), BF16)],
        compiler_params=cp,
        name="in_proj_lat",
    )(s3, mod, gpre, *w_ins)
    lat = [lat[0].reshape(bsz * seq, n_a * tn), lat[1]] \
        + ([lat[2].reshape(bsz * seq, SMALL_W), lat[3]] if has_small else [])

    tc = TOKEN_TILE
    n_c = bsz * ctx_len
    t0 = bsz * seq // tc
    out_shape = [jax.ShapeDtypeStruct((n_c, n_a * tn), F32), jax.ShapeDtypeStruct((n_c, n_b * tn), F32)]
    out_specs = [pl.BlockSpec((tc, tn), lambda i, j: (i, ja(j))), pl.BlockSpec((tc, tn), lambda i, j: (i, jb(j)))]
    if has_small:
        out_shape += [jax.ShapeDtypeStruct((n_c, SMALL_W), F32)] * 2
        out_specs += [pl.BlockSpec((tc, SMALL_W), lambda i, j: (i, 0))] * 2
    ctx = pl.pallas_call(
        functools.partial(_proj_kernel, n_a=n_a, n_r=0, has_small=has_small),
        out_shape=out_shape,
        grid=(n_c // tc, n_tiles_n),
        in_specs=[pl.BlockSpec((tc, d), lambda i, j: (t0 + i, 0)),
                  pl.BlockSpec((None, N_MOD, d), lambda i, j: (bsz, 0, 0)),
                  pl.BlockSpec((3, d), lambda i, j: (0, 0))] + w_specs,
        out_specs=out_specs,
        scratch_shapes=[pltpu.VMEM((tc, d), BF16), pltpu.VMEM((tc, d), BF16)],
        compiler_params=cp,
        name="in_proj_ctx",
    )(s, mod, gpre, *w_ins)
    return lat, list(ctx)


def _conv_kernel(x_ref, w_ref, b_ref, sc_ref, o_ref):
    n, ct = x_ref.shape
    ch = CONV_ROWS
    n_ch = n // ch
    r = lax.broadcasted_iota(jnp.int32, (ch, ct), 0)

    def body(c, carry):
        r0 = pl.multiple_of(c * ch, ch)
        x = x_ref[pl.ds(r0, ch), :]
        before = x_ref[pl.ds(pl.multiple_of(jnp.maximum(r0 - 8, 0), 8), 8), :][7:8, :]
        after = x_ref[pl.ds(pl.multiple_of(jnp.minimum(r0 + ch, n - 8), 8), 8), :][0:1, :]
        before = jnp.where(c == 0, 0.0, before)
        after = jnp.where(c == n_ch - 1, 0.0, after)
        prev = jnp.where(r == 0, before, pltpu.roll(x, 1, 0))
        nxt = jnp.where(r == ch - 1, after, pltpu.roll(x, ch - 1, 0))
        y = w_ref[1:2, :] * x + w_ref[0:1, :] * prev + w_ref[2:3, :] * nxt + b_ref[...]
        o_ref[pl.ds(r0, ch), :] = _silu(y) * sc_ref[...]
        return carry

    lax.fori_loop(0, n_ch, body, 0)


def _conv(p, conv_w, conv_b, scale, *, bsz, n):
    c = conv_w.shape[1]
    ct = CONV_TILE
    return pl.pallas_call(
        _conv_kernel,
        out_shape=jax.ShapeDtypeStruct((bsz * n, c), F32),
        grid=(bsz, c // ct),
        in_specs=[pl.BlockSpec((n, ct), lambda b, j: (b, j)),
                  pl.BlockSpec((3, ct), lambda b, j: (0, j)),
                  pl.BlockSpec((1, ct), lambda b, j: (0, j)),
                  pl.BlockSpec((1, ct), lambda b, j: (0, j))],
        out_specs=pl.BlockSpec((n, ct), lambda b, j: (b, j)),
        compiler_params=_cparams(("parallel", "parallel")),
        name="conv",
    )(p, conv_w, conv_b.reshape(1, c), scale.reshape(1, c))


def _gla_kernel(*refs, L, dk, dv, gated, q_scale, k_scale):
    if gated:
        qf, kf, vf, df, qb, kb, vb, db, wgate_ref, bgate_ref, st_in, of, ob, st_out = refs
    else:
        qf, kf, vf, qb, kb, vb, lg_ref, st_in, of, ob, st_out = refs
        df = db = None

    @pl.when(pl.program_id(1) == 0)
    def _():
        st_out[...] = st_in[...]

    row = lax.broadcasted_iota(jnp.int32, (L, L), 0)
    col = lax.broadcasted_iota(jnp.int32, (L, L), 1)
    trow = lax.broadcasted_iota(jnp.int32, (L, 1), 0)
    for d, (q_ref, k_ref, v_ref, d_ref, o_ref) in enumerate(((qf, kf, vf, df, of), (qb, kb, vb, db, ob))):
        mask = (row >= col) if d == 0 else (row <= col)
        mid = L // 2 if d == 0 else L - 1 - L // 2
        end = L - 1 if d == 0 else 0
        if gated:
            low = d_ref[:, d * GLA_RANK:(d + 1) * GLA_RANK].astype(BF16)
            z = _dot(low, wgate_ref[d]) + bgate_ref[d:d + 1, :]
            la = _log_sigmoid(z) * (1.0 / GLA_GATE_NORM)
            bcum = _dot01(mask.astype(BF16), la)
        else:
            cnt = (trow + 1) if d == 0 else (L - trow)
            bcum = cnt.astype(F32) * lg_ref[d:d + 1, :]
            dist = jnp.abs(row - col).astype(F32)
        for h in range(HEADS):
            b = bcum[:, h * dk:(h + 1) * dk]
            b_end = b[end:end + 1, :]
            q = q_ref[:, h * dk:(h + 1) * dk] * q_scale
            k = k_ref[:, h * dk:(h + 1) * dk] * k_scale
            v = v_ref[:, h * dv:(h + 1) * dv].astype(BF16)
            if gated:
                b_mid = b[mid:mid + 1, :]
                qs = (q * jnp.exp(b - b_mid)).astype(BF16)
                ks = (k * jnp.exp(b_mid - b)).astype(BF16)
                scores = jnp.where(mask, _dot_nt(qs, ks), 0.0)
            else:
                lg_h = lg_ref[d:d + 1, h * dk:h * dk + 1]
                scores = _dot_nt(q.astype(BF16), k.astype(BF16)) * jnp.where(mask, jnp.exp(dist * lg_h), 0.0)
            st = st_out[d, h]
            o = _dot(scores.astype(BF16), v) + _dot_nt((q * jnp.exp(b)).astype(BF16), st.astype(BF16))
            o_ref[:, h * dv:(h + 1) * dv] = o.astype(o_ref.dtype)
            kd = (k * jnp.exp(b_end - b)).astype(BF16)
            st_out[d, h] = jnp.exp(b_end) * st + _dot_tn(v, kd)


def _mlstm_kernel(qf, kf, vf, gf, qb, kb, vb, gb, bias_ref, c_in, nm_in, of, ob, c_out, nm_out, *, L, dh):
    @pl.when(pl.program_id(1) == 0)
    def _():
        c_out[...] = c_in[...]
        nm_out[...] = nm_in[...]

    row = lax.broadcasted_iota(jnp.int32, (L, L), 0)
    col = lax.broadcasted_iota(jnp.int32, (L, L), 1)
    eye = row == col
    ones = jnp.ones((L, L), BF16)
    for d, (q_ref, k_ref, v_ref, g_ref, o_ref) in enumerate(((qf, kf, vf, gf, of), (qb, kb, vb, gb, ob))):
        mask = (row >= col) if d == 0 else (row <= col)
        cum_us = (row <= col) if d == 0 else (row >= col)
        end = L - 1 if d == 0 else 0
        g = g_ref[...] + bias_ref[...]
        fl = _log_sigmoid(g)
        bc_all = _dot01(mask.astype(BF16), fl)
        for h in range(HEADS):
            ci = d * 2 * HEADS + h
            cf = ci + HEADS
            i_col = g[:, ci:ci + 1]
            f_col = fl[:, cf:cf + 1]
            b_col = bc_all[:, cf:cf + 1]
            x = jnp.where(cum_us, f_col, 0.0) - jnp.where(eye, i_col, 0.0)
            drow = _dot01(ones, x)
            m_st = nm_out[d, h, 1:2, 0:1]
            dmat = jnp.where(mask, b_col - drow, -jnp.inf)
            inter = b_col + m_st
            m_t = jnp.maximum(inter, jnp.max(dmat, axis=-1, keepdims=True))
            q = q_ref[:, h * dh:(h + 1) * dh]
            k = k_ref[:, h * dh:(h + 1) * dh]
            v = v_ref[:, h * dh:(h + 1) * dh].astype(BF16)
            qh = q.astype(BF16)
            scores = _dot_nt(qh, k.astype(BF16)) * jnp.exp(dmat - m_t)
            w_inter = jnp.exp(inter - m_t)
            c_st = c_out[d, h]
            n_st = nm_out[d, h, 0:1, :]
            num = _dot(scores.astype(BF16), v) + w_inter * _dot(qh, c_st.astype(BF16))
            den = (jnp.sum(scores, axis=-1, keepdims=True)
                   + w_inter * jnp.sum(q * n_st, axis=-1, keepdims=True))
            hout = num / jnp.maximum(jnp.abs(den), jnp.exp(-m_t))
            o_ref[:, h * dh:(h + 1) * dh] = hout.astype(o_ref.dtype)
            b_end = b_col[end:end + 1, :]
            d_end = b_end - b_col + i_col
            m_new = jnp.maximum(b_end + m_st, jnp.max(d_end, axis=0, keepdims=True))
            kw = k * jnp.exp(d_end - m_new)
            decay = jnp.exp(b_end + m_st - m_new)
            c_out[d, h] = decay * c_st + _dot_tn(kw.astype(BF16), v)
            nm_out[d, h, 0:1, :] = decay * n_st + jnp.sum(kw, axis=0, keepdims=True)
            nm_out[d, h, 1:2, :] = jnp.broadcast_to(m_new, (1, dh))


def _scan_call(kernel, bsz, n, L, chunk_ins, consts, states, out_width, name, out_dtype=F32):
    bg = SCAN_BATCH
    chunk = (lambda s: s, lambda s: n - 1 - s)
    ins, in_specs, batched = [], [], []
    for d in (0, 1):
        for a, width, coloff in chunk_ins:
            ins.append(a.reshape(bsz, n * L, a.shape[1]))
            in_specs.append(pl.BlockSpec((bg, L, width), lambda b, s, d=d, cb=coloff // width: (b, chunk[d](s), cb)))
            batched.append(True)
    for a in consts:
        ins.append(a)
        in_specs.append(pl.BlockSpec(a.shape, lambda b, s, nd=a.ndim: (0,) * nd))
        batched.append(False)
    state_specs = [pl.BlockSpec((bg,) + a.shape[1:], lambda b, s, nd=a.ndim: (b,) + (0,) * (nd - 1))
                   for a in states]
    ins += list(states)
    in_specs += state_specs
    batched += [True] * (len(states) + 2 + len(states))

    def grouped(*refs):
        for bi in range(bg):
            kernel(*[r.at[bi] if is_b else r for r, is_b in zip(refs, batched)])

    out_struct = jax.ShapeDtypeStruct((bsz, n * L, out_width), out_dtype)
    res = pl.pallas_call(
        grouped,
        out_shape=[out_struct, out_struct] + [jax.ShapeDtypeStruct(a.shape, F32) for a in states],
        grid=(bsz // bg, n),
        in_specs=in_specs,
        out_specs=[pl.BlockSpec((bg, L, out_width), lambda b, s, d=d: (b, chunk[d](s), 0)) for d in (0, 1)]
        + state_specs,
        compiler_params=_cparams(("parallel", "arbitrary")),
        name=name,
    )(*ins)
    return (res[0].reshape(bsz * n * L, out_width), res[1].reshape(bsz * n * L, out_width)), list(res[2:])


def _bidir_scan(make_kernel, chunk, ins_lat, ins_ctx, consts, states0, *, bsz, seq, ctx_len, out_width, name,
                out_dtype=F32):
    kernel = functools.partial(make_kernel, L=chunk)
    o_ctx, st = _scan_call(kernel, bsz, ctx_len // chunk, chunk, ins_ctx, consts, states0, out_width,
                           name + "_ctx", out_dtype)
    o_lat, _ = _scan_call(kernel, bsz, seq // chunk, chunk, ins_lat, consts, st, out_width, name + "_lat",
                          out_dtype)
    return o_lat, o_ctx


def _s5_load_chunks(u_ref, kt):
    return jnp.concatenate([u_ref[pl.ds(t, kt, stride=S5_CHUNK), :].astype(BF16) for t in range(S5_CHUNK)],
                           axis=-1)


def _expand_blockdiag(c, rows_per_group, lanes_per_group, lane_period, n_out):
    r_n, k = c.shape
    kk = lax.broadcasted_iota(jnp.int32, (k, n_out), 0)
    jj = lax.broadcasted_iota(jnp.int32, (k, n_out), 1)
    sel = (kk // lanes_per_group == jj // lane_period) & (kk % lanes_per_group == jj % lanes_per_group)
    rr = lax.broadcasted_iota(jnp.int32, (r_n, n_out), 0)
    cc = lax.broadcasted_iota(jnp.int32, (r_n, n_out), 1)
    same = rr // rows_per_group == (cc % lane_period) // lanes_per_group
    return jnp.where(same, _dot(c, jnp.where(sel, 1.0, 0.0).astype(BF16)), 0.0).astype(BF16)


def _s5_state_kernel(u_ref, qc_ref, zfr_ref, zfi_ref, zbr_ref, zbi_ref, w_ref):
    kt = zfr_ref.shape[0]
    lt = S5_LANE_TILE

    @pl.when(pl.program_id(1) == 0)
    def _():
        for s in range(S5_CHUNK):
            w_ref[s * lt:(s + 1) * lt, :] = _expand_blockdiag(qc_ref[s], S5_GROUP, S5_STATE,
                                                              lt // S5_GROUP * S5_STATE, w_ref.shape[1])

    z = _dot(_s5_load_chunks(u_ref, kt), w_ref[...])
    w = z.shape[1] // 4
    for i, ref in enumerate((zfr_ref, zfi_ref, zbr_ref, zbi_ref)):
        ref[...] = z[:, i * w:(i + 1) * w]


def _s5_rec_kernel(*refs, bsz, nc_tiles, nl_tiles):
    zc, zl = refs[0:4], refs[4:8]
    pf_ref, pb_ref = refs[8:10]
    hc, hl = refs[10:14], refs[14:18]
    pw = ((pf_ref[0], pf_ref[1]), (pb_ref[0], pb_ref[1]))
    sub = lax.broadcasted_iota(jnp.int32, pw[0][0].shape, 0)

    def cmul(a, b):
        return a[0] * b[0] - a[1] * b[1], a[0] * b[1] + a[1] * b[0]

    def cadd(a, b):
        return a[0] + b[0], a[1] + b[1]

    def scan_tile(z, carry, d):
        p = pw[d]
        h = z
        for sh in (1, 2, 4):
            prow = sh - 1 if d == 0 else 8 - sh
            keep = (sub >= sh) if d == 0 else (sub < 8 - sh)
            shift = sh if d == 0 else 8 - sh
            moved = tuple(jnp.where(keep, pltpu.roll(v, shift, 0), 0.0) for v in h)
            h = cadd(h, cmul((p[0][prow:prow + 1, :], p[1][prow:prow + 1, :]), moved))
        full = cadd(h, cmul(p, carry))
        edge, shift, last = (0, 1, 7) if d == 0 else (7, 7, 0)
        before = tuple(jnp.where(sub == edge, c, pltpu.roll(v, shift, 0)) for v, c in zip(full, carry))
        return before, tuple(v[last:last + 1, :] for v in full)

    def walk(z_refs, h_refs, n_tiles, carries):
        def body(i, carries):
            out = []
            for b in range(bsz):
                for d in (0, 1):
                    t = i if d == 0 else n_tiles - 1 - i
                    r0 = pl.multiple_of((b * n_tiles + t) * 8, 8)
                    z = (z_refs[2 * d][pl.ds(r0, 8), :], z_refs[2 * d + 1][pl.ds(r0, 8), :])
                    before, new = scan_tile(z, carries[2 * b + d], d)
                    h_refs[2 * d][pl.ds(r0, 8), :] = before[0]
                    h_refs[2 * d + 1][pl.ds(r0, 8), :] = before[1]
                    out.append(new)
            return tuple(out)
        return lax.fori_loop(0, n_tiles, body, carries)

    zero = jnp.zeros((1, sub.shape[1]), F32)
    carries = walk(zc, hc, nc_tiles, tuple((zero, zero) for _ in range(2 * bsz)))
    walk(zl, hl, nl_tiles, carries)


def _s5_out_kernel(u_ref, hfr, hfi, hbr, hbi, bd_ref, rc_ref, y_ref, wm_ref, wr_ref):
    kt = hfr.shape[0]
    lt = S5_LANE_TILE

    @pl.when(pl.program_id(1) == 0)
    def _():
        for s in range(S5_CHUNK):
            for t in range(S5_CHUNK):
                wm_ref[s * lt:(s + 1) * lt, t * lt:(t + 1) * lt] = bd_ref[t - s + S5_CHUNK - 1]
        for i in range(4):
            wr_ref[i] = _expand_blockdiag(rc_ref[i], S5_STATE, S5_GROUP, lt, wr_ref.shape[2])

    y = _dot(_s5_load_chunks(u_ref, kt), wm_ref[...])
    for i, h_ref in enumerate((hfr, hfi, hbr, hbi)):
        y += _dot(h_ref[...].astype(BF16), wr_ref[i])
    w = y_ref.shape[1]
    for t in range(S5_CHUNK):
        y_ref[pl.ds(t, kt, stride=S5_CHUNK), :] = y[:, t * w:(t + 1) * w]


def _s5_matrices(lam_re, lam_im, log_step, b_re, b_im, c_re, c_im):
    hi = lax.Precision.HIGHEST
    L = S5_CHUNK
    step = jnp.exp(log_step)[..., None]
    n = jnp.arange(L + 1, dtype=F32)[:, None, None, None]
    mag = jnp.exp(n * (lam_re * step)[None])
    ang = n * (lam_im * step)[None]
    p_re, p_im = mag * jnp.cos(ang), mag * jnp.sin(ang)
    ab_re, ab_im = p_re[1], p_im[1]
    den = lam_re * lam_re + lam_im * lam_im
    coef_re = ((ab_re - 1.0) * lam_re + ab_im * lam_im) / den
    coef_im = (ab_im * lam_re - (ab_re - 1.0) * lam_im) / den
    e_re = p_re[:L] * coef_re - p_im[:L] * coef_im
    e_im = p_re[:L] * coef_im + p_im[:L] * coef_re
    ce_re = c_re[None] * e_re[..., None, :] - c_im[None] * e_im[..., None, :]
    ce_im = c_re[None] * e_im[..., None, :] + c_im[None] * e_re[..., None, :]
    kern = (jnp.einsum('ndgps,dgsq->ndgpq', ce_re, b_re, precision=hi)
            - jnp.einsum('ndgps,dgsq->ndgpq', ce_im, b_im, precision=hi))
    w_re, w_im = e_re[::-1], e_im[::-1]
    q_re = (w_re[..., None] * b_re[None] - w_im[..., None] * b_im[None]).transpose(1, 2, 0, 4, 3)
    q_im = (w_re[..., None] * b_im[None] + w_im[..., None] * b_re[None]).transpose(1, 2, 0, 4, 3)
    a_re, a_im = p_re[1:], p_im[1:]
    ca_re = c_re[None] * a_re[..., None, :] - c_im[None] * a_im[..., None, :]
    ca_im = c_re[None] * a_im[..., None, :] + c_im[None] * a_re[..., None, :]
    r_re = ca_re.transpose(1, 2, 4, 0, 3)
    r_im = (-ca_im).transpose(1, 2, 4, 0, 3)
    nl = L * jnp.arange(1, 9, dtype=F32)[:, None, None, None]
    t_mag = jnp.exp(nl * (lam_re * step)[None])
    t_ang = nl * (lam_im * step)[None]
    return kern, q_re, q_im, r_re, r_im, t_mag * jnp.cos(t_ang), t_mag * jnp.sin(t_ang)


def _s5_layer(u_lat, u_ctx, params, *, bsz, seq, ctx_len):
    c = u_lat.shape[1]
    L, pp, ns = S5_CHUNK, S5_GROUP, S5_STATE
    g = c // pp
    kern, q_re, q_im, r_re, r_im, t_re, t_im = _s5_matrices(*params)
    gl = S5_LANE_TILE // pp
    nb = g // gl
    lt = S5_LANE_TILE
    kw = L * lt
    sw = gl * ns
    lanes = g * ns
    kt_ = kern.transpose(0, 1, 2, 4, 3)
    lags = jnp.concatenate([kt_[:0:-1, 1], (kt_[0, 0] + kt_[0, 1])[None], kt_[1:, 0]], axis=0)
    eye = jnp.eye(gl, dtype=F32)
    bd = (lags.reshape(2 * L - 1, nb, gl, pp, pp).transpose(1, 0, 2, 3, 4)[:, :, :, :, None, :]
          * eye[None, None, :, None, :, None]).reshape(nb, 2 * L - 1, lt, lt).astype(BF16)
    qs = jnp.stack([q_re[0], q_im[0], q_re[1][:, ::-1], q_im[1][:, ::-1]])
    qc = qs.reshape(4, nb, gl, L, pp, ns).transpose(1, 3, 2, 4, 0, 5).reshape(nb, L, lt, 4 * ns).astype(BF16)
    rs = jnp.stack([r_re[0], r_im[0], r_re[1][:, :, ::-1], r_im[1][:, :, ::-1]])
    rc = rs.reshape(4, nb, gl, ns, L, pp).transpose(1, 0, 2, 3, 4, 5).reshape(nb, 4, sw, L * pp).astype(BF16)
    pf =jnp.stack([t_re[:, 0], t_im[:, 0]]).reshape(2, 8, lanes)
    pb = jnp.stack([t_re[::-1, 1], t_im[::-1, 1]]).reshape(2, 8, lanes)

    u_spec = lambda kt: pl.BlockSpec((kt * L, S5_LANE_TILE), lambda b, i: (i, b))
    h_spec = lambda kt: pl.BlockSpec((kt, sw), lambda b, i: (i, b))
    cp2 = _cparams(("parallel", "arbitrary"))

    def state(u):
        rows = u.shape[0] // L
        kt = min(S5_ROW_TILE, rows)
        return pl.pallas_call(
            _s5_state_kernel,
            out_shape=[jax.ShapeDtypeStruct((rows, lanes), F32)] * 4,
            grid=(nb, rows // kt),
            in_specs=[u_spec(kt), pl.BlockSpec((None, L, lt, 4 * ns), lambda b, i: (b, 0, 0, 0))],
            out_specs=[h_spec(kt)] * 4,
            scratch_shapes=[pltpu.VMEM((kw, 4 * sw), BF16)],
            compiler_params=cp2,
            name="s5_state",
        )(u, qc)

    z_c, z_l = state(u_ctx), state(u_lat)
    lb = S5_LANES
    rows_c, rows_l = z_c[0].shape[0], z_l[0].shape[0]
    h = pl.pallas_call(
        functools.partial(_s5_rec_kernel, bsz=bsz, nc_tiles=rows_c // bsz // 8, nl_tiles=rows_l // bsz // 8),
        out_shape=[jax.ShapeDtypeStruct((rows_c, lanes), F32)] * 4 + [jax.ShapeDtypeStruct((rows_l, lanes), F32)] * 4,
        grid=(lanes // lb,),
        in_specs=[pl.BlockSpec((rows_c, lb), lambda j: (0, j))] * 4
        + [pl.BlockSpec((rows_l, lb), lambda j: (0, j))] * 4
        + [pl.BlockSpec((2, 8, lb), lambda j: (0, 0, j))] * 2,
        out_specs=[pl.BlockSpec((rows_c, lb), lambda j: (0, j))] * 4
        + [pl.BlockSpec((rows_l, lb), lambda j: (0, j))] * 4,
        compiler_params=_cparams(("parallel",)),
        name="s5_recurrence",
    )(*z_c, *z_l, pf, pb)

    def readout(u, hs):
        rows = u.shape[0] // L
        kt = min(S5_ROW_TILE, rows)
        return pl.pallas_call(
            _s5_out_kernel,
            out_shape=jax.ShapeDtypeStruct(u.shape, F32),
            grid=(nb, rows // kt),
            in_specs=[u_spec(kt)] + [h_spec(kt)] * 4
            + [pl.BlockSpec((None, 2 * L - 1, lt, lt), lambda b, i: (b, 0, 0, 0)),
               pl.BlockSpec((None, 4, sw, L * pp), lambda b, i: (b, 0, 0, 0))],
            out_specs=u_spec(kt),
            scratch_shapes=[pltpu.VMEM((kw, kw), BF16), pltpu.VMEM((4, sw, kw), BF16)],
            compiler_params=cp2,
            name="s5_out",
        )(u, *hs, bd, rc)

    return readout(u_lat, h[4:]), readout(u_ctx, h[:4])


def _head_norm(o, g_ref, center, dh):
    outs = []
    for h in range(HEADS):
        oh = o[:, h * dh:(h + 1) * dh]
        if center:
            oh = oh - jnp.mean(oh, axis=-1, keepdims=True)
        outs.append(oh * lax.rsqrt(jnp.mean(oh * oh, axis=-1, keepdims=True) + EPS))
    return jnp.concatenate(outs, axis=-1) * g_ref[...]


def _finish(x_ref, a, b, wout_ref, mod_ref, gpost_ref, o_ref, n_r):
    half = a.shape[1]
    b = b.astype(BF16)
    if n_r:
        b = _dot(_perm_matrix(b.shape[0], n_r, False), b).astype(BF16)
    y = _dot(a.astype(BF16), wout_ref[0:half, :]) + _dot(b, wout_ref[half:2 * half, :])
    out = _tile2d(x_ref) + mod_ref[5:6, :] * _rms(y) * gpost_ref[1:2, :]
    o_ref[...] = out.reshape(o_ref.shape)


def _even_merge_kernel(x_ref, ga_ref, oaf_ref, oab_ref, gb_ref, obf_ref, obb_ref, na_ref, nb_ref,
                       wout_ref, mod_ref, gpost_ref, o_ref, *, dh, n_r):
    a = _silu(_tile2d(ga_ref)) * _head_norm(_tile2d(oaf_ref) + _tile2d(oab_ref), na_ref, False, dh)
    b = _sigmoid(_tile2d(gb_ref)) * _head_norm(_tile2d(obf_ref) + _tile2d(obb_ref), nb_ref, True, dh)
    _finish(x_ref, a, b, wout_ref, mod_ref, gpost_ref, o_ref, n_r)


def _odd_merge_kernel(x_ref, u_ref, ys_ref, gb_ref, obf_ref, obb_ref, sd_ref, wglu_ref, bglu_ref, nb_ref,
                      wout_ref, mod_ref, gpost_ref, o_ref, *, dh, n_r):
    y = _tile2d(ys_ref) + sd_ref[...] * _tile2d(u_ref)
    g = jax.nn.gelu(y)
    a = g * _sigmoid(_dot(g.astype(BF16), wglu_ref[...]) + bglu_ref[...])
    b = _silu(_tile2d(gb_ref)) * _head_norm(_tile2d(obf_ref) + _tile2d(obb_ref), nb_ref, True, dh)
    _finish(x_ref, a, b, wout_ref, mod_ref, gpost_ref, o_ref, n_r)


def _merge(kernel, s, rm_lat, cm_lat, rm_ctx, cm_ctx, consts, w_out, mod, gpost, *, bsz, seq, ctx_len, name):
    d = s.shape[1]
    half = d // 2
    rows = seq // GRID_W
    rt = min(MERGE_ROWS, rows)
    n_rh = rows // rt
    ncb = GRID_W // COL_TILE
    per_b = n_rh * ncb
    const_specs = [pl.BlockSpec(a.shape, lambda i, nd=a.ndim: (0,) * nd) for a in consts] \
        + [pl.BlockSpec((d, d), lambda i: (0, 0))]
    gpost_spec = pl.BlockSpec((3, d), lambda i: (0, 0))
    cp = _cparams(("parallel",))

    rm_idx = lambda i: ((i // per_b) * n_rh + (i % per_b) // ncb, i % ncb)
    cm_idx = lambda i: ((i // per_b) * ncb + i % ncb, (i % per_b) // ncb)
    s3 = s.reshape(s.shape[0] // GRID_W, GRID_W, d)
    ins = [s3] + [a.reshape(bsz * rows, GRID_W, a.shape[1]) for a, _ in rm_lat] \
        + [a.reshape(bsz * GRID_W, rows, a.shape[1]) for a, _ in cm_lat]
    in_specs = [pl.BlockSpec((rt, COL_TILE, d), lambda i: rm_idx(i) + (0,))] \
        + [pl.BlockSpec((rt, COL_TILE, half), lambda i, cb=cb: rm_idx(i) + (cb,)) for _, cb in rm_lat] \
        + [pl.BlockSpec((COL_TILE, rt, half), lambda i, cb=cb: cm_idx(i) + (cb,)) for _, cb in cm_lat]
    n_out_groups = s3.shape[0] if rm_ctx is not None else bsz * rows
    out = pl.pallas_call(
        functools.partial(kernel, n_r=rt),
        out_shape=jax.ShapeDtypeStruct((n_out_groups, GRID_W, d), F32),
        grid=(bsz * per_b,),
        in_specs=in_specs + const_specs
        + [pl.BlockSpec((None, N_MOD, d), lambda i: (i // per_b, 0, 0)), gpost_spec],
        out_specs=pl.BlockSpec((rt, COL_TILE, d), lambda i: rm_idx(i) + (0,)),
        compiler_params=cp,
        name=name + "_lat",
    )(*ins, *consts, w_out, mod, gpost)
    out = out.reshape(n_out_groups * GRID_W, d)
    if rm_ctx is None:
        return out

    tm = rt * COL_TILE
    t0 = bsz * seq // tm
    tiles = list(rm_ctx) + list(cm_ctx)
    n_in = 1 + len(tiles) + len(consts) + 3
    inner = functools.partial(kernel, n_r=0)
    return pl.pallas_call(
        lambda *refs: inner(*refs[:n_in], refs[n_in + 1]),
        out_shape=jax.ShapeDtypeStruct(out.shape, F32),
        grid=(bsz * ctx_len // tm,),
        in_specs=[pl.BlockSpec((tm, d), lambda i: (t0 + i, 0))]
        + [pl.BlockSpec((tm, half), lambda i, cb=cb: (i, cb)) for _, cb in tiles] + const_specs
        + [pl.BlockSpec((None, N_MOD, d), lambda i: (bsz, 0, 0)), gpost_spec,
           pl.BlockSpec(memory_space=pl.ANY)],
        out_specs=pl.BlockSpec((tm, d), lambda i: (t0 + i, 0)),
        input_output_aliases={n_in: 0},
        compiler_params=cp,
        name=name + "_ctx",
    )(s, *[a for a, _ in tiles], *consts, w_out, mod, gpost, out)


def kernel(x, c, ctx, c_ctx, w_mod, b_mod, norm_pre, norm_post, ffn_w_gate, ffn_w_up, ffn_w_down, ev_w_in, ev_w_out, gla_w_gate, gla_b_gate, gla_norm, ml_conv_w, ml_conv_b, ml_b_gates, ml_norm, od_w_in, od_w_out, s5_lam_re, s5_lam_im, s5_log_step, s5_b_re, s5_b_im, s5_c_re, s5_c_im, s5_d, s5_w_glu, s5_b_glu, ret_log_decay, ret_norm):
    bsz, seq, d = x.shape
    ctx_len = ctx.shape[1]
    depth = w_mod.shape[0]
    half = d // 2
    dh = half // HEADS
    gla_dk = dh // 2
    d_ff = ffn_w_gate.shape[-1]
    tm = FFN_TILE
    rows = seq // GRID_W
    assert seq % tm == 0 and (bsz * ctx_len) % tm == 0 and seq == rows * GRID_W and ctx_len % CHUNK == 0
    assert 2 * bsz == 8 and rows % 8 == 0 and seq % CHUNK == 0 and half == PROJ_TILE
    assert ctx_len % WIDE_CHUNK == 0 and seq % WIDE_CHUNK == 0
    assert ctx_len % (8 * S5_CHUNK) == 0 and seq % (8 * S5_CHUNK) == 0 and bsz % SCAN_BATCH == 0
    assert all((n // S5_CHUNK) % min(S5_ROW_TILE, n // S5_CHUNK) == 0 for n in (bsz * ctx_len, bsz * seq))

    tiles_x = bsz * seq // tm
    tiles_all = tiles_x + bsz * ctx_len // tm
    tiles_per_batch = seq // tm
    mod_of_tile = lambda i: jnp.minimum(i // tiles_per_batch, bsz)

    wg, wu, wd = ffn_w_gate.astype(BF16), ffn_w_up.astype(BF16), ffn_w_down.astype(BF16)

    cc = jnp.concatenate([c, c_ctx[None], jnp.zeros((8 - bsz - 1, d), F32)], axis=0)
    mod_all = _modulation(cc, w_mod, b_mod).reshape(depth, 8, N_MOD, d)

    s = None
    dims = dict(bsz=bsz, seq=seq, ctx_len=ctx_len)

    for layer in range(depth):
        need_ctx = layer < depth - 1
        mod = mod_all[layer]
        gpre, gpost = norm_pre[layer], norm_post[layer]
        ffn = functools.partial(_ffn, mod=mod, gpre=gpre, gpost=gpost, wg=wg, wu=wu, wd=wd, layer=layer,
                                mod_of_tile=mod_of_tile)
        if s is None:
            s = ffn(x.reshape(bsz * seq, d), which=0, sub=0, n_tiles=tiles_x, out_tiles=tiles_all)
            s = ffn(ctx.reshape(bsz * ctx_len, d), which=0, sub=0, n_tiles=tiles_all - tiles_x, out_tile0=tiles_x,
                    out_tiles=tiles_all, prev=s)
        else:
            s = ffn(s, which=0, sub=0, n_tiles=tiles_all)

        if layer % 2 == 0:
            e = layer // 2
            w = ev_w_in[e]
            qk_w = 2 * HEADS * gla_dk
            n_gate = 4 * HEADS
            c_low = qk_w + 2 * half
            c_ml = c_low + 2 * GLA_RANK
            c_g = c_ml + 4 * half
            w_in = jnp.concatenate([w[:, :c_low], w[:, c_ml:c_g]], axis=1).astype(BF16)
            w_small = jnp.concatenate([w[:, c_low:c_ml], jnp.zeros((d, SMALL_W - 2 * GLA_RANK), F32),
                                       w[:, c_g:], jnp.zeros((d, SMALL_W - n_gate), F32)], axis=1).astype(BF16)
            (pa, pb, psa, psb), (pa_c, pb_c, psa_c, psb_c) = _proj(s, mod, gpre, w_in, w_small,
                                                                   n_a=c_low // PROJ_TILE, **dims)

            st0 = [jnp.zeros((bsz, 2, HEADS, dh, gla_dk), F32)]
            gla_ins = lambda p, ps: [(p, qk_w // 2, 0), (p, qk_w // 2, qk_w // 2), (p, half, qk_w), (ps, SMALL_W, 0)]
            oa, oa_c = _bidir_scan(
                functools.partial(_gla_kernel, dk=gla_dk, dv=dh, gated=True,
                                  q_scale=float(gla_dk) ** -0.5, k_scale=1.0), CHUNK,
                gla_ins(pa, psa), gla_ins(pa_c, psa_c), [gla_w_gate[e].astype(BF16), gla_b_gate[e]], st0,
                out_width=half, name="gla", **dims)

            scale = jnp.concatenate([jnp.ones((half,), F32), jnp.full((half,), float(dh) ** -0.5, F32)])
            qk = _conv(pb, ml_conv_w[e], ml_conv_b[e], scale, bsz=bsz, n=seq)
            qk_c = _conv(pb_c, ml_conv_w[e], ml_conv_b[e], scale, bsz=bsz, n=ctx_len)
            bias = jnp.zeros((1, SMALL_W), F32).at[0, :n_gate].set(ml_b_gates[e].reshape(-1))
            ml_ins = lambda qk_, p, ps: [(qk_, half, 0), (qk_, half, half), (p, half, 2 * half), (ps, SMALL_W, 0)]
            st0 = [jnp.zeros((bsz, 2, HEADS, dh, dh), F32), jnp.zeros((bsz, 2, HEADS, 8, dh), F32)]
            ob, ob_c = _bidir_scan(
                functools.partial(_mlstm_kernel, dh=dh), WIDE_CHUNK,
                ml_ins(qk, pb, psb), ml_ins(qk_c, pb_c, psb_c), [bias], st0,
                out_width=half, name="mlstm", out_dtype=BF16, **dims)

            ga_cb, gb_cb = (qk_w + half) // half, 3
            s = _merge(
                functools.partial(_even_merge_kernel, dh=dh), s,
                [(pa, ga_cb), (oa[0], 0), (oa[1], 0)], [(pb, gb_cb), (ob[0], 0), (ob[1], 0)],
                [(pa_c, ga_cb), (oa_c[0], 0), (oa_c[1], 0)] if need_ctx else None,
                [(pb_c, gb_cb), (ob_c[0], 0), (ob_c[1], 0)],
                [gla_norm[e].reshape(1, half), ml_norm[e].reshape(1, half)],
                ev_w_out[e].astype(BF16), mod, gpost, name="even_merge", **dims)
        else:
            o = layer // 2
            (pa, pb), (pa_c, pb_c) = _proj(s, mod, gpre, od_w_in[o].astype(BF16), None, n_a=1, **dims)

            ys, ys_c = _s5_layer(pa, pa_c, (s5_lam_re[o], s5_lam_im[o], s5_log_step[o], s5_b_re[o], s5_b_im[o],
                                            s5_c_re[o], s5_c_im[o]), **dims)

            lg = jnp.repeat(-jnp.exp(ret_log_decay[o]), dh, axis=1)
            ret_ins = lambda p: [(p, half, 0), (p, half, half), (p, half, 2 * half)]
            st0 = [jnp.zeros((bsz, 2, HEADS, dh, dh), F32)]
            od, od_c = _bidir_scan(
                functools.partial(_gla_kernel, dk=dh, dv=dh, gated=False,
                                  q_scale=1.0, k_scale=float(dh) ** -0.5), WIDE_CHUNK,
                ret_ins(pb), ret_ins(pb_c), [lg], st0, out_width=half, name="retention", out_dtype=BF16, **dims)

            s = _merge(
                functools.partial(_odd_merge_kernel, dh=dh), s,
                [(pa, 0), (ys, 0)], [(pb, 3), (od[0], 0), (od[1], 0)],
                [(pa_c, 0), (ys_c, 0)] if need_ctx else None, [(pb_c, 3), (od_c[0], 0), (od_c[1], 0)],
                [s5_d[o].reshape(1, half), s5_w_glu[o].astype(BF16), s5_b_glu[o].reshape(1, half),
                 ret_norm[o].reshape(1, half)],
                od_w_out[o].astype(BF16), mod, gpost, name="odd_merge", **dims)

        s = ffn(s, which=1, sub=2, n_tiles=tiles_all if need_ctx else tiles_x)

    return s[:bsz * seq].reshape(bsz, seq, d)
```

```python
import functools

import jax
import jax.numpy as jnp
from jax import lax
from jax.experimental import pallas as pl
from jax.experimental.pallas import tpu as pltpu

F32 = jnp.float32
BF16 = jnp.bfloat16

EPS = 1e-6
GRID_W = 64
COL_TILE = 8
CHUNK = 64
WIDE_CHUNK = 128
SCAN_BATCH = 2
FFN_RES = 0.5
N_MOD = 9
HEADS = 4
GLA_RANK = 16
GLA_GATE_NORM = 16.0
S5_GROUP = 16
S5_STATE = 64
S5_CHUNK = 8
S5_LANE_TILE = 128
S5_ROW_TILE = 512
SMALL_W = 128

TOKEN_TILE = 512
MERGE_ROWS = 32
MERGE_SPLIT = 2
FF_TILE = 1024
PROJ_TILE = 1024
MOD_TILE = 1024
CONV_TILE = 256
CONV_ROWS = 256
S5_LANES = 256
VMEM_LIMIT = 56 * 1024 * 1024


def _cparams(sem):
    return pltpu.CompilerParams(dimension_semantics=sem, vmem_limit_bytes=VMEM_LIMIT)


def _rms(x):
    return x * lax.rsqrt(jnp.mean(x * x, axis=-1, keepdims=True) + EPS)


def _sigmoid(x):
    return 1.0 / (1.0 + jnp.exp(-x))


def _silu(x):
    return x * _sigmoid(x)


def _log_sigmoid(x):
    return jnp.minimum(x, 0.0) - jnp.log(1.0 + jnp.exp(-jnp.abs(x)))


def _dot(a, b):
    return jnp.dot(a, b, preferred_element_type=F32)


def _dot_nt(a, b):
    return lax.dot_general(a, b, (((1,), (1,)), ((), ())), preferred_element_type=F32)


def _dot_tn(a, b):
    return lax.dot_general(a, b, (((0,), (0,)), ((), ())), preferred_element_type=F32)


def _dot01(t01, x):
    x0 = x.astype(BF16)
    r1 = x - x0.astype(F32)
    x1 = r1.astype(BF16)
    x2 = (r1 - x1.astype(F32)).astype(BF16)
    return _dot(t01, x0) + _dot(t01, x1) + _dot(t01, x2)


def _perm_matrix(n, n_r, to_col_major):
    i = lax.broadcasted_iota(jnp.int32, (n, n), 0)
    c = lax.broadcasted_iota(jnp.int32, (n, n), 1)
    if to_col_major:
        hit = (i // n_r == c % COL_TILE) & (i % n_r == c // COL_TILE)
    else:
        hit = (i // COL_TILE == c % n_r) & (i % COL_TILE == c // n_r)
    return jnp.where(hit, 1.0, 0.0).astype(BF16)


def _tile2d(ref):
    v = ref[...]
    return v.reshape(-1, v.shape[-1]).astype(F32)


def _mod_kernel(c_ref, w_ref, b_ref, o_ref):
    a = _silu(c_ref[...]).astype(BF16)
    o_ref[...] = _dot(a, w_ref[...].astype(BF16)) + b_ref[...]


def _modulation(cc, w_mod, b_mod):
    depth, d, nm = w_mod.shape
    return pl.pallas_call(
        _mod_kernel,
        out_shape=jax.ShapeDtypeStruct((depth, 8, nm), F32),
        grid=(depth, nm // MOD_TILE),
        in_specs=[pl.BlockSpec((8, d), lambda l, j: (0, 0)),
                  pl.BlockSpec((None, d, MOD_TILE), lambda l, j: (l, 0, j)),
                  pl.BlockSpec((None, 1, MOD_TILE), lambda l, j: (l, 0, j))],
        out_specs=pl.BlockSpec((None, 8, MOD_TILE), lambda l, j: (l, 0, j)),
        compiler_params=_cparams(("parallel", "parallel")),
        name="modulation",
    )(cc, w_mod, b_mod.reshape(depth, 1, nm))


def _ffn_kernel(x_ref, mod_ref, gpre_ref, gpost_ref, wg_ref, wu_ref, wd_ref, o_ref, h_ref, acc_ref, *, sub, nff,
                tail):
    j = pl.program_id(1)

    @pl.when(j == 0)
    def _():
        shift = mod_ref[3 * sub:3 * sub + 1, :]
        scale = mod_ref[3 * sub + 1:3 * sub + 2, :]
        h = _rms(x_ref[...]) * gpre_ref[sub:sub + 1, :] * (1.0 + scale) + shift
        h_ref[...] = h.astype(BF16)

    def hidden_step(valid, first):
        h = h_ref[...]
        t = _silu(_dot(h, wg_ref[:, :valid])) * _dot(h, wu_ref[:, :valid])
        y = _dot(t.astype(BF16), wd_ref[:valid, :])
        if first:
            acc_ref[...] = y
        else:
            acc_ref[...] += y

    pl.when(j == 0)(lambda: hidden_step(tail or FF_TILE, True))
    pl.when(j > 0)(lambda: hidden_step(FF_TILE, False))

    @pl.when(j == nff - 1)
    def _():
        gate = mod_ref[3 * sub + 2:3 * sub + 3, :]
        o_ref[...] = x_ref[...] + FFN_RES * gate * _rms(acc_ref[...]) * gpost_ref[sub:sub + 1, :]


def _ffn(s, mod, gpre, gpost, wg, wu, wd, *, layer, which, sub, n_tiles, mod_of_tile, out_tile0=0, out_tiles=None,
         prev=None):
    d = s.shape[1]
    d_ff = wg.shape[-1]
    nff = pl.cdiv(d_ff, FF_TILE)
    assert nff >= 2 and (d_ff % FF_TILE) % 128 == 0
    tm = TOKEN_TILE
    out_tiles = n_tiles if out_tiles is None else out_tiles
    kernel = functools.partial(_ffn_kernel, sub=sub, nff=nff, tail=d_ff % FF_TILE)
    blk = lambda j: (j + nff - 1) % nff
    ins = [s, mod, gpre, gpost, wg, wu, wd]
    in_specs = [pl.BlockSpec((tm, d), lambda i, j: (i, 0)),
                pl.BlockSpec((None, N_MOD, d), lambda i, j: (mod_of_tile(out_tile0 + i), 0, 0)),
                pl.BlockSpec((3, d), lambda i, j: (0, 0)),
                pl.BlockSpec((3, d), lambda i, j: (0, 0)),
                pl.BlockSpec((None, None, d, FF_TILE), lambda i, j: (layer, which, 0, blk(j))),
                pl.BlockSpec((None, None, d, FF_TILE), lambda i, j: (layer, which, 0, blk(j))),
                pl.BlockSpec((None, None, FF_TILE, d), lambda i, j: (layer, which, blk(j), 0))]
    aliases = {}
    if prev is not None:
        n_in = len(ins)
        inner = kernel
        kernel = lambda *refs: inner(*refs[:n_in], *refs[n_in + 1:])
        ins.append(prev)
        in_specs.append(pl.BlockSpec(memory_space=pl.ANY))
        aliases = {n_in: 0}
    return pl.pallas_call(
        kernel,
        out_shape=jax.ShapeDtypeStruct((out_tiles * tm, d), F32),
        grid=(n_tiles, nff),
        in_specs=in_specs,
        out_specs=pl.BlockSpec((tm, d), lambda i, j: (out_tile0 + i, 0)),
        scratch_shapes=[pltpu.VMEM((tm, d), BF16), pltpu.VMEM((tm, d), F32)],
        input_output_aliases=aliases,
        compiler_params=_cparams(("parallel", "arbitrary")),
        name="ffn",
    )(*ins)


def _proj_kernel(*refs, n_a, n_r, has_small):
    if has_small:
        x_ref, mod_ref, gpre_ref, w_ref, ws_ref, pa_ref, pb_ref, psa_ref, psb_ref, h_ref, hp_ref = refs
    else:
        x_ref, mod_ref, gpre_ref, w_ref, pa_ref, pb_ref, h_ref, hp_ref = refs
    j = pl.program_id(1)
    tm = h_ref.shape[0]

    @pl.when(j == 0)
    def _():
        x = _tile2d(x_ref)
        h = (_rms(x) * gpre_ref[1:2, :] * (1.0 + mod_ref[4:5, :]) + mod_ref[3:4, :]).astype(BF16)
        hp = _dot(_perm_matrix(tm, n_r, True), h).astype(BF16) if n_r else h
        h_ref[...] = h
        hp_ref[...] = hp
        if has_small:
            psa_ref[...] = _dot(h, ws_ref[:, :SMALL_W]).reshape(psa_ref.shape)
            psb_ref[...] = _dot(hp, ws_ref[:, SMALL_W:])

    @pl.when(j < n_a)
    def _():
        pa_ref[...] = _dot(h_ref[...], w_ref[j]).reshape(pa_ref.shape)

    @pl.when(j >= n_a)
    def _():
        pb_ref[...] = _dot(hp_ref[...], w_ref[j])


def _proj(s, mod, gpre, w, w_small, *, n_a, bsz, seq, ctx_len):
    d = s.shape[1]
    rows = seq // GRID_W
    tn = PROJ_TILE
    n_tiles_n = w.shape[1] // tn
    n_b = n_tiles_n - n_a
    has_small = w_small is not None
    w_ins = [w.reshape(d, n_tiles_n, tn).transpose(1, 0, 2)] + ([w_small] if has_small else [])
    w_specs = [pl.BlockSpec((n_tiles_n, d, tn), lambda i, j: (0, 0, 0), pipeline_mode=pl.Buffered(1))]
    if has_small:
        w_specs.append(pl.BlockSpec((d, 2 * SMALL_W), lambda i, j: (0, 0)))
    ja = lambda j: jnp.minimum(j, n_a - 1)
    jb = lambda j: jnp.maximum(j - n_a, 0)
    cp = _cparams(("parallel", "arbitrary"))

    tm = rows * COL_TILE
    ncb = GRID_W // COL_TILE
    s3 = s.reshape(s.shape[0] // GRID_W, GRID_W, d)
    out_shape = [jax.ShapeDtypeStruct((bsz * rows, GRID_W, n_a * tn), F32),
                 jax.ShapeDtypeStruct((bsz * seq, n_b * tn), F32)]
    out_specs = [pl.BlockSpec((rows, COL_TILE, tn), lambda i, j: (i // ncb, i % ncb, ja(j))),
                 pl.BlockSpec((tm, tn), lambda i, j: (i, jb(j)))]
    if has_small:
        out_shape += [jax.ShapeDtypeStruct((bsz * rows, GRID_W, SMALL_W), F32),
                      jax.ShapeDtypeStruct((bsz * seq, SMALL_W), F32)]
        out_specs += [pl.BlockSpec((rows, COL_TILE, SMALL_W), lambda i, j: (i // ncb, i % ncb, 0)),
                      pl.BlockSpec((tm, SMALL_W), lambda i, j: (i, 0))]
    lat = pl.pallas_call(
        functools.partial(_proj_kernel, n_a=n_a, n_r=rows, has_small=has_small),
        out_shape=out_shape,
        grid=(bsz * ncb, n_tiles_n),
        in_specs=[pl.BlockSpec((rows, COL_TILE, d), lambda i, j: (i // ncb, i % ncb, 0)),
                  pl.BlockSpec((None, N_MOD, d), lambda i, j: (i // ncb, 0, 0)),
                  pl.BlockSpec((3, d), lambda i, j: (0, 0))] + w_specs,
        out_specs=out_specs,
        scratch_shapes=[pltpu.VMEM((tm, d), BF16), pltpu.VMEM((tm, d), BF16)],
        compiler_params=cp,
        name="in_proj_lat",
    )(s3, mod, gpre, *w_ins)
    lat = [lat[0].reshape(bsz * seq, n_a * tn), lat[1]] \
        + ([lat[2].reshape(bsz * seq, SMALL_W), lat[3]] if has_small else [])

    tc = TOKEN_TILE
    n_c = bsz * ctx_len
    t0 = bsz * seq // tc
    out_shape = [jax.ShapeDtypeStruct((n_c, n_a * tn), F32), jax.ShapeDtypeStruct((n_c, n_b * tn), F32)]
    out_specs = [pl.BlockSpec((tc, tn), lambda i, j: (i, ja(j))), pl.BlockSpec((tc, tn), lambda i, j: (i, jb(j)))]
    if has_small:
        out_shape += [jax.ShapeDtypeStruct((n_c, SMALL_W), F32)] * 2
        out_specs += [pl.BlockSpec((tc, SMALL_W), lambda i, j: (i, 0))] * 2
    ctx = pl.pallas_call(
        functools.partial(_proj_kernel, n_a=n_a, n_r=0, has_small=has_small),
        out_shape=out_shape,
        grid=(n_c // tc, n_tiles_n),
        in_specs=[pl.BlockSpec((tc, d), lambda i, j: (t0 + i, 0)),
                  pl.BlockSpec((None, N_MOD, d), lambda i, j: (bsz, 0, 0)),
                  pl.BlockSpec((3, d), lambda i, j: (0, 0))] + w_specs,
        out_specs=out_specs,
        scratch_shapes=[pltpu.VMEM((tc, d), BF16), pltpu.VMEM((tc, d), BF16)],
        compiler_params=cp,
        name="in_proj_ctx",
    )(s, mod, gpre, *w_ins)
    return lat, list(ctx)


def _conv_kernel(x_ref, w_ref, b_ref, sc_ref, o_ref):
    n, ct = x_ref.shape
    ch = CONV_ROWS
    n_ch = n // ch
    r = lax.broadcasted_iota(jnp.int32, (ch, ct), 0)

    def body(c, carry):
        r0 = pl.multiple_of(c * ch, ch)
        x = x_ref[pl.ds(r0, ch), :]
        before = x_ref[pl.ds(pl.multiple_of(jnp.maximum(r0 - 8, 0), 8), 8), :][7:8, :]
        after = x_ref[pl.ds(pl.multiple_of(jnp.minimum(r0 + ch, n - 8), 8), 8), :][0:1, :]
        before = jnp.where(c == 0, 0.0, before)
        after = jnp.where(c == n_ch - 1, 0.0, after)
        prev = jnp.where(r == 0, before, pltpu.roll(x, 1, 0))
        nxt = jnp.where(r == ch - 1, after, pltpu.roll(x, ch - 1, 0))
        y = w_ref[1:2, :] * x + w_ref[0:1, :] * prev + w_ref[2:3, :] * nxt + b_ref[...]
        o_ref[pl.ds(r0, ch), :] = _silu(y) * sc_ref[...]
        return carry

    lax.fori_loop(0, n_ch, body, 0)


def _conv(p, conv_w, conv_b, scale, *, bsz, n):
    c = conv_w.shape[1]
    ct = CONV_TILE
    return pl.pallas_call(
        _conv_kernel,
        out_shape=jax.ShapeDtypeStruct((bsz * n, c), F32),
        grid=(bsz, c // ct),
        in_specs=[pl.BlockSpec((n, ct), lambda b, j: (b, j)),
                  pl.BlockSpec((3, ct), lambda b, j: (0, j)),
                  pl.BlockSpec((1, ct), lambda b, j: (0, j)),
                  pl.BlockSpec((1, ct), lambda b, j: (0, j))],
        out_specs=pl.BlockSpec((n, ct), lambda b, j: (b, j)),
        compiler_params=_cparams(("parallel", "parallel")),
        name="conv",
    )(p, conv_w, conv_b.reshape(1, c), scale.reshape(1, c))


def _gla_kernel(*refs, L, dk, dv, gated, q_scale, k_scale):
    if gated:
        qf, kf, vf, df, qb, kb, vb, db, wgate_ref, bgate_ref, st_in, of, ob, st_out = refs
    else:
        qf, kf, vf, qb, kb, vb, lg_ref, st_in, of, ob, st_out = refs
        df = db = None

    @pl.when(pl.program_id(1) == 0)
    def _():
        st_out[...] = st_in[...]

    row = lax.broadcasted_iota(jnp.int32, (L, L), 0)
    col = lax.broadcasted_iota(jnp.int32, (L, L), 1)
    trow = lax.broadcasted_iota(jnp.int32, (L, 1), 0)
    for d, (q_ref, k_ref, v_ref, d_ref, o_ref) in enumerate(((qf, kf, vf, df, of), (qb, kb, vb, db, ob))):
        mask = (row >= col) if d == 0 else (row <= col)
        mid = L // 2 if d == 0 else L - 1 - L // 2
        end = L - 1 if d == 0 else 0
        if gated:
            low = d_ref[:, d * GLA_RANK:(d + 1) * GLA_RANK].astype(BF16)
            z = _dot(low, wgate_ref[d]) + bgate_ref[d:d + 1, :]
            la = _log_sigmoid(z) * (1.0 / GLA_GATE_NORM)
            bcum = _dot01(mask.astype(BF16), la)
        else:
            cnt = (trow + 1) if d == 0 else (L - trow)
            bcum = cnt.astype(F32) * lg_ref[d:d + 1, :]
            dist = jnp.abs(row - col).astype(F32)
        for h in range(HEADS):
            b = bcum[:, h * dk:(h + 1) * dk]
            b_end = b[end:end + 1, :]
            q = q_ref[:, h * dk:(h + 1) * dk] * q_scale
            k = k_ref[:, h * dk:(h + 1) * dk] * k_scale
            v = v_ref[:, h * dv:(h + 1) * dv].astype(BF16)
            if gated:
                b_mid = b[mid:mid + 1, :]
                qs = (q * jnp.exp(b - b_mid)).astype(BF16)
                ks = (k * jnp.exp(b_mid - b)).astype(BF16)
                scores = jnp.where(mask, _dot_nt(qs, ks), 0.0)
            else:
                lg_h = lg_ref[d:d + 1, h * dk:h * dk + 1]
                scores = _dot_nt(q.astype(BF16), k.astype(BF16)) * jnp.where(mask, jnp.exp(dist * lg_h), 0.0)
            st = st_out[d, h]
            o = _dot(scores.astype(BF16), v) + _dot_nt((q * jnp.exp(b)).astype(BF16), st.astype(BF16))
            o_ref[:, h * dv:(h + 1) * dv] = o.astype(o_ref.dtype)
            kd = (k * jnp.exp(b_end - b)).astype(BF16)
            st_out[d, h] = jnp.exp(b_end) * st + _dot_tn(v, kd)


def _mlstm_kernel(qf, kf, vf, gf, qb, kb, vb, gb, bias_ref, c_in, nm_in, of, ob, c_out, nm_out, *, L, dh):
    @pl.when(pl.program_id(1) == 0)
    def _():
        c_out[...] = c_in[...]
        nm_out[...] = nm_in[...]

    row = lax.broadcasted_iota(jnp.int32, (L, L), 0)
    col = lax.broadcasted_iota(jnp.int32, (L, L), 1)
    eye = row == col
    ones = jnp.ones((L, L), BF16)
    for d, (q_ref, k_ref, v_ref, g_ref, o_ref) in enumerate(((qf, kf, vf, gf, of), (qb, kb, vb, gb, ob))):
        mask = (row >= col) if d == 0 else (row <= col)
        cum_us = (row <= col) if d == 0 else (row >= col)
        end = L - 1 if d == 0 else 0
        g = g_ref[...] + bias_ref[...]
        fl = _log_sigmoid(g)
        bc_all = _dot01(mask.astype(BF16), fl)
        for h in range(HEADS):
            ci = d * 2 * HEADS + h
            cf = ci + HEADS
            i_col = g[:, ci:ci + 1]
            f_col = fl[:, cf:cf + 1]
            b_col = bc_all[:, cf:cf + 1]
            x = jnp.where(cum_us, f_col, 0.0) - jnp.where(eye, i_col, 0.0)
            drow = _dot01(ones, x)
            m_st = nm_out[d, h, 1:2, 0:1]
            dmat = jnp.where(mask, b_col - drow, -jnp.inf)
            inter = b_col + m_st
            m_t = jnp.maximum(inter, jnp.max(dmat, axis=-1, keepdims=True))
            q = q_ref[:, h * dh:(h + 1) * dh]
            k = k_ref[:, h * dh:(h + 1) * dh]
            v = v_ref[:, h * dh:(h + 1) * dh].astype(BF16)
            qh = q.astype(BF16)
            scores = _dot_nt(qh, k.astype(BF16)) * jnp.exp(dmat - m_t)
            w_inter = jnp.exp(inter - m_t)
            c_st = c_out[d, h]
            n_st = nm_out[d, h, 0:1, :]
            num = _dot(scores.astype(BF16), v) + w_inter * _dot(qh, c_st.astype(BF16))
            den = (jnp.sum(scores, axis=-1, keepdims=True)
                   + w_inter * jnp.sum(q * n_st, axis=-1, keepdims=True))
            hout = num / jnp.maximum(jnp.abs(den), jnp.exp(-m_t))
            o_ref[:, h * dh:(h + 1) * dh] = hout.astype(o_ref.dtype)
            b_end = b_col[end:end + 1, :]
            d_end = b_end - b_col + i_col
            m_new = jnp.maximum(b_end + m_st, jnp.max(d_end, axis=0, keepdims=True))
            kw = k * jnp.exp(d_end - m_new)
            decay = jnp.exp(b_end + m_st - m_new)
            c_out[d, h] = decay * c_st + _dot_tn(kw.astype(BF16), v)
            nm_out[d, h, 0:1, :] = decay * n_st + jnp.sum(kw, axis=0, keepdims=True)
            nm_out[d, h, 1:2, :] = jnp.broadcast_to(m_new, (1, dh))


def _scan_call(kernel, bsz, n, L, chunk_ins, consts, states, out_width, name, out_dtype=F32):
    bg = SCAN_BATCH
    chunk = (lambda s: s, lambda s: n - 1 - s)
    ins, in_specs, batched = [], [], []
    for d in (0, 1):
        for a, width, coloff in chunk_ins:
            ins.append(a.reshape(bsz, n * L, a.shape[1]))
            in_specs.append(pl.BlockSpec((bg, L, width), lambda b, s, d=d, cb=coloff // width: (b, chunk[d](s), cb)))
            batched.append(True)
    for a in consts:
        ins.append(a)
        in_specs.append(pl.BlockSpec(a.shape, lambda b, s, nd=a.ndim: (0,) * nd))
        batched.append(False)
    state_specs = [pl.BlockSpec((bg,) + a.shape[1:], lambda b, s, nd=a.ndim: (b,) + (0,) * (nd - 1))
                   for a in states]
    ins += list(states)
    in_specs += state_specs
    batched += [True] * (len(states) + 2 + len(states))

    def grouped(*refs):
        for bi in range(bg):
            kernel(*[r.at[bi] if is_b else r for r, is_b in zip(refs, batched)])

    out_struct = jax.ShapeDtypeStruct((bsz, n * L, out_width), out_dtype)
    res = pl.pallas_call(
        grouped,
        out_shape=[out_struct, out_struct] + [jax.ShapeDtypeStruct(a.shape, F32) for a in states],
        grid=(bsz // bg, n),
        in_specs=in_specs,
        out_specs=[pl.BlockSpec((bg, L, out_width), lambda b, s, d=d: (b, chunk[d](s), 0)) for d in (0, 1)]
        + state_specs,
        compiler_params=_cparams(("parallel", "arbitrary")),
        name=name,
    )(*ins)
    return (res[0].reshape(bsz * n * L, out_width), res[1].reshape(bsz * n * L, out_width)), list(res[2:])


def _bidir_scan(make_kernel, chunk, ins_lat, ins_ctx, consts, states0, *, bsz, seq, ctx_len, out_width, name,
                out_dtype=F32):
    kernel = functools.partial(make_kernel, L=chunk)
    o_ctx, st = _scan_call(kernel, bsz, ctx_len // chunk, chunk, ins_ctx, consts, states0, out_width,
                           name + "_ctx", out_dtype)
    o_lat, _ = _scan_call(kernel, bsz, seq // chunk, chunk, ins_lat, consts, st, out_width, name + "_lat",
                          out_dtype)
    return o_lat, o_ctx


def _s5_load_chunks(u_ref, kt):
    return jnp.concatenate([u_ref[pl.ds(t, kt, stride=S5_CHUNK), :].astype(BF16) for t in range(S5_CHUNK)],
                           axis=-1)


def _expand_blockdiag(c, rows_per_group, lanes_per_group, lane_period, n_out):
    r_n, k = c.shape
    kk = lax.broadcasted_iota(jnp.int32, (k, n_out), 0)
    jj = lax.broadcasted_iota(jnp.int32, (k, n_out), 1)
    sel = (kk // lanes_per_group == jj // lane_period) & (kk % lanes_per_group == jj % lanes_per_group)
    rr = lax.broadcasted_iota(jnp.int32, (r_n, n_out), 0)
    cc = lax.broadcasted_iota(jnp.int32, (r_n, n_out), 1)
    same = rr // rows_per_group == (cc % lane_period) // lanes_per_group
    return jnp.where(same, _dot(c, jnp.where(sel, 1.0, 0.0).astype(BF16)), 0.0).astype(BF16)


def _s5_state_kernel(u_ref, qc_ref, zfr_ref, zfi_ref, zbr_ref, zbi_ref, w_ref):
    kt = zfr_ref.shape[0]
    lt = S5_LANE_TILE

    @pl.when(pl.program_id(1) == 0)
    def _():
        for s in range(S5_CHUNK):
            w_ref[s * lt:(s + 1) * lt, :] = _expand_blockdiag(qc_ref[s], S5_GROUP, S5_STATE,
                                                              lt // S5_GROUP * S5_STATE, w_ref.shape[1])

    z = _dot(_s5_load_chunks(u_ref, kt), w_ref[...])
    w = z.shape[1] // 4
    for i, ref in enumerate((zfr_ref, zfi_ref, zbr_ref, zbi_ref)):
        ref[...] = z[:, i * w:(i + 1) * w]


def _s5_rec_kernel(*refs, bsz, nc_tiles, nl_tiles):
    zc, zl = refs[0:4], refs[4:8]
    pf_ref, pb_ref = refs[8:10]
    hc, hl = refs[10:14], refs[14:18]
    pw = ((pf_ref[0], pf_ref[1]), (pb_ref[0], pb_ref[1]))
    sub = lax.broadcasted_iota(jnp.int32, pw[0][0].shape, 0)

    def cmul(a, b):
        return a[0] * b[0] - a[1] * b[1], a[0] * b[1] + a[1] * b[0]

    def cadd(a, b):
        return a[0] + b[0], a[1] + b[1]

    def scan_tile(z, carry, d):
        p = pw[d]
        h = z
        for sh in (1, 2, 4):
            prow = sh - 1 if d == 0 else 8 - sh
            keep = (sub >= sh) if d == 0 else (sub < 8 - sh)
            shift = sh if d == 0 else 8 - sh
            moved = tuple(jnp.where(keep, pltpu.roll(v, shift, 0), 0.0) for v in h)
            h = cadd(h, cmul((p[0][prow:prow + 1, :], p[1][prow:prow + 1, :]), moved))
        full = cadd(h, cmul(p, carry))
        edge, shift, last = (0, 1, 7) if d == 0 else (7, 7, 0)
        before = tuple(jnp.where(sub == edge, c, pltpu.roll(v, shift, 0)) for v, c in zip(full, carry))
        return before, tuple(v[last:last + 1, :] for v in full)

    def walk(z_refs, h_refs, n_tiles, carries):
        def body(i, carries):
            out = []
            for b in range(bsz):
                for d in (0, 1):
                    t = i if d == 0 else n_tiles - 1 - i
                    r0 = pl.multiple_of((b * n_tiles + t) * 8, 8)
                    z = (z_refs[2 * d][pl.ds(r0, 8), :], z_refs[2 * d + 1][pl.ds(r0, 8), :])
                    before, new = scan_tile(z, carries[2 * b + d], d)
                    h_refs[2 * d][pl.ds(r0, 8), :] = before[0]
                    h_refs[2 * d + 1][pl.ds(r0, 8), :] = before[1]
                    out.append(new)
            return tuple(out)
        return lax.fori_loop(0, n_tiles, body, carries)

    zero = jnp.zeros((1, sub.shape[1]), F32)
    carries = walk(zc, hc, nc_tiles, tuple((zero, zero) for _ in range(2 * bsz)))
    walk(zl, hl, nl_tiles, carries)


def _s5_out_kernel(u_ref, hfr, hfi, hbr, hbi, bd_ref, rc_ref, y_ref, wm_ref, wr_ref):
    kt = hfr.shape[0]
    lt = S5_LANE_TILE

    @pl.when(pl.program_id(1) == 0)
    def _():
        for s in range(S5_CHUNK):
            for t in range(S5_CHUNK):
                wm_ref[s * lt:(s + 1) * lt, t * lt:(t + 1) * lt] = bd_ref[t - s + S5_CHUNK - 1]
        for i in range(4):
            wr_ref[i] = _expand_blockdiag(rc_ref[i], S5_STATE, S5_GROUP, lt, wr_ref.shape[2])

    y = _dot(_s5_load_chunks(u_ref, kt), wm_ref[...])
    for i, h_ref in enumerate((hfr, hfi, hbr, hbi)):
        y += _dot(h_ref[...].astype(BF16), wr_ref[i])
    w = y_ref.shape[1]
    for t in range(S5_CHUNK):
        y_ref[pl.ds(t, kt, stride=S5_CHUNK), :] = y[:, t * w:(t + 1) * w]


def _s5_matrices(lam_re, lam_im, log_step, b_re, b_im, c_re, c_im):
    hi = lax.Precision.HIGHEST
    L = S5_CHUNK
    step = jnp.exp(log_step)[..., None]
    n = jnp.arange(L + 1, dtype=F32)[:, None, None, None]
    mag = jnp.exp(n * (lam_re * step)[None])
    ang = n * (lam_im * step)[None]
    p_re, p_im = mag * jnp.cos(ang), mag * jnp.sin(ang)
    ab_re, ab_im = p_re[1], p_im[1]
    den = lam_re * lam_re + lam_im * lam_im
    coef_re = ((ab_re - 1.0) * lam_re + ab_im * lam_im) / den
    coef_im = (ab_im * lam_re - (ab_re - 1.0) * lam_im) / den
    e_re = p_re[:L] * coef_re - p_im[:L] * coef_im
    e_im = p_re[:L] * coef_im + p_im[:L] * coef_re
    ce_re = c_re[None] * e_re[..., None, :] - c_im[None] * e_im[..., None, :]
    ce_im = c_re[None] * e_im[..., None, :] + c_im[None] * e_re[..., None, :]
    kern = (jnp.einsum('ndgps,dgsq->ndgpq', ce_re, b_re, precision=hi)
            - jnp.einsum('ndgps,dgsq->ndgpq', ce_im, b_im, precision=hi))
    w_re, w_im = e_re[::-1], e_im[::-1]
    q_re = (w_re[..., None] * b_re[None] - w_im[..., None] * b_im[None]).transpose(1, 2, 0, 4, 3)
    q_im = (w_re[..., None] * b_im[None] + w_im[..., None] * b_re[None]).transpose(1, 2, 0, 4, 3)
    a_re, a_im = p_re[1:], p_im[1:]
    ca_re = c_re[None] * a_re[..., None, :] - c_im[None] * a_im[..., None, :]
    ca_im = c_re[None] * a_im[..., None, :] + c_im[None] * a_re[..., None, :]
    r_re = ca_re.transpose(1, 2, 4, 0, 3)
    r_im = (-ca_im).transpose(1, 2, 4, 0, 3)
    nl = L * jnp.arange(1, 9, dtype=F32)[:, None, None, None]
    t_mag = jnp.exp(nl * (lam_re * step)[None])
    t_ang = nl * (lam_im * step)[None]
    return kern, q_re, q_im, r_re, r_im, t_mag * jnp.cos(t_ang), t_mag * jnp.sin(t_ang)


def _s5_layer(u_lat, u_ctx, params, *, bsz, seq, ctx_len):
    c = u_lat.shape[1]
    L, pp, ns = S5_CHUNK, S5_GROUP, S5_STATE
    g = c // pp
    kern, q_re, q_im, r_re, r_im, t_re, t_im = _s5_matrices(*params)
    gl = S5_LANE_TILE // pp
    nb = g // gl
    lt = S5_LANE_TILE
    kw = L * lt
    sw = gl * ns
    lanes = g * ns
    kt_ = kern.transpose(0, 1, 2, 4, 3)
    lags = jnp.concatenate([kt_[:0:-1, 1], (kt_[0, 0] + kt_[0, 1])[None], kt_[1:, 0]], axis=0)
    eye = jnp.eye(gl, dtype=F32)
    bd = (lags.reshape(2 * L - 1, nb, gl, pp, pp).transpose(1, 0, 2, 3, 4)[:, :, :, :, None, :]
          * eye[None, None, :, None, :, None]).reshape(nb, 2 * L - 1, lt, lt).astype(BF16)
    qs = jnp.stack([q_re[0], q_im[0], q_re[1][:, ::-1], q_im[1][:, ::-1]])
    qc = qs.reshape(4, nb, gl, L, pp, ns).transpose(1, 3, 2, 4, 0, 5).reshape(nb, L, lt, 4 * ns).astype(BF16)
    rs = jnp.stack([r_re[0], r_im[0], r_re[1][:, :, ::-1], r_im[1][:, :, ::-1]])
    rc = rs.reshape(4, nb, gl, ns, L, pp).transpose(1, 0, 2, 3, 4, 5).reshape(nb, 4, sw, L * pp).astype(BF16)
    pf =jnp.stack([t_re[:, 0], t_im[:, 0]]).reshape(2, 8, lanes)
    pb = jnp.stack([t_re[::-1, 1], t_im[::-1, 1]]).reshape(2, 8, lanes)

    u_spec = lambda kt: pl.BlockSpec((kt * L, S5_LANE_TILE), lambda b, i: (i, b))
    h_spec = lambda kt: pl.BlockSpec((kt, sw), lambda b, i: (i, b))
    cp2 = _cparams(("parallel", "arbitrary"))

    def state(u):
        rows = u.shape[0] // L
        kt = min(S5_ROW_TILE, rows)
        return pl.pallas_call(
            _s5_state_kernel,
            out_shape=[jax.ShapeDtypeStruct((rows, lanes), F32)] * 4,
            grid=(nb, rows // kt),
            in_specs=[u_spec(kt), pl.BlockSpec((None, L, lt, 4 * ns), lambda b, i: (b, 0, 0, 0))],
            out_specs=[h_spec(kt)] * 4,
            scratch_shapes=[pltpu.VMEM((kw, 4 * sw), BF16)],
            compiler_params=cp2,
            name="s5_state",
        )(u, qc)

    z_c, z_l = state(u_ctx), state(u_lat)
    lb = S5_LANES
    rows_c, rows_l = z_c[0].shape[0], z_l[0].shape[0]
    h = pl.pallas_call(
        functools.partial(_s5_rec_kernel, bsz=bsz, nc_tiles=rows_c // bsz // 8, nl_tiles=rows_l // bsz // 8),
        out_shape=[jax.ShapeDtypeStruct((rows_c, lanes), F32)] * 4 + [jax.ShapeDtypeStruct((rows_l, lanes), F32)] * 4,
        grid=(lanes // lb,),
        in_specs=[pl.BlockSpec((rows_c, lb), lambda j: (0, j))] * 4
        + [pl.BlockSpec((rows_l, lb), lambda j: (0, j))] * 4
        + [pl.BlockSpec((2, 8, lb), lambda j: (0, 0, j))] * 2,
        out_specs=[pl.BlockSpec((rows_c, lb), lambda j: (0, j))] * 4
        + [pl.BlockSpec((rows_l, lb), lambda j: (0, j))] * 4,
        compiler_params=_cparams(("parallel",)),
        name="s5_recurrence",
    )(*z_c, *z_l, pf, pb)

    def readout(u, hs):
        rows = u.shape[0] // L
        kt = min(S5_ROW_TILE, rows)
        return pl.pallas_call(
            _s5_out_kernel,
            out_shape=jax.ShapeDtypeStruct(u.shape, F32),
            grid=(nb, rows // kt),
            in_specs=[u_spec(kt)] + [h_spec(kt)] * 4
            + [pl.BlockSpec((None, 2 * L - 1, lt, lt), lambda b, i: (b, 0, 0, 0)),
               pl.BlockSpec((None, 4, sw, L * pp), lambda b, i: (b, 0, 0, 0))],
            out_specs=u_spec(kt),
            scratch_shapes=[pltpu.VMEM((kw, kw), BF16), pltpu.VMEM((4, sw, kw), BF16)],
            compiler_params=cp2,
            name="s5_out",
        )(u, *hs, bd, rc)

    return readout(u_lat, h[4:]), readout(u_ctx, h[:4])


def _head_norm(o, g_ref, center, dh):
    outs = []
    for h in range(HEADS):
        oh = o[:, h * dh:(h + 1) * dh]
        if center:
            oh = oh - jnp.mean(oh, axis=-1, keepdims=True)
        outs.append(oh * lax.rsqrt(jnp.mean(oh * oh, axis=-1, keepdims=True) + EPS))
    return jnp.concatenate(outs, axis=-1) * g_ref[...]


def _sub2d(ref, h, cm=False):
    if ref.ndim == 2:
        n = ref.shape[0] // MERGE_SPLIT
        v = ref[h * n:(h + 1) * n, :]
    elif cm:
        n = ref.shape[1] // MERGE_SPLIT
        v = ref[:, h * n:(h + 1) * n, :]
    else:
        n = ref.shape[0] // MERGE_SPLIT
        v = ref[h * n:(h + 1) * n]
    return v.reshape(-1, v.shape[-1]).astype(F32)


def _finish(x_ref, a, b, wout_ref, mod_ref, gpost_ref, o_ref, n_r, h):
    half = a.shape[1]
    b = b.astype(BF16)
    if n_r:
        b = _dot(_perm_matrix(b.shape[0], n_r // MERGE_SPLIT, False), b).astype(BF16)
    y = _dot(a.astype(BF16), wout_ref[0:half, :]) + _dot(b, wout_ref[half:2 * half, :])
    out = _sub2d(x_ref, h) + mod_ref[5:6, :] * _rms(y) * gpost_ref[1:2, :]
    n = o_ref.shape[0] // MERGE_SPLIT
    o_ref[h * n:(h + 1) * n] = out.reshape((n,) + o_ref.shape[1:])


def _even_merge_kernel(x_ref, ga_ref, oaf_ref, oab_ref, gb_ref, obf_ref, obb_ref, na_ref, nb_ref,
                       wout_ref, mod_ref, gpost_ref, o_ref, *, dh, n_r):
    cm = bool(n_r)
    for h in range(MERGE_SPLIT):
        a = _silu(_sub2d(ga_ref, h)) * _head_norm(_sub2d(oaf_ref, h) + _sub2d(oab_ref, h), na_ref, False, dh)
        b = _sigmoid(_sub2d(gb_ref, h, cm)) * _head_norm(_sub2d(obf_ref, h, cm) + _sub2d(obb_ref, h, cm),
                                                         nb_ref, True, dh)
        _finish(x_ref, a, b, wout_ref, mod_ref, gpost_ref, o_ref, n_r, h)


def _odd_merge_kernel(x_ref, u_ref, ys_ref, gb_ref, obf_ref, obb_ref, sd_ref, wglu_ref, bglu_ref, nb_ref,
                      wout_ref, mod_ref, gpost_ref, o_ref, *, dh, n_r):
    cm = bool(n_r)
    for h in range(MERGE_SPLIT):
        y = _sub2d(ys_ref, h) + sd_ref[...] * _sub2d(u_ref, h)
        g = jax.nn.gelu(y)
        a = g * _sigmoid(_dot(g.astype(BF16), wglu_ref[...]) + bglu_ref[...])
        b = _silu(_sub2d(gb_ref, h, cm)) * _head_norm(_sub2d(obf_ref, h, cm) + _sub2d(obb_ref, h, cm),
                                                      nb_ref, True, dh)
        _finish(x_ref, a, b, wout_ref, mod_ref, gpost_ref, o_ref, n_r, h)


def _merge(kernel, s, rm_lat, cm_lat, rm_ctx, cm_ctx, consts, w_out, mod, gpost, *, bsz, seq, ctx_len, name):
    d = s.shape[1]
    half = d // 2
    rows = seq // GRID_W
    rt = min(MERGE_ROWS, rows)
    n_rh = rows // rt
    ncb = GRID_W // COL_TILE
    per_b = n_rh * ncb
    const_specs = [pl.BlockSpec(a.shape, lambda i, nd=a.ndim: (0,) * nd) for a in consts] \
        + [pl.BlockSpec((d, d), lambda i: (0, 0))]
    gpost_spec = pl.BlockSpec((3, d), lambda i: (0, 0))
    cp = _cparams(("parallel",))

    rm_idx = lambda i: ((i // per_b) * n_rh + (i % per_b) // ncb, i % ncb)
    cm_idx = lambda i: ((i // per_b) * ncb + i % ncb, (i % per_b) // ncb)
    s3 = s.reshape(s.shape[0] // GRID_W, GRID_W, d)
    ins = [s3] + [a.reshape(bsz * rows, GRID_W, a.shape[1]) for a, _ in rm_lat] \
        + [a.reshape(bsz * GRID_W, rows, a.shape[1]) for a, _ in cm_lat]
    in_specs = [pl.BlockSpec((rt, COL_TILE, d), lambda i: rm_idx(i) + (0,))] \
        + [pl.BlockSpec((rt, COL_TILE, half), lambda i, cb=cb: rm_idx(i) + (cb,)) for _, cb in rm_lat] \
        + [pl.BlockSpec((COL_TILE, rt, half), lambda i, cb=cb: cm_idx(i) + (cb,)) for _, cb in cm_lat]
    n_out_groups = s3.shape[0] if rm_ctx is not None else bsz * rows
    out = pl.pallas_call(
        functools.partial(kernel, n_r=rt),
        out_shape=jax.ShapeDtypeStruct((n_out_groups, GRID_W, d), F32),
        grid=(bsz * per_b,),
        in_specs=in_specs + const_specs
        + [pl.BlockSpec((None, N_MOD, d), lambda i: (i // per_b, 0, 0)), gpost_spec],
        out_specs=pl.BlockSpec((rt, COL_TILE, d), lambda i: rm_idx(i) + (0,)),
        compiler_params=cp,
        name=name + "_lat",
    )(*ins, *consts, w_out, mod, gpost)
    out = out.reshape(n_out_groups * GRID_W, d)
    if rm_ctx is None:
        return out

    tm = rt * COL_TILE
    t0 = bsz * seq // tm
    tiles = list(rm_ctx) + list(cm_ctx)
    n_in = 1 + len(tiles) + len(consts) + 3
    inner = functools.partial(kernel, n_r=0)
    return pl.pallas_call(
        lambda *refs: inner(*refs[:n_in], refs[n_in + 1]),
        out_shape=jax.ShapeDtypeStruct(out.shape, F32),
        grid=(bsz * ctx_len // tm,),
        in_specs=[pl.BlockSpec((tm, d), lambda i: (t0 + i, 0))]
        + [pl.BlockSpec((tm, half), lambda i, cb=cb: (i, cb)) for _, cb in tiles] + const_specs
        + [pl.BlockSpec((None, N_MOD, d), lambda i: (bsz, 0, 0)), gpost_spec,
           pl.BlockSpec(memory_space=pl.ANY)],
        out_specs=pl.BlockSpec((tm, d), lambda i: (t0 + i, 0)),
        input_output_aliases={n_in: 0},
        compiler_params=cp,
        name=name + "_ctx",
    )(s, *[a for a, _ in tiles], *consts, w_out, mod, gpost, out)


def kernel(x, c, ctx, c_ctx, w_mod, b_mod, norm_pre, norm_post, ffn_w_gate, ffn_w_up, ffn_w_down, ev_w_in, ev_w_out, gla_w_gate, gla_b_gate, gla_norm, ml_conv_w, ml_conv_b, ml_b_gates, ml_norm, od_w_in, od_w_out, s5_lam_re, s5_lam_im, s5_log_step, s5_b_re, s5_b_im, s5_c_re, s5_c_im, s5_d, s5_w_glu, s5_b_glu, ret_log_decay, ret_norm):
    bsz, seq, d = x.shape
    ctx_len = ctx.shape[1]
    depth = w_mod.shape[0]
    half = d // 2
    dh = half // HEADS
    gla_dk = dh // 2
    d_ff = ffn_w_gate.shape[-1]
    tm = TOKEN_TILE
    rows = seq // GRID_W
    assert seq % tm == 0 and (bsz * ctx_len) % tm == 0 and seq == rows * GRID_W and ctx_len % CHUNK == 0
    assert 2 * bsz == 8 and rows % 8 == 0 and seq % CHUNK == 0 and half == PROJ_TILE
    assert ctx_len % WIDE_CHUNK == 0 and seq % WIDE_CHUNK == 0
    assert ctx_len % (8 * S5_CHUNK) == 0 and seq % (8 * S5_CHUNK) == 0 and bsz % SCAN_BATCH == 0
    assert all((n // S5_CHUNK) % min(S5_ROW_TILE, n // S5_CHUNK) == 0 for n in (bsz * ctx_len, bsz * seq))

    tiles_x = bsz * seq // tm
    tiles_all = tiles_x + bsz * ctx_len // tm
    tiles_per_batch = seq // tm
    mod_of_tile = lambda i: jnp.minimum(i // tiles_per_batch, bsz)

    wg, wu, wd = ffn_w_gate.astype(BF16), ffn_w_up.astype(BF16), ffn_w_down.astype(BF16)

    cc = jnp.concatenate([c, c_ctx[None], jnp.zeros((8 - bsz - 1, d), F32)], axis=0)
    mod_all = _modulation(cc, w_mod, b_mod).reshape(depth, 8, N_MOD, d)

    s = None
    dims = dict(bsz=bsz, seq=seq, ctx_len=ctx_len)

    for layer in range(depth):
        need_ctx = layer < depth - 1
        mod = mod_all[layer]
        gpre, gpost = norm_pre[layer], norm_post[layer]
        ffn = functools.partial(_ffn, mod=mod, gpre=gpre, gpost=gpost, wg=wg, wu=wu, wd=wd, layer=layer,
                                mod_of_tile=mod_of_tile)
        if s is None:
            s = ffn(x.reshape(bsz * seq, d), which=0, sub=0, n_tiles=tiles_x, out_tiles=tiles_all)
            s = ffn(ctx.reshape(bsz * ctx_len, d), which=0, sub=0, n_tiles=tiles_all - tiles_x, out_tile0=tiles_x,
                    out_tiles=tiles_all, prev=s)
        else:
            s = ffn(s, which=0, sub=0, n_tiles=tiles_all)

        if layer % 2 == 0:
            e = layer // 2
            w = ev_w_in[e]
            qk_w = 2 * HEADS * gla_dk
            n_gate = 4 * HEADS
            c_low = qk_w + 2 * half
            c_ml = c_low + 2 * GLA_RANK
            c_g = c_ml + 4 * half
            w_in = jnp.concatenate([w[:, :c_low], w[:, c_ml:c_g]], axis=1).astype(BF16)
            w_small = jnp.concatenate([w[:, c_low:c_ml], jnp.zeros((d, SMALL_W - 2 * GLA_RANK), F32),
                                       w[:, c_g:], jnp.zeros((d, SMALL_W - n_gate), F32)], axis=1).astype(BF16)
            (pa, pb, psa, psb), (pa_c, pb_c, psa_c, psb_c) = _proj(s, mod, gpre, w_in, w_small,
                                                                   n_a=c_low // PROJ_TILE, **dims)

            st0 = [jnp.zeros((bsz, 2, HEADS, dh, gla_dk), F32)]
            gla_ins = lambda p, ps: [(p, qk_w // 2, 0), (p, qk_w // 2, qk_w // 2), (p, half, qk_w), (ps, SMALL_W, 0)]
            oa, oa_c = _bidir_scan(
                functools.partial(_gla_kernel, dk=gla_dk, dv=dh, gated=True,
                                  q_scale=float(gla_dk) ** -0.5, k_scale=1.0), CHUNK,
                gla_ins(pa, psa), gla_ins(pa_c, psa_c), [gla_w_gate[e].astype(BF16), gla_b_gate[e]], st0,
                out_width=half, name="gla", **dims)

            scale = jnp.concatenate([jnp.ones((half,), F32), jnp.full((half,), float(dh) ** -0.5, F32)])
            qk = _conv(pb, ml_conv_w[e], ml_conv_b[e], scale, bsz=bsz, n=seq)
            qk_c = _conv(pb_c, ml_conv_w[e], ml_conv_b[e], scale, bsz=bsz, n=ctx_len)
            bias = jnp.zeros((1, SMALL_W), F32).at[0, :n_gate].set(ml_b_gates[e].reshape(-1))
            ml_ins = lambda qk_, p, ps: [(qk_, half, 0), (qk_, half, half), (p, half, 2 * half), (ps, SMALL_W, 0)]
            st0 = [jnp.zeros((bsz, 2, HEADS, dh, dh), F32), jnp.zeros((bsz, 2, HEADS, 8, dh), F32)]
            ob, ob_c = _bidir_scan(
                functools.partial(_mlstm_kernel, dh=dh), WIDE_CHUNK,
                ml_ins(qk, pb, psb), ml_ins(qk_c, pb_c, psb_c), [bias], st0,
                out_width=half, name="mlstm", out_dtype=BF16, **dims)

            ga_cb, gb_cb = (qk_w + half) // half, 3
            s = _merge(
                functools.partial(_even_merge_kernel, dh=dh), s,
                [(pa, ga_cb), (oa[0], 0), (oa[1], 0)], [(pb, gb_cb), (ob[0], 0), (ob[1], 0)],
                [(pa_c, ga_cb), (oa_c[0], 0), (oa_c[1], 0)] if need_ctx else None,
                [(pb_c, gb_cb), (ob_c[0], 0), (ob_c[1], 0)],
                [gla_norm[e].reshape(1, half), ml_norm[e].reshape(1, half)],
                ev_w_out[e].astype(BF16), mod, gpost, name="even_merge", **dims)
        else:
            o = layer // 2
            (pa, pb), (pa_c, pb_c) = _proj(s, mod, gpre, od_w_in[o].astype(BF16), None, n_a=1, **dims)

            ys, ys_c = _s5_layer(pa, pa_c, (s5_lam_re[o], s5_lam_im[o], s5_log_step[o], s5_b_re[o], s5_b_im[o],
                                            s5_c_re[o], s5_c_im[o]), **dims)

            lg = jnp.repeat(-jnp.exp(ret_log_decay[o]), dh, axis=1)
            ret_ins = lambda p: [(p, half, 0), (p, half, half), (p, half, 2 * half)]
            st0 = [jnp.zeros((bsz, 2, HEADS, dh, dh), F32)]
            od, od_c = _bidir_scan(
                functools.partial(_gla_kernel, dk=dh, dv=dh, gated=False,
                                  q_scale=1.0, k_scale=float(dh) ** -0.5), WIDE_CHUNK,
                ret_ins(pb), ret_ins(pb_c), [lg], st0, out_width=half, name="retention", out_dtype=BF16, **dims)

            s = _merge(
                functools.partial(_odd_merge_kernel, dh=dh), s,
                [(pa, 0), (ys, 0)], [(pb, 3), (od[0], 0), (od[1], 0)],
                [(pa_c, 0), (ys_c, 0)] if need_ctx else None, [(pb_c, 3), (od_c[0], 0), (od_c[1], 0)],
                [s5_d[o].reshape(1, half), s5_w_glu[o].astype(BF16), s5_b_glu[o].reshape(1, half),
                 ret_norm[o].reshape(1, half)],
                od_w_out[o].astype(BF16), mod, gpost, name="odd_merge", **dims)

        s = ffn(s, which=1, sub=2, n_tiles=tiles_all if need_ctx else tiles_x)

    return s[:bsz * seq].reshape(bsz, seq, d)
```
